```python
import math
import jax, jax.numpy as jnp
from jax import lax
import numpy as np

D_MODEL = 1024
BATCH = 4
SEQ = 4096
DEPTH = 4

SSD_D_INNER = 1024
SSD_HEAD_DIM = 64
SSD_HEADS = SSD_D_INNER // SSD_HEAD_DIM
SSD_GROUPS = 2
SSD_STATE = 128
SSD_CONV = 4
SSD_CHUNK = 128
SSD_CONV_DIM = SSD_D_INNER + 2 * SSD_GROUPS * SSD_STATE

ATTN_HEAD_DIM = 128
ATTN_HEADS_PER_GROUP = 4
DILATED_GROUPS = ((128, 1), (512, 4), (2048, 16))
ATTN_HEADS = ATTN_HEADS_PER_GROUP * len(DILATED_GROUPS)
ATTN_WIDTH = ATTN_HEADS * ATTN_HEAD_DIM
ATTN_OUT_WIDTH = ATTN_HEADS_PER_GROUP * ATTN_HEAD_DIM

POOL_WINDOWS = (2, 4, 8, 16)
POOL_WIDTH = 1024
POOL_GROUP_WIDTH = POOL_WIDTH // len(POOL_WINDOWS)

N_BRANCHES = 3
D_FF = 4 * D_MODEL
EPS = 1e-6
IN_SIZES = (SSD_D_INNER, SSD_CONV_DIM, SSD_HEADS, ATTN_WIDTH, ATTN_WIDTH, ATTN_WIDTH, POOL_WIDTH, N_BRANCHES * D_MODEL)
IN_WIDTH = SSD_D_INNER + SSD_CONV_DIM + SSD_HEADS + 3 * ATTN_WIDTH + POOL_WIDTH + N_BRANCHES * D_MODEL

kernel_name = "hybrid_ssd_dilattn_pool_gated"


def _rms(x, w):
    xf = x.astype(jnp.float32)
    y = xf * lax.rsqrt(jnp.mean(xf * xf, axis=-1, keepdims=True) + EPS)
    return (y * w.astype(jnp.float32)).astype(x.dtype)


def _alibi_slopes(n):
    def pow2(k):
        start = 2.0 ** (-8.0 / k)
        return [start ** (i + 1) for i in range(k)]
    if math.log2(n).is_integer():
        s = pow2(n)
    else:
        c = 2 ** math.floor(math.log2(n))
        s = pow2(c) + pow2(2 * c)[0::2][: n - c]
    return np.sort(np.asarray(s, np.float32))[::-1].copy()


def _causal_dwconv(x, w, bias):
    ch = x.shape[-1]
    y = lax.conv_general_dilated(x, w[:, None, :].astype(x.dtype), window_strides=(1,),
                                 padding=[(SSD_CONV - 1, 0)],
                                 dimension_numbers=('NWC', 'WIO', 'NWC'),
                                 feature_group_count=ch)
    return y + bias


def _ssd(xs, dt, A, B, C, D):
    b, S, H, P = xs.shape
    G, N, L = SSD_GROUPS, SSD_STATE, SSD_CHUNK
    J = H // G
    nc = S // L
    xc = xs.reshape(b, nc, L, G, J, P)
    X = xc * dt.reshape(b, nc, L, G, J)[..., None]
    a_cum = jnp.cumsum((dt * A).reshape(b, nc, L, G, J), axis=2)
    Bc = B.reshape(b, nc, L, G, N)
    Cc = C.reshape(b, nc, L, G, N)
    seg = a_cum[:, :, :, None] - a_cum[:, :, None, :]
    causal = jnp.tril(jnp.ones((L, L), bool))[:, :, None, None]
    decay = jnp.exp(jnp.where(causal, seg, -jnp.inf))
    cb = jnp.einsum('bclgn,bcsgn->bclsg', Cc, Bc)
    y_diag = jnp.einsum('bclsgj,bcsgjp->bclgjp', cb[..., None] * decay, X)
    Xd = X * jnp.exp(a_cum[:, :, -1:] - a_cum)[..., None]
    states = jnp.einsum('bclgn,bclgjp->bcgjpn', Bc, Xd)
    chunk_decay = jnp.exp(a_cum[:, :, -1])

    def step(h, inp):
        st, dec = inp
        return dec[..., None, None] * h + st, h

    h0 = jnp.zeros((b, G, J, P, N), X.dtype)
    _, prev = lax.scan(step, h0, (jnp.moveaxis(states, 1, 0), jnp.moveaxis(chunk_decay, 1, 0)))
    prev = jnp.moveaxis(prev, 0, 1)
    y_off = jnp.einsum('bclgn,bcgjpn->bclgjp', Cc, prev) * jnp.exp(a_cum)[..., None]
    y = y_diag + y_off + D.reshape(G, J)[:, :, None] * xc
    return y.reshape(b, S, H * P)


def _dilated_window_attn(q, k, v, steps, dilation, slopes):
    b, S, h, e = q.shape
    n = S // dilation
    nb = -(-n // steps)
    n_pad = nb * steps

    def to_sub(t):
        t = t.reshape(b, n, dilation, h, e).transpose(0, 3, 2, 1, 4)
        return jnp.pad(t, ((0, 0), (0, 0), (0, 0), (0, n_pad - n), (0, 0)))

    qs, ks, vs = to_sub(q), to_sub(k), to_sub(v)
    qb = qs.reshape(b, h, dilation, nb, steps, e)

    def band(t):
        tp = jnp.pad(t, ((0, 0), (0, 0), (0, 0), (steps, 0), (0, 0)))
        prev = tp[:, :, :, :n_pad].reshape(b, h, dilation, nb, steps, e)
        cur = t.reshape(b, h, dilation, nb, steps, e)
        return jnp.concatenate([prev, cur], axis=4)

    kb, vb = band(ks), band(vs)
    s = jnp.einsum('bhdnqe,bhdnke->bhdnqk', qb, kb) * (e ** -0.5)
    qi = jnp.arange(steps)[:, None]
    kj = jnp.arange(2 * steps)[None, :]
    rel = qi + steps - kj
    blk = jnp.arange(nb)[:, None, None]
    valid = (rel >= 0) & (rel <= steps) & (blk * steps + kj - steps >= 0)
    alibi = -slopes[:, None, None] * (rel * dilation).astype(jnp.float32)
    s = jnp.where(valid, s + alibi[:, None, None], -jnp.inf)
    lse = jax.nn.logsumexp(s, axis=-1)
    p = jnp.exp(s - lse[..., None])
    o = jnp.einsum('bhdnqk,bhdnke->bhdnqe', p, vb)
    o = o.reshape(b, h, dilation, n_pad, e)[:, :, :, :n].transpose(0, 3, 2, 1, 4).reshape(b, S, h, e)
    lse = lse.reshape(b, h, dilation, n_pad)[:, :, :, :n].transpose(0, 3, 2, 1).reshape(b, S, h)
    return o, lse


def _multi_scale_pool(u, w_mix):
    b, S, _ = u.shape
    ug = u.reshape(b, S, len(POOL_WINDOWS), POOL_GROUP_WIDTH)
    cs = jnp.cumsum(ug, axis=1)
    t = jnp.arange(S)
    outs = []
    for gi, w in enumerate(POOL_WINDOWS):
        csg = cs[:, :, gi]
        shifted = jnp.pad(csg, ((0, 0), (w, 0), (0, 0)))[:, :S]
        count = jnp.minimum(t + 1, w).astype(jnp.float32)[None, :, None]
        outs.append((csg - shifted) / count - ug[:, :, gi])
    pooled = jnp.stack(outs, axis=2)
    return jnp.einsum('bsgi,gio->bsgo', pooled, w_mix).reshape(b, S, POOL_WIDTH)


def _hybrid_mixer(h, w_in, conv_w, conv_b, dt_bias, a_log, d_skip, ssd_norm_w, w_ssd_out,
                  q_norm_w, k_norm_w, w_attn_out, w_pool_mix, pool_scale, w_pool_out, w_out, slopes):
    f32 = jnp.float32
    dtype = h.dtype
    b, S, _ = h.shape
    splits = np.cumsum(IN_SIZES)[:-1].tolist()
    z, xbc, dt_raw, q, k, v, u, gates = jnp.split(h @ w_in, splits, axis=-1)

    xbc = jax.nn.silu(_causal_dwconv(xbc, conv_w, conv_b)).astype(f32)
    xs, Bm, Cm = jnp.split(xbc, [SSD_D_INNER, SSD_D_INNER + SSD_GROUPS * SSD_STATE], axis=-1)
    dt = jax.nn.softplus(dt_raw.astype(f32) + dt_bias.astype(f32))
    A = -jnp.exp(a_log.astype(f32))
    y = _ssd(xs.reshape(b, S, SSD_HEADS, SSD_HEAD_DIM), dt, A,
             Bm.reshape(b, S, SSD_GROUPS, SSD_STATE), Cm.reshape(b, S, SSD_GROUPS, SSD_STATE),
             d_skip.astype(f32))
    y = (y * jax.nn.silu(z.astype(f32))).reshape(b, S, SSD_GROUPS, SSD_D_INNER // SSD_GROUPS)
    y = (y * lax.rsqrt(jnp.mean(y * y, axis=-1, keepdims=True) + EPS)).reshape(b, S, SSD_D_INNER)
    y_ssd = (y * ssd_norm_w.astype(f32)).astype(dtype) @ w_ssd_out

    qh = _rms(q.reshape(b, S, ATTN_HEADS, ATTN_HEAD_DIM).astype(f32), q_norm_w)
    kh = _rms(k.reshape(b, S, ATTN_HEADS, ATTN_HEAD_DIM).astype(f32), k_norm_w)
    vh = v.reshape(b, S, ATTN_HEADS, ATTN_HEAD_DIM).astype(f32)
    outs, lses = [], []
    for gi, (win, dil) in enumerate(DILATED_GROUPS):
        sl = slice(gi * ATTN_HEADS_PER_GROUP, (gi + 1) * ATTN_HEADS_PER_GROUP)
        o, lse = _dilated_window_attn(qh[:, :, sl], kh[:, :, sl], vh[:, :, sl], win // dil, dil, slopes[gi])
        outs.append(o)
        lses.append(lse)
    wts = jax.nn.softmax(jnp.stack(lses, axis=0), axis=0)
    o = jnp.sum(wts[..., None] * jnp.stack(outs, axis=0), axis=0).reshape(b, S, ATTN_OUT_WIDTH)
    y_attn = o.astype(dtype) @ w_attn_out

    y_pool = (_multi_scale_pool(u.astype(f32), w_pool_mix) * pool_scale).astype(dtype) @ w_pool_out

    g_ssd, g_attn, g_pool = jnp.split(jax.nn.sigmoid(gates.astype(f32)).astype(dtype), N_BRANCHES, axis=-1)
    return (g_ssd * y_ssd + g_attn * y_attn + g_pool * y_pool) @ w_out


def setup_inputs(seed: int = 0) -> dict:
    key = jax.random.key(seed)
    ks = jax.random.split(key, 24)
    n = jax.random.normal
    L, D = DEPTH, D_MODEL
    u01 = jax.random.uniform(ks[9], (L, SSD_HEADS))
    dt0 = jnp.exp(u01 * (math.log(0.1) - math.log(0.001)) + math.log(0.001))
    return {
        "x": n(ks[0], (BATCH, SEQ, D), jnp.float32),
        "c": n(ks[1], (BATCH, D), jnp.float32),
        "w_ada": n(ks[2], (L, D, 6 * D)) * D ** -0.5,
        "b_ada": n(ks[3], (L, 6 * D)) * 0.01,
        "norm1_w": 1.0 + 0.05 * n(ks[4], (L, D)),
        "norm2_w": 1.0 + 0.05 * n(ks[5], (L, D)),
        "w_in": n(ks[6], (L, D, IN_WIDTH)) * D ** -0.5,
        "conv_w": n(ks[7], (L, SSD_CONV, SSD_CONV_DIM)) * SSD_CONV ** -0.5,
        "conv_b": n(ks[8], (L, SSD_CONV_DIM)) * 0.01,
        "dt_bias": dt0 + jnp.log(-jnp.expm1(-dt0)),
        "a_log": jnp.log(jax.random.uniform(ks[10], (L, SSD_HEADS), minval=1.0, maxval=16.0)),
        "d_skip": 1.0 + 0.05 * n(ks[11], (L, SSD_HEADS)),
        "ssd_norm_w": 1.0 + 0.05 * n(ks[12], (L, SSD_D_INNER)),
        "w_ssd_out": n(ks[13], (L, SSD_D_INNER, D)) * SSD_D_INNER ** -0.5,
        "q_norm_w": 1.0 + 0.05 * n(ks[14], (L, ATTN_HEAD_DIM)),
        "k_norm_w": 1.0 + 0.05 * n(ks[15], (L, ATTN_HEAD_DIM)),
        "w_attn_out": n(ks[16], (L, ATTN_OUT_WIDTH, D)) * ATTN_OUT_WIDTH ** -0.5,
        "w_pool_mix": n(ks[17], (L, len(POOL_WINDOWS), POOL_GROUP_WIDTH, POOL_GROUP_WIDTH)) * POOL_GROUP_WIDTH ** -0.5,
        "pool_scale": 1.0 + 0.05 * n(ks[18], (L, POOL_WIDTH)),
        "w_pool_out": n(ks[19], (L, POOL_WIDTH, D)) * POOL_WIDTH ** -0.5,
        "w_out": n(ks[20], (L, D, D)) * D ** -0.5,
        "w_ff1": n(ks[21], (L, D, D_FF)) * D ** -0.5,
        "w_ff2": n(ks[22], (L, D_FF, D)) * D_FF ** -0.5,
    }


def reference(x, c, w_ada, b_ada, norm1_w, norm2_w, w_in, conv_w, conv_b, dt_bias, a_log, d_skip,
              ssd_norm_w, w_ssd_out, q_norm_w, k_norm_w, w_attn_out, w_pool_mix, pool_scale,
              w_pool_out, w_out, w_ff1, w_ff2):
    cond = jax.nn.silu(c)
    slopes = jnp.asarray(_alibi_slopes(ATTN_HEADS)).reshape(len(DILATED_GROUPS), ATTN_HEADS_PER_GROUP)
    for l in range(DEPTH):
        mod = (cond @ w_ada[l] + b_ada[l])[:, None, :]
        sh1, sc1, g1, sh2, sc2, g2 = jnp.split(mod, 6, axis=-1)
        h = _rms(x, norm1_w[l]) * (1 + sc1) + sh1
        x = x + g1 * _hybrid_mixer(h, w_in[l], conv_w[l], conv_b[l], dt_bias[l], a_log[l], d_skip[l],
                                   ssd_norm_w[l], w_ssd_out[l], q_norm_w[l], k_norm_w[l], w_attn_out[l],
                                   w_pool_mix[l], pool_scale[l], w_pool_out[l], w_out[l], slopes)
        h = _rms(x, norm2_w[l]) * (1 + sc2) + sh2
        x = x + g2 * (jnp.square(jax.nn.relu(h @ w_ff1[l])) @ w_ff2[l])
    return x
```

```python
import functools
import math

import numpy as np
import jax
import jax.numpy as jnp
from jax import lax
from jax.experimental import pallas as pl
from jax.experimental.pallas import tpu as pltpu

f32 = jnp.float32
bf16 = jnp.bfloat16

D_MODEL = 1024
SSD_D_INNER = 1024
SSD_HEAD_DIM = 64
SSD_HEADS = SSD_D_INNER // SSD_HEAD_DIM
SSD_GROUPS = 2
SSD_STATE = 128
SSD_CONV = 4
SSD_CHUNK = 128
SSD_BC = 2 * SSD_GROUPS * SSD_STATE
ATTN_HEAD_DIM = 128
ATTN_HEADS_PER_GROUP = 4
DILATED_GROUPS = ((128, 1), (512, 4), (2048, 16))
ATTN_HEADS = ATTN_HEADS_PER_GROUP * len(DILATED_GROUPS)
ATTN_WIDTH = ATTN_HEADS * ATTN_HEAD_DIM
ATTN_OUT_WIDTH = ATTN_HEADS_PER_GROUP * ATTN_HEAD_DIM
POOL_WINDOWS = (2, 4, 8, 16)
POOL_WIDTH = 1024
POOL_GROUP_WIDTH = POOL_WIDTH // len(POOL_WINDOWS)
N_BRANCHES = 3
D_FF = 4 * D_MODEL
EPS = 1e-6

LANES = 128
SUBLANES = 8
VMEM_LIMIT_BYTES = 56 * 1024 * 1024

COL_Z, COL_XS, COL_U, COL_GATES = 0, 8, 16, 24
COL_BC, COL_Q, COL_K, COL_V, COL_DT = 48, 52, 64, 76, 88
PROJ_TILE_N = 1280
PROJ_WIDTH = 9 * PROJ_TILE_N
POOL_HALO = 16
CONV_HALO = 8


def _alibi_slopes(n):
    def pow2(k):
        start = 2.0 ** (-8.0 / k)
        return [start ** (i + 1) for i in range(k)]
    if math.log2(n).is_integer():
        s = pow2(n)
    else:
        c = 2 ** math.floor(math.log2(n))
        s = pow2(c) + pow2(2 * c)[0::2][: n - c]
    return np.sort(np.asarray(s, np.float32))[::-1].copy()


def _cparams(sem):
    return pltpu.CompilerParams(dimension_semantics=sem, vmem_limit_bytes=VMEM_LIMIT_BYTES)


def _silu(x):
    return x * jax.nn.sigmoid(x)


def _adaln_kernel(c_ref, w_ref, b_ref, o_ref):
    cond = _silu(c_ref[...])
    o_ref[0] = jnp.dot(cond, w_ref[0], preferred_element_type=f32,
                       precision=lax.Precision.HIGHEST) + b_ref[0]


def _adaln(c, w_ada, b_ada):
    depth, d, n = w_ada.shape
    rows = SUBLANES * pl.cdiv(c.shape[0], SUBLANES)
    c_pad = jnp.pad(c, ((0, rows - c.shape[0]), (0, 0)))
    tn = n // 4
    return pl.pallas_call(
        _adaln_kernel,
        out_shape=jax.ShapeDtypeStruct((depth, rows, n), f32),
        grid=(depth, n // tn),
        in_specs=[pl.BlockSpec((rows, d), lambda l, j: (0, 0)),
                  pl.BlockSpec((1, d, tn), lambda l, j: (l, 0, j)),
                  pl.BlockSpec((1, 1, tn), lambda l, j: (l, 0, j))],
        out_specs=pl.BlockSpec((1, rows, tn), lambda l, j: (l, 0, j)),
        compiler_params=_cparams(("parallel", "parallel")),
        name="adaln",
    )(c_pad, w_ada, b_ada.reshape(depth, 1, n))


def _norm_mod_kernel(x_ref, nw_ref, sc_ref, sh_ref, h_ref):
    x = x_ref[0]
    y = x * lax.rsqrt(jnp.mean(x * x, axis=-1, keepdims=True) + EPS) * nw_ref[...]
    h_ref[0] = (y * (1.0 + sc_ref[0]) + sh_ref[0]).astype(h_ref.dtype)


def _norm_mod(x, nw, sc, sh, ts=1024):
    b, s, d = x.shape
    return pl.pallas_call(
        _norm_mod_kernel,
        out_shape=jax.ShapeDtypeStruct((b, s, d), bf16),
        grid=(b, s // ts),
        in_specs=[pl.BlockSpec((1, ts, d), lambda i, j: (i, j, 0)),
                  pl.BlockSpec((1, d), lambda i, j: (0, 0)),
                  pl.BlockSpec((1, 1, d), lambda i, j: (i, 0, 0)),
                  pl.BlockSpec((1, 1, d), lambda i, j: (i, 0, 0))],
        out_specs=pl.BlockSpec((1, ts, d), lambda i, j: (i, j, 0)),
        compiler_params=_cparams(("parallel", "parallel")),
        name="norm_mod",
    )(x, nw.reshape(1, d), sc, sh)


def _matmul_kernel(a_ref, w_ref, o_ref):
    o_ref[...] = jnp.dot(a_ref[...], w_ref[...], preferred_element_type=f32).astype(o_ref.dtype)


def _matmul(a, w, tm, tn, out_dtype):
    t, k = a.shape
    n = w.shape[1]
    return pl.pallas_call(
        _matmul_kernel,
        out_shape=jax.ShapeDtypeStruct((t, n), out_dtype),
        grid=(t // tm, n // tn),
        in_specs=[pl.BlockSpec((tm, k), lambda i, j: (i, 0)),
                  pl.BlockSpec((k, tn), lambda i, j: (0, j))],
        out_specs=pl.BlockSpec((tm, tn), lambda i, j: (i, j)),
        compiler_params=_cparams(("parallel", "parallel")),
        name="in_proj",
    )(a, w)


def _expand_heads(v):
    lane = lax.broadcasted_iota(jnp.int32, (v.shape[0], LANES), 1)
    tiles = [jnp.where(lane < SSD_HEAD_DIM, v[:, 2 * p:2 * p + 1], v[:, 2 * p + 1:2 * p + 2])
             for p in range(SSD_HEADS // 2)]
    return jnp.concatenate(tiles, axis=1)


def _ssd_kernel(z_ref, xs_ref, bc_ref, dt_ref, cwx_ref, cwbc_ref, cbx_ref, cbbc_ref, dtb_ref,
                alog_ref, dsk_ref, nw_ref, o_ref, xext, bcext, state):
    L = SSD_CHUNK
    N = SSD_STATE
    GW = SSD_D_INNER // SSD_GROUPS

    @pl.when(pl.program_id(1) == 0)
    def _():
        xext[0:CONV_HALO, :] = jnp.zeros((CONV_HALO, SSD_D_INNER), f32)
        bcext[0:CONV_HALO, :] = jnp.zeros((CONV_HALO, SSD_BC), f32)
        state[...] = jnp.zeros_like(state)

    xext[CONV_HALO:CONV_HALO + L, :] = xs_ref[0]
    bcext[CONV_HALO:CONV_HALO + L, :] = bc_ref[0]

    def conv(ext, w_ref, b_ref):
        acc = b_ref[...]
        for k in range(SSD_CONV):
            off = CONV_HALO - (SSD_CONV - 1) + k
            acc = acc + ext[pl.ds(off, L), :] * w_ref[k:k + 1, :]
        return _silu(acc)

    xs = conv(xext, cwx_ref, cbx_ref)
    bc = conv(bcext, cwbc_ref, cbbc_ref)
    xext[0:CONV_HALO, :] = xext[L:L + CONV_HALO, :]
    bcext[0:CONV_HALO, :] = bcext[L:L + CONV_HALO, :]

    dtr = dt_ref[0] + dtb_ref[...]
    dt = jnp.maximum(dtr, 0.0) + jnp.log1p(jnp.exp(-jnp.abs(dtr)))
    a = dt * (-jnp.exp(alog_ref[...]))
    ti = lax.broadcasted_iota(jnp.int32, (L, L), 0)
    si = lax.broadcasted_iota(jnp.int32, (L, L), 1)
    causal = ti >= si
    a_cum = jnp.dot(causal.astype(f32), a, preferred_element_type=f32,
                    precision=lax.Precision.HIGHEST)
    a_cum_t = a_cum.T

    dt_e = _expand_heads(dt)
    acum_e = _expand_heads(a_cum)
    alast_e = acum_e[L - 1:L, :]
    x_dt = xs * dt_e
    x_dec = (x_dt * jnp.exp(alast_e - acum_e)).astype(bf16)
    x_dt_b = x_dt.astype(bf16)
    e_acum = jnp.exp(acum_e)
    e_alast = jnp.exp(alast_e)
    lane = lax.broadcasted_iota(jnp.int32, (L, LANES), 1)
    lo = lane < SSD_HEAD_DIM
    zero_b = jnp.zeros((L, LANES), bf16)

    ys = []
    for g in range(SSD_GROUPS):
        b_g = bc[:, g * N:(g + 1) * N]
        c_g = bc[:, (SSD_GROUPS + g) * N:(SSD_GROUPS + g + 1) * N].astype(bf16)
        cb = lax.dot_general(c_g, b_g.astype(bf16), (((1,), (1,)), ((), ())),
                             preferred_element_type=f32)
        st = state[g]
        y_off = jnp.dot(c_g, st.astype(bf16), preferred_element_type=f32) * e_acum[:, g * GW:(g + 1) * GW]
        new = jnp.dot(b_g.T.astype(bf16), x_dec[:, g * GW:(g + 1) * GW], preferred_element_type=f32)
        state[g] = e_alast[:, g * GW:(g + 1) * GW] * st + new
        tiles = []
        for p in range(GW // LANES):
            col0 = g * GW + p * LANES
            xp = x_dt_b[:, col0:col0 + LANES]
            acc = None
            for half in range(2):
                h = col0 // SSD_HEAD_DIM + half
                seg = a_cum[:, h:h + 1] - a_cum_t[h:h + 1, :]
                m_h = (cb * jnp.exp(jnp.where(causal, seg, -jnp.inf))).astype(bf16)
                x_h = jnp.where(lo if half == 0 else jnp.logical_not(lo), xp, zero_b)
                part = jnp.dot(m_h, x_h, preferred_element_type=f32)
                acc = part if acc is None else acc + part
            tiles.append(acc)
        ys.append(jnp.concatenate(tiles, axis=1) + y_off)
    y = jnp.concatenate(ys, axis=1) + dsk_ref[...] * xs
    y = y * _silu(z_ref[0])
    outs = []
    for g in range(SSD_GROUPS):
        yg = y[:, g * GW:(g + 1) * GW]
        outs.append(yg * lax.rsqrt(jnp.mean(yg * yg, axis=-1, keepdims=True) + EPS))
    o_ref[0] = (jnp.concatenate(outs, axis=1) * nw_ref[...]).astype(o_ref.dtype)


def _pad_lanes(v):
    return jnp.pad(v.astype(f32), (0, LANES - v.shape[0])).reshape(1, LANES)


def _ssd(proj, conv_w, conv_b, dt_bias, a_log, d_skip, ssd_norm_w):
    b, s, _ = proj.shape
    L = SSD_CHUNK
    di = SSD_D_INNER
    row = lambda v: v.reshape(1, -1)
    full = lambda shape: pl.BlockSpec(shape, lambda i, j: (0,) * len(shape))
    return pl.pallas_call(
        _ssd_kernel,
        out_shape=jax.ShapeDtypeStruct((b, s, di), bf16),
        grid=(b, s // L),
        in_specs=[pl.BlockSpec((1, L, di), lambda i, j: (i, j, COL_Z * LANES // di)),
                  pl.BlockSpec((1, L, di), lambda i, j: (i, j, COL_XS * LANES // di)),
                  pl.BlockSpec((1, L, SSD_BC), lambda i, j: (i, j, COL_BC * LANES // SSD_BC)),
                  pl.BlockSpec((1, L, LANES), lambda i, j: (i, j, COL_DT)),
                  full((SSD_CONV, di)), full((SSD_CONV, SSD_BC)), full((1, di)), full((1, SSD_BC)),
                  full((1, LANES)), full((1, LANES)), full((1, di)), full((1, di))],
        out_specs=pl.BlockSpec((1, L, di), lambda i, j: (i, j, 0)),
        scratch_shapes=[pltpu.VMEM((L + CONV_HALO, di), f32),
                        pltpu.VMEM((L + CONV_HALO, SSD_BC), f32),
                        pltpu.VMEM((SSD_GROUPS, SSD_STATE, di // SSD_GROUPS), f32)],
        compiler_params=_cparams(("parallel", "arbitrary")),
        name="ssd",
    )(proj, proj, proj, proj, conv_w[:, :di], conv_w[:, di:], row(conv_b[:di]), row(conv_b[di:]),
      _pad_lanes(dt_bias), _pad_lanes(a_log), row(jnp.repeat(d_skip, SSD_HEAD_DIM)), row(ssd_norm_w))


def _attn_group(gi, dil, slope, qn, kn, v_ref, acc, mrun, lrun):
    seq = qn.shape[0]
    steps = DILATED_GROUPS[gi][0] // dil
    e = ATTN_HEAD_DIM
    nb = seq // (dil * steps)
    scale = e ** -0.5
    qi = lax.broadcasted_iota(jnp.int32, (steps, steps), 0)
    kj = lax.broadcasted_iota(jnp.int32, (steps, steps), 1)
    neg = jnp.float32(-jnp.inf)
    bias_c = jnp.where(kj <= qi, (-slope * dil) * (qi - kj).astype(f32), neg)
    bias_p = jnp.where(kj >= qi, (-slope * dil) * (qi + steps - kj).astype(f32), neg)
    nt = (((1,), (1,)), ((), ()))

    def body(it, carry):
        r = it // nb
        i = it % nb
        start = r + i * (steps * dil)
        pstart = jnp.maximum(start - steps * dil, 0)
        if dil == 1:
            rows = pl.ds(pl.multiple_of(start, steps), steps)
            prows = pl.ds(pl.multiple_of(pstart, steps), steps)
        else:
            rows = pl.ds(start, steps, stride=dil)
            prows = pl.ds(pstart, steps, stride=dil)
        qb = qn[rows, :].astype(bf16)
        kc = kn[rows, :].astype(bf16)
        kp = kn[prows, :].astype(bf16)
        vc = v_ref[0, rows, :].astype(bf16)
        vp = v_ref[0, prows, :].astype(bf16)
        s_c = lax.dot_general(qb, kc, nt, preferred_element_type=f32) * scale + bias_c
        s_p = lax.dot_general(qb, kp, nt, preferred_element_type=f32) * scale + jnp.where(i > 0, bias_p, neg)
        m_blk = jnp.maximum(jnp.max(s_c, axis=-1, keepdims=True), jnp.max(s_p, axis=-1, keepdims=True))
        p_c = jnp.exp(s_c - m_blk)
        p_p = jnp.exp(s_p - m_blk)
        l_blk = jnp.sum(p_c, axis=-1, keepdims=True) + jnp.sum(p_p, axis=-1, keepdims=True)
        o_blk = (jnp.dot(p_c.astype(bf16), vc, preferred_element_type=f32)
                 + jnp.dot(p_p.astype(bf16), vp, preferred_element_type=f32))
        m_b = jnp.broadcast_to(m_blk, (steps, e))
        l_b = jnp.broadcast_to(l_blk, (steps, e))
        if gi == 0:
            acc[rows, :] = o_blk
            mrun[rows, :] = m_b
            lrun[rows, :] = l_b
        else:
            m_old = mrun[rows, :]
            m_new = jnp.maximum(m_old, m_b)
            a_old = jnp.exp(m_old - m_new)
            a_blk = jnp.exp(m_b - m_new)
            acc[rows, :] = acc[rows, :] * a_old + o_blk * a_blk
            lrun[rows, :] = lrun[rows, :] * a_old + l_b * a_blk
            mrun[rows, :] = m_new
        return carry

    lax.fori_loop(0, dil * nb, body, 0)


def _attn_kernel(slopes_ref, q_ref, k_ref, v_ref, qnw_ref, knw_ref, o_ref, qn, kn, acc, mrun, lrun):
    j = pl.program_id(1)
    g = pl.program_id(2)

    def rms(x, w):
        return x * lax.rsqrt(jnp.mean(x * x, axis=-1, keepdims=True) + EPS) * w

    qn[...] = rms(q_ref[0], qnw_ref[...])
    kn[...] = rms(k_ref[0], knw_ref[...])
    for gi, (_, dil) in enumerate(DILATED_GROUPS):
        @pl.when(g == gi)
        def _(gi=gi, dil=dil):
            _attn_group(gi, dil, slopes_ref[gi, j], qn, kn, v_ref, acc, mrun, lrun)

    @pl.when(g == len(DILATED_GROUPS) - 1)
    def _():
        o_ref[0] = (acc[...] / lrun[...]).astype(o_ref.dtype)


def _attention(proj, q_norm_w, k_norm_w):
    b, s, _ = proj.shape
    e = ATTN_HEAD_DIM
    hg = ATTN_HEADS_PER_GROUP
    ng = len(DILATED_GROUPS)
    for win, dil in DILATED_GROUPS:
        assert s % win == 0 and win // dil == DILATED_GROUPS[0][0]
    slopes = jnp.asarray(_alibi_slopes(ATTN_HEADS)).reshape(ng, hg)
    head = lambda col: pl.BlockSpec((1, s, e), lambda i, j, g: (i, 0, col + g * hg + j))
    return pl.pallas_call(
        _attn_kernel,
        out_shape=jax.ShapeDtypeStruct((b, s, ATTN_OUT_WIDTH), bf16),
        grid=(b, hg, ng),
        in_specs=[pl.BlockSpec(memory_space=pltpu.SMEM),
                  head(COL_Q), head(COL_K), head(COL_V),
                  pl.BlockSpec((1, e), lambda i, j, g: (0, 0)),
                  pl.BlockSpec((1, e), lambda i, j, g: (0, 0))],
        out_specs=pl.BlockSpec((1, s, e), lambda i, j, g: (i, 0, j)),
        scratch_shapes=[pltpu.VMEM((s, e), f32)] * 5,
        compiler_params=_cparams(("parallel", "parallel", "arbitrary")),
        name="dilated_attn",
    )(slopes, proj, proj, proj, q_norm_w.reshape(1, e), k_norm_w.reshape(1, e))


def _pool_kernel(u_ref, halo_ref, wmix_ref, scale_ref, o_ref, ext):
    ts = u_ref.shape[1]
    gw = POOL_GROUP_WIDTH
    i = pl.program_id(1)
    halo = halo_ref[0]
    ext[0:POOL_HALO, :] = jnp.where(i > 0, halo, jnp.zeros_like(halo))
    ext[POOL_HALO:POOL_HALO + ts, :] = u_ref[0]
    pos = i * ts + lax.broadcasted_iota(jnp.int32, (ts, 1), 0)
    outs = []
    for gi, w in enumerate(POOL_WINDOWS):
        cols = slice(gi * gw, (gi + 1) * gw)
        cur = ext[pl.ds(POOL_HALO, ts), cols]
        tot = cur
        for k in range(1, w):
            tot = tot + ext[pl.ds(POOL_HALO - k, ts), cols]
        count = jnp.minimum(pos + 1, w).astype(f32)
        pooled = tot / count - cur
        outs.append(jnp.dot(pooled.astype(bf16), wmix_ref[gi], preferred_element_type=f32))
    o_ref[0] = (jnp.concatenate(outs, axis=1) * scale_ref[...]).astype(o_ref.dtype)


def _pool(proj, w_pool_mix, pool_scale, ts=512):
    b, s, _ = proj.shape
    pw = POOL_WIDTH
    per = ts // POOL_HALO
    return pl.pallas_call(
        _pool_kernel,
        out_shape=jax.ShapeDtypeStruct((b, s, pw), bf16),
        grid=(b, s // ts),
        in_specs=[pl.BlockSpec((1, ts, pw), lambda i, j: (i, j, COL_U * LANES // pw)),
                  pl.BlockSpec((1, POOL_HALO, pw),
                               lambda i, j: (i, jnp.maximum(j * per - 1, 0), COL_U * LANES // pw)),
                  pl.BlockSpec(w_pool_mix.shape, lambda i, j: (0, 0, 0)),
                  pl.BlockSpec((1, pw), lambda i, j: (0, 0))],
        out_specs=pl.BlockSpec((1, ts, pw), lambda i, j: (i, j, 0)),
        scratch_shapes=[pltpu.VMEM((ts + POOL_HALO, pw), f32)],
        compiler_params=_cparams(("parallel", "parallel")),
        name="pool",
    )(proj, proj, w_pool_mix.astype(bf16), pool_scale.reshape(1, pw))


def _merge_kernel(x_ref, ys_ref, ya_ref, yp_ref, gs_ref, ga_ref, gp_ref, g1_ref,
                  ws_ref, wa_ref, wp_ref, wo_ref, o_ref):
    dot = functools.partial(jnp.dot, preferred_element_type=f32)
    m = (jax.nn.sigmoid(gs_ref[0]) * dot(ys_ref[0], ws_ref[...])
         + jax.nn.sigmoid(ga_ref[0]) * dot(ya_ref[0], wa_ref[...])
         + jax.nn.sigmoid(gp_ref[0]) * dot(yp_ref[0], wp_ref[...]))
    o_ref[0] = x_ref[0] + g1_ref[0] * dot(m.astype(bf16), wo_ref[...])


def _merge(x, y_ssd, y_attn, y_pool, proj, g1, w_ssd_out, w_attn_out, w_pool_out, w_out, ts=512):
    b, s, d = x.shape
    tile = lambda w, col=0: pl.BlockSpec((1, ts, w), lambda i, j: (i, j, col))
    full = lambda w: pl.BlockSpec(w.shape, lambda i, j: (0, 0))
    gcol = COL_GATES * LANES // d
    return pl.pallas_call(
        _merge_kernel,
        out_shape=jax.ShapeDtypeStruct((b, s, d), f32),
        grid=(b, s // ts),
        in_specs=[tile(d), tile(y_ssd.shape[-1]), tile(y_attn.shape[-1]), tile(y_pool.shape[-1]),
                  tile(d, gcol), tile(d, gcol + 1), tile(d, gcol + 2),
                  pl.BlockSpec((1, 1, d), lambda i, j: (i, 0, 0)),
                  full(w_ssd_out), full(w_attn_out), full(w_pool_out), full(w_out)],
        out_specs=tile(d),
        compiler_params=_cparams(("parallel", "parallel")),
        name="merge",
    )(x, y_ssd, y_attn, y_pool, proj, proj, proj, g1, w_ssd_out, w_attn_out, w_pool_out, w_out)


def _ffn_kernel(x_ref, h_ref, g2_ref, w1_ref, w2_ref, o_ref, acc):
    k = pl.program_id(2)

    @pl.when(k == 0)
    def _():
        acc[...] = jnp.zeros_like(acc)

    a = jnp.maximum(jnp.dot(h_ref[0], w1_ref[...], preferred_element_type=f32), 0.0)
    acc[...] += jnp.dot((a * a).astype(bf16), w2_ref[...], preferred_element_type=f32)

    @pl.when(k == pl.num_programs(2) - 1)
    def _():
        o_ref[0] = x_ref[0] + g2_ref[0] * acc[...]


def _ffn(x, h, g2, w1, w2, ts=1024, tf=1024):
    b, s, d = x.shape
    ff = w1.shape[1]
    return pl.pallas_call(
        _ffn_kernel,
        out_shape=jax.ShapeDtypeStruct((b, s, d), f32),
        grid=(b, s // ts, ff // tf),
        in_specs=[pl.BlockSpec((1, ts, d), lambda i, j, k: (i, j, 0)),
                  pl.BlockSpec((1, ts, d), lambda i, j, k: (i, j, 0)),
                  pl.BlockSpec((1, 1, d), lambda i, j, k: (i, 0, 0)),
                  pl.BlockSpec((d, tf), lambda i, j, k: (0, k)),
                  pl.BlockSpec((tf, d), lambda i, j, k: (k, 0))],
        out_specs=pl.BlockSpec((1, ts, d), lambda i, j, k: (i, j, 0)),
        scratch_shapes=[pltpu.VMEM((ts, d), f32)],
        compiler_params=_cparams(("parallel", "parallel", "arbitrary")),
        name="ffn",
    )(x, h, g2, w1, w2)


def _pack_w_in(w_in):
    depth, d, _ = w_in.shape
    di = SSD_D_INNER
    o = np.cumsum([0, di, di + SSD_BC, SSD_HEADS, ATTN_WIDTH, ATTN_WIDTH, ATTN_WIDTH, POOL_WIDTH,
                   N_BRANCHES * D_MODEL]).tolist()
    seg = lambda a, b_: w_in[:, :, a:b_]
    z, xbc, dt, q, k, v, u, gates = [seg(o[i], o[i + 1]) for i in range(8)]
    used = COL_DT * LANES + SSD_HEADS
    pad = jnp.zeros((depth, d, PROJ_WIDTH - used), w_in.dtype)
    packed = jnp.concatenate([z, xbc[:, :, :di], u, gates, xbc[:, :, di:], q, k, v, dt, pad], axis=-1)
    return packed.astype(bf16)


def kernel(x, c, w_ada, b_ada, norm1_w, norm2_w, w_in, conv_w, conv_b, dt_bias, a_log, d_skip, ssd_norm_w, w_ssd_out, q_norm_w, k_norm_w, w_attn_out, w_pool_mix, pool_scale, w_pool_out, w_out, w_ff1, w_ff2):
    b, s, d = x.shape
    depth = w_in.shape[0]
    mod = _adaln(c, w_ada, b_ada)[:, :b].reshape(depth, b, 6, 1, d)
    w_proj = _pack_w_in(w_in)
    for l in range(depth):
        sh1, sc1, g1, sh2, sc2, g2 = [mod[l, :, i] for i in range(6)]
        h = _norm_mod(x, norm1_w[l], sc1, sh1)
        proj = _matmul(h.reshape(b * s, d), w_proj[l], 1024, PROJ_TILE_N, f32).reshape(b, s, PROJ_WIDTH)
        y_ssd = _ssd(proj, conv_w[l], conv_b[l], dt_bias[l], a_log[l], d_skip[l], ssd_norm_w[l])
        y_attn = _attention(proj, q_norm_w[l], k_norm_w[l])
        y_pool = _pool(proj, w_pool_mix[l], pool_scale[l])
        x = _merge(x, y_ssd, y_attn, y_pool, proj, g1, w_ssd_out[l].astype(bf16),
                   w_attn_out[l].astype(bf16), w_pool_out[l].astype(bf16), w_out[l].astype(bf16))
        h = _norm_mod(x, norm2_w[l], sc2, sh2)
        x = _ffn(x, h, g2, w_ff1[l].astype(bf16), w_ff2[l].astype(bf16))
    return x
```

```python
import functools
import math

import numpy as np
import jax
import jax.numpy as jnp
from jax import lax
from jax.experimental import pallas as pl
from jax.experimental.pallas import tpu as pltpu

f32 = jnp.float32
bf16 = jnp.bfloat16

D_MODEL = 1024
SSD_D_INNER = 1024
SSD_HEAD_DIM = 64
SSD_HEADS = SSD_D_INNER // SSD_HEAD_DIM
SSD_GROUPS = 2
SSD_STATE = 128
SSD_CONV = 4
SSD_CHUNK = 128
SSD_BC = 2 * SSD_GROUPS * SSD_STATE
SSD_CONV_DIM = SSD_D_INNER + SSD_BC
ATTN_HEAD_DIM = 128
ATTN_HEADS_PER_GROUP = 4
DILATED_GROUPS = ((128, 1), (512, 4), (2048, 16))
ATTN_HEADS = ATTN_HEADS_PER_GROUP * len(DILATED_GROUPS)
ATTN_WIDTH = ATTN_HEADS * ATTN_HEAD_DIM
ATTN_OUT_WIDTH = ATTN_HEADS_PER_GROUP * ATTN_HEAD_DIM
POOL_WINDOWS = (2, 4, 8, 16)
POOL_WIDTH = 1024
POOL_GROUP_WIDTH = POOL_WIDTH // len(POOL_WINDOWS)
N_BRANCHES = 3
D_FF = 4 * D_MODEL
EPS = 1e-6
LOG2E = math.log2(math.e)

LANES = 128
SUBLANES = 8
VMEM_LIMIT_BYTES = 56 * 1024 * 1024

COL_Z = 0
COL_XS = COL_Z + SSD_D_INNER
COL_BC = COL_XS + SSD_D_INNER
COL_Q = COL_BC + SSD_BC
COL_K = COL_Q + ATTN_WIDTH
COL_V = COL_K + ATTN_WIDTH
COL_U = COL_V + ATTN_WIDTH
COL_GATES = COL_U + POOL_WIDTH
PROJ_WIDTH = COL_GATES + N_BRANCHES * D_MODEL
POOL_HALO = 16
CONV_HALO = 8
ATTN_UNROLL = 8


def _alibi_slopes(n):
    def pow2(k):
        start = 2.0 ** (-8.0 / k)
        return [start ** (i + 1) for i in range(k)]
    if math.log2(n).is_integer():
        s = pow2(n)
    else:
        c = 2 ** math.floor(math.log2(n))
        s = pow2(c) + pow2(2 * c)[0::2][: n - c]
    return np.sort(np.asarray(s, np.float32))[::-1].copy()


def _cparams(sem):
    return pltpu.CompilerParams(dimension_semantics=sem, vmem_limit_bytes=VMEM_LIMIT_BYTES)


def _sigmoid(x):
    return 0.5 * (1.0 + jnp.tanh(0.5 * x))


def _silu(x):
    return x * _sigmoid(x)


def _modulated_rms(x, nw, sc, sh):
    y = x * lax.rsqrt(jnp.mean(x * x, axis=-1, keepdims=True) + EPS) * nw
    return y * (1.0 + sc) + sh


def _adaln_kernel(c_ref, w_ref, b_ref, o_ref):
    cond = _silu(c_ref[...])
    o_ref[0] = jnp.dot(cond, w_ref[0], preferred_element_type=f32,
                       precision=lax.Precision.HIGHEST) + b_ref[0]


def _adaln(c, w_ada, b_ada):
    depth, d, n = w_ada.shape
    rows = SUBLANES * pl.cdiv(c.shape[0], SUBLANES)
    c_pad = jnp.pad(c, ((0, rows - c.shape[0]), (0, 0)))
    tn = n // 4
    return pl.pallas_call(
        _adaln_kernel,
        out_shape=jax.ShapeDtypeStruct((depth, rows, n), f32),
        grid=(depth, n // tn),
        in_specs=[pl.BlockSpec((rows, d), lambda l, j: (0, 0)),
                  pl.BlockSpec((1, d, tn), lambda l, j: (l, 0, j)),
                  pl.BlockSpec((1, 1, tn), lambda l, j: (l, 0, j))],
        out_specs=pl.BlockSpec((1, rows, tn), lambda l, j: (l, 0, j)),
        compiler_params=_cparams(("parallel", "parallel")),
        name="adaln",
    )(c_pad, w_ada, b_ada.reshape(depth, 1, n))


def _in_proj_kernel(x_ref, nw_ref, sc_ref, sh_ref, w_ref, p_ref, h_ref, hs):
    @pl.when(pl.program_id(2) == 0)
    def _():
        h = _modulated_rms(x_ref[0], nw_ref[...], sc_ref[0], sh_ref[0]).astype(bf16)
        hs[...] = h
        h_ref[0] = h

    p_ref[0] = jnp.dot(hs[...], w_ref[...], preferred_element_type=f32).astype(p_ref.dtype)


def _in_proj(x, nw, sc, sh, w, tm=1024, tn=1024):
    b, s, d = x.shape
    n = w.shape[1]
    vec = pl.BlockSpec((1, 1, d), lambda i, j, k: (i, 0, 0))
    return pl.pallas_call(
        _in_proj_kernel,
        out_shape=(jax.ShapeDtypeStruct((b, s, n), bf16), jax.ShapeDtypeStruct((b, s, d), bf16)),
        grid=(b, s // tm, n // tn),
        in_specs=[pl.BlockSpec((1, tm, d), lambda i, j, k: (i, j, 0)),
                  pl.BlockSpec((1, d), lambda i, j, k: (0, 0)), vec, vec,
                  pl.BlockSpec((d, tn), lambda i, j, k: (0, k))],
        out_specs=(pl.BlockSpec((1, tm, tn), lambda i, j, k: (i, j, k)),
                   pl.BlockSpec((1, tm, d), lambda i, j, k: (i, j, 0))),
        scratch_shapes=[pltpu.VMEM((tm, d), bf16)],
        compiler_params=_cparams(("parallel", "parallel", "arbitrary")),
        name="in_proj",
    )(x, nw.reshape(1, d), sc, sh, w)


def _expand_heads(v):
    lane = lax.broadcasted_iota(jnp.int32, (v.shape[0], LANES), 1)
    tiles = [jnp.where(lane < SSD_HEAD_DIM, v[:, 2 * p:2 * p + 1], v[:, 2 * p + 1:2 * p + 2])
             for p in range(SSD_HEADS // 2)]
    return jnp.concatenate(tiles, axis=1)


def _ssd_kernel(h_ref, z_ref, xbc_ref, wdt_ref, cw_ref, cb_ref, dtb_ref, alog_ref, dsk_ref, nw_ref,
                o_ref, ext, conv, state):
    L = SSD_CHUNK
    N = SSD_STATE
    DI = SSD_D_INNER
    GW = DI // SSD_GROUPS

    @pl.when(pl.program_id(1) == 0)
    def _():
        ext[:, 0:CONV_HALO, :] = jnp.zeros((SSD_CONV_DIM // LANES, CONV_HALO, LANES), f32)
        state[...] = jnp.zeros_like(state)

    R = L // SUBLANES
    base = CONV_HALO - (SSD_CONV - 1)
    xbc_raw = xbc_ref[0].astype(f32)
    for c in range(SSD_CONV_DIM // LANES):
        cols = slice(c * LANES, (c + 1) * LANES)
        ext[c, CONV_HALO:CONV_HALO + L, :] = xbc_raw[:, cols]
        taps = [jnp.broadcast_to(cw_ref[k:k + 1, cols], (SUBLANES, LANES)) for k in range(SSD_CONV)]
        bias = jnp.broadcast_to(cb_ref[:, cols], (SUBLANES, LANES))
        rows = [ext[c, pl.ds(base + j, SUBLANES, stride=R), :] for j in range(R + SSD_CONV - 1)]
        for v in range(R):
            acc = bias
            for k in range(SSD_CONV):
                acc = acc + rows[v + k] * taps[k]
            conv[c, pl.ds(v, SUBLANES, stride=R), :] = _silu(acc)
        ext[c, 0:CONV_HALO, :] = ext[c, L:L + CONV_HALO, :]
    xs = jnp.concatenate([conv[c] for c in range(DI // LANES)], axis=1)
    bc = jnp.concatenate([conv[c] for c in range(DI // LANES, SSD_CONV_DIM // LANES)], axis=1)

    dtr = jnp.dot(h_ref[0], wdt_ref[...], preferred_element_type=f32) + dtb_ref[...]
    dt = jnp.maximum(dtr, 0.0) + jnp.log1p(jnp.exp(-jnp.abs(dtr)))
    a2 = dt * (-LOG2E * jnp.exp(alog_ref[...]))
    ti = lax.broadcasted_iota(jnp.int32, (L, L), 0)
    si = lax.broadcasted_iota(jnp.int32, (L, L), 1)
    causal = ti >= si
    a_cum = jnp.dot(causal.astype(f32), a2, preferred_element_type=f32,
                    precision=lax.Precision.HIGHEST)
    a_cum_t = a_cum.T

    dt_e = _expand_heads(dt)
    acum_e = _expand_heads(a_cum)
    alast_e = acum_e[L - 1:L, :]
    x_dt = xs * dt_e
    x_dec = (x_dt * jnp.exp2(alast_e - acum_e)).astype(bf16)
    x_dt_b = x_dt.astype(bf16)
    e_acum = jnp.exp2(acum_e)
    e_alast = jnp.exp2(alast_e)
    lane = lax.broadcasted_iota(jnp.int32, (L, LANES), 1)
    lo = lane < SSD_HEAD_DIM
    zero_b = jnp.zeros((L, LANES), bf16)

    ys = []
    for g in range(SSD_GROUPS):
        b_g = bc[:, g * N:(g + 1) * N]
        c_g = bc[:, (SSD_GROUPS + g) * N:(SSD_GROUPS + g + 1) * N].astype(bf16)
        cb = lax.dot_general(c_g, b_g.astype(bf16), (((1,), (1,)), ((), ())),
                             preferred_element_type=f32)
        st = state[g]
        y_off = jnp.dot(c_g, st.astype(bf16), preferred_element_type=f32) * e_acum[:, g * GW:(g + 1) * GW]
        new = jnp.dot(b_g.T.astype(bf16), x_dec[:, g * GW:(g + 1) * GW], preferred_element_type=f32)
        state[g] = e_alast[:, g * GW:(g + 1) * GW] * st + new
        tiles = []
        for p in range(GW // LANES):
            col0 = g * GW + p * LANES
            xp = x_dt_b[:, col0:col0 + LANES]
            acc = None
            for half in range(2):
                h = col0 // SSD_HEAD_DIM + half
                seg = a_cum[:, h:h + 1] - a_cum_t[h:h + 1, :]
                m_h = (cb * jnp.exp2(jnp.where(causal, seg, -jnp.inf))).astype(bf16)
                x_h = jnp.where(lo if half == 0 else jnp.logical_not(lo), xp, zero_b)
                part = jnp.dot(m_h, x_h, preferred_element_type=f32)
                acc = part if acc is None else acc + part
            tiles.append(acc)
        ys.append(jnp.concatenate(tiles, axis=1) + y_off)
    y = jnp.concatenate(ys, axis=1) + dsk_ref[...] * xs
    y = y * _silu(z_ref[0].astype(f32))
    outs = []
    for g in range(SSD_GROUPS):
        yg = y[:, g * GW:(g + 1) * GW]
        outs.append(yg * lax.rsqrt(jnp.mean(yg * yg, axis=-1, keepdims=True) + EPS))
    o_ref[0] = (jnp.concatenate(outs, axis=1) * nw_ref[...]).astype(o_ref.dtype)


def _pad_lanes(v):
    return jnp.pad(v, [(0, 0)] * (v.ndim - 1) + [(0, LANES - v.shape[-1])])


def _ssd(h, proj, w_dt, conv_w, conv_b, dt_bias, a_log, d_skip, ssd_norm_w):
    b, s, d = h.shape
    L = SSD_CHUNK
    di = SSD_D_INNER
    row = lambda v: v.reshape(1, -1)
    full = lambda shape: pl.BlockSpec(shape, lambda i, j: (0,) * len(shape))
    return pl.pallas_call(
        _ssd_kernel,
        out_shape=jax.ShapeDtypeStruct((b, s, di), bf16),
        grid=(b, s // L),
        in_specs=[pl.BlockSpec((1, L, d), lambda i, j: (i, j, 0)),
                  pl.BlockSpec((1, L, di), lambda i, j: (i, j, COL_Z // di)),
                  pl.BlockSpec((pl.Element(1), pl.Element(L), pl.Element(SSD_CONV_DIM)),
                               lambda i, j: (i, j * L, COL_XS)),
                  full((d, LANES)), full((SSD_CONV, SSD_CONV_DIM)), full((1, SSD_CONV_DIM)),
                  full((1, LANES)), full((1, LANES)), full((1, di)), full((1, di))],
        out_specs=pl.BlockSpec((1, L, di), lambda i, j: (i, j, 0)),
        scratch_shapes=[pltpu.VMEM((SSD_CONV_DIM // LANES, L + CONV_HALO, LANES), f32),
                        pltpu.VMEM((SSD_CONV_DIM // LANES, L, LANES), f32),
                        pltpu.VMEM((SSD_GROUPS, SSD_STATE, di // SSD_GROUPS), f32)],
        compiler_params=_cparams(("parallel", "arbitrary")),
        name="ssd",
    )(h, proj, proj, w_dt, conv_w, row(conv_b), row(_pad_lanes(dt_bias)), row(_pad_lanes(a_log)),
      row(jnp.repeat(d_skip, SSD_HEAD_DIM)), row(ssd_norm_w))


def _attn_group(gi, dil, slope, qn, kn, vf, og, lse):
    seq = qn.shape[0]
    steps = DILATED_GROUPS[gi][0] // dil
    e = ATTN_HEAD_DIM
    nb = seq // (dil * steps)
    qi = lax.broadcasted_iota(jnp.int32, (steps, steps), 0)
    kj = lax.broadcasted_iota(jnp.int32, (steps, steps), 1)
    neg = jnp.float32(-jnp.inf)
    bias_c = jnp.where(kj <= qi, (-slope * dil) * (qi - kj).astype(f32), neg)
    bias_p = jnp.where(kj >= qi, (-slope * dil) * (qi + steps - kj).astype(f32), neg)
    ones = jnp.ones((steps, e), bf16)
    nt = (((1,), (1,)), ((), ()))

    def block(it):
        r = it // nb
        i = it % nb
        start = r + i * (steps * dil)
        pstart = jnp.maximum(start - steps * dil, 0)
        if dil == 1:
            rows = pl.ds(pl.multiple_of(start, steps), steps)
            prows = pl.ds(pl.multiple_of(pstart, steps), steps)
        else:
            rows = pl.ds(start, steps, stride=dil)
            prows = pl.ds(pstart, steps, stride=dil)
        qb = qn[rows, :].astype(bf16)
        kc = kn[rows, :].astype(bf16)
        kp = kn[prows, :].astype(bf16)
        vc = jnp.concatenate([vf[rows, :].astype(bf16), ones], axis=1)
        vp = jnp.concatenate([vf[prows, :].astype(bf16), ones], axis=1)
        s_c = lax.dot_general(qb, kc, nt, preferred_element_type=f32) + bias_c
        s_p = lax.dot_general(qb, kp, nt, preferred_element_type=f32) + jnp.where(i > 0, bias_p, neg)
        m = jnp.max(jnp.maximum(s_c, s_p), axis=-1, keepdims=True)
        p_c = jnp.exp(s_c - m).astype(bf16)
        p_p = jnp.exp(s_p - m).astype(bf16)
        oe = (jnp.dot(p_c, vc, preferred_element_type=f32)
              + jnp.dot(p_p, vp, preferred_element_type=f32))
        l = oe[:, e:]
        og[gi, rows, :] = oe[:, :e] / l
        lse[gi, rows, :] = m + jnp.log(l)

    def body(it, carry):
        for u in range(ATTN_UNROLL):
            block(it * ATTN_UNROLL + u)
        return carry

    lax.fori_loop(0, dil * nb // ATTN_UNROLL, body, 0)


def _attn_kernel(slopes_ref, q_ref, k_ref, v_ref, qnw_ref, knw_ref, o_ref, qn, kn, vf, og, lse):
    j = pl.program_id(1)
    g = pl.program_id(2)
    seq, e = qn.shape
    rt = 512
    ng = len(DILATED_GROUPS)

    ones = jnp.ones((e, e), bf16)

    def rms(x, w):
        ss = jnp.dot((x * x).astype(bf16), ones, preferred_element_type=f32)
        return x * lax.rsqrt(ss * (1.0 / e) + EPS) * w

    def prep(t, carry):
        rows = pl.ds(pl.multiple_of(t * rt, rt), rt)
        qn[rows, :] = rms(q_ref[0, rows, :].astype(f32), qnw_ref[...] * (e ** -0.5))
        kn[rows, :] = rms(k_ref[0, rows, :].astype(f32), knw_ref[...])
        vf[rows, :] = v_ref[0, rows, :].astype(f32)
        return carry

    lax.fori_loop(0, seq // rt, prep, 0)
    for gi, (_, dil) in enumerate(DILATED_GROUPS):
        @pl.when(g == gi)
        def _(gi=gi, dil=dil):
            _attn_group(gi, dil, slopes_ref[gi, j], qn, kn, vf, og, lse)

    @pl.when(g == ng - 1)
    def _():
        def merge(t, carry):
            rows = pl.ds(pl.multiple_of(t * rt, rt), rt)
            ls = [lse[gi, rows, :] for gi in range(ng)]
            top = functools.reduce(jnp.maximum, ls)
            ws = [jnp.exp(x - top) for x in ls]
            num = functools.reduce(jnp.add, [w * og[gi, rows, :] for gi, w in enumerate(ws)])
            o_ref[0, rows, :] = (num / functools.reduce(jnp.add, ws)).astype(o_ref.dtype)
            return carry

        lax.fori_loop(0, seq // rt, merge, 0)


def _attention(proj, q_norm_w, k_norm_w):
    b, s, _ = proj.shape
    e = ATTN_HEAD_DIM
    hg = ATTN_HEADS_PER_GROUP
    ng = len(DILATED_GROUPS)
    for win, dil in DILATED_GROUPS:
        assert s % win == 0 and win // dil == DILATED_GROUPS[0][0]
        assert (s // (win // dil)) % ATTN_UNROLL == 0
    slopes = jnp.asarray(_alibi_slopes(ATTN_HEADS)).reshape(ng, hg)
    head = lambda col: pl.BlockSpec((1, s, e), lambda i, j, g: (i, 0, col // e + g * hg + j))
    return pl.pallas_call(
        _attn_kernel,
        out_shape=jax.ShapeDtypeStruct((b, s, ATTN_OUT_WIDTH), bf16),
        grid=(b, hg, ng),
        in_specs=[pl.BlockSpec(memory_space=pltpu.SMEM),
                  head(COL_Q), head(COL_K), head(COL_V),
                  pl.BlockSpec((1, e), lambda i, j, g: (0, 0)),
                  pl.BlockSpec((1, e), lambda i, j, g: (0, 0))],
        out_specs=pl.BlockSpec((1, s, e), lambda i, j, g: (i, 0, j)),
        scratch_shapes=[pltpu.VMEM((s, e), f32)] * 3 + [pltpu.VMEM((ng, s, e), f32)] * 2,
        compiler_params=_cparams(("parallel", "parallel", "arbitrary")),
        name="dilated_attn",
    )(slopes, proj, proj, proj, q_norm_w.reshape(1, e), k_norm_w.reshape(1, e))


def _pool_kernel(u_ref, halo_ref, wmix_ref, scale_ref, o_ref, ext):
    ts = u_ref.shape[1]
    gw = POOL_GROUP_WIDTH
    i = pl.program_id(1)
    halo = halo_ref[0].astype(f32)
    ext[0:POOL_HALO, :] = jnp.where(i > 0, halo, jnp.zeros_like(halo))
    ext[POOL_HALO:POOL_HALO + ts, :] = u_ref[0].astype(f32)
    pos = i * ts + lax.broadcasted_iota(jnp.int32, (ts, 1), 0)
    outs = []
    for gi, w in enumerate(POOL_WINDOWS):
        cols = slice(gi * gw, (gi + 1) * gw)
        cur = ext[pl.ds(POOL_HALO, ts), cols]
        tot = cur
        for k in range(1, w):
            tot = tot + ext[pl.ds(POOL_HALO - k, ts), cols]
        count = jnp.minimum(pos + 1, w).astype(f32)
        pooled = tot / count - cur
        outs.append(jnp.dot(pooled.astype(bf16), wmix_ref[gi], preferred_element_type=f32))
    o_ref[0] = (jnp.concatenate(outs, axis=1) * scale_ref[...]).astype(o_ref.dtype)


def _pool(proj, w_pool_mix, pool_scale, ts=512):
    b, s, _ = proj.shape
    pw = POOL_WIDTH
    per = ts // POOL_HALO
    return pl.pallas_call(
        _pool_kernel,
        out_shape=jax.ShapeDtypeStruct((b, s, pw), bf16),
        grid=(b, s // ts),
        in_specs=[pl.BlockSpec((1, ts, pw), lambda i, j: (i, j, COL_U // pw)),
                  pl.BlockSpec((1, POOL_HALO, pw),
                               lambda i, j: (i, jnp.maximum(j * per - 1, 0), COL_U // pw)),
                  pl.BlockSpec(w_pool_mix.shape, lambda i, j: (0, 0, 0)),
                  pl.BlockSpec((1, pw), lambda i, j: (0, 0))],
        out_specs=pl.BlockSpec((1, ts, pw), lambda i, j: (i, j, 0)),
        scratch_shapes=[pltpu.VMEM((ts + POOL_HALO, pw), f32)],
        compiler_params=_cparams(("parallel", "parallel")),
        name="pool",
    )(proj, proj, w_pool_mix, pool_scale.reshape(1, pw))


def _merge_kernel(x_ref, ys_ref, ya_ref, yp_ref, gs_ref, ga_ref, gp_ref, g1_ref,
                  ws_ref, wa_ref, wp_ref, wo_ref, o_ref):
    dot = functools.partial(jnp.dot, preferred_element_type=f32)
    gate = lambda ref: _sigmoid(ref[0].astype(f32))
    m = (gate(gs_ref) * dot(ys_ref[0], ws_ref[...])
         + gate(ga_ref) * dot(ya_ref[0], wa_ref[...])
         + gate(gp_ref) * dot(yp_ref[0], wp_ref[...]))
    o_ref[0] = x_ref[0] + g1_ref[0] * dot(m.astype(bf16), wo_ref[...])


def _merge(x, y_ssd, y_attn, y_pool, proj, g1, w_ssd_out, w_attn_out, w_pool_out, w_out, ts=512):
    b, s, d = x.shape
    tile = lambda w, col=0: pl.BlockSpec((1, ts, w), lambda i, j: (i, j, col))
    full = lambda w: pl.BlockSpec(w.shape, lambda i, j: (0, 0))
    gcol = COL_GATES // d
    return pl.pallas_call(
        _merge_kernel,
        out_shape=jax.ShapeDtypeStruct((b, s, d), f32),
        grid=(b, s // ts),
        in_specs=[tile(d), tile(y_ssd.shape[-1]), tile(y_attn.shape[-1]), tile(y_pool.shape[-1]),
                  tile(d, gcol), tile(d, gcol + 1), tile(d, gcol + 2),
                  pl.BlockSpec((1, 1, d), lambda i, j: (i, 0, 0)),
                  full(w_ssd_out), full(w_attn_out), full(w_pool_out), full(w_out)],
        out_specs=tile(d),
        compiler_params=_cparams(("parallel", "parallel")),
        name="merge",
    )(x, y_ssd, y_attn, y_pool, proj, proj, proj, g1, w_ssd_out, w_attn_out, w_pool_out, w_out)


def _ffn_kernel(x_ref, nw_ref, sc_ref, sh_ref, g2_ref, w1_ref, w2_ref, o_ref, hs, acc):
    k = pl.program_id(2)

    @pl.when(k == 0)
    def _():
        hs[...] = _modulated_rms(x_ref[0], nw_ref[...], sc_ref[0], sh_ref[0]).astype(bf16)
        acc[...] = jnp.zeros_like(acc)

    a = jnp.maximum(jnp.dot(hs[...], w1_ref[...], preferred_element_type=f32), 0.0)
    acc[...] += jnp.dot((a * a).astype(bf16), w2_ref[...], preferred_element_type=f32)

    @pl.when(k == pl.num_programs(2) - 1)
    def _():
        o_ref[0] = x_ref[0] + g2_ref[0] * acc[...]


def _ffn(x, nw, sc, sh, g2, w1, w2, ts=1024, tf=1024):
    b, s, d = x.shape
    ff = w1.shape[1]
    vec = pl.BlockSpec((1, 1, d), lambda i, j, k: (i, 0, 0))
    return pl.pallas_call(
        _ffn_kernel,
        out_shape=jax.ShapeDtypeStruct((b, s, d), f32),
        grid=(b, s // ts, ff // tf),
        in_specs=[pl.BlockSpec((1, ts, d), lambda i, j, k: (i, j, 0)),
                  pl.BlockSpec((1, d), lambda i, j, k: (0, 0)), vec, vec, vec,
                  pl.BlockSpec((d, tf), lambda i, j, k: (0, k)),
                  pl.BlockSpec((tf, d), lambda i, j, k: (k, 0))],
        out_specs=pl.BlockSpec((1, ts, d), lambda i, j, k: (i, j, 0)),
        scratch_shapes=[pltpu.VMEM((ts, d), bf16), pltpu.VMEM((ts, d), f32)],
        compiler_params=_cparams(("parallel", "parallel", "arbitrary")),
        name="ffn",
    )(x, nw.reshape(1, d), sc, sh, g2, w1, w2)


def kernel(x, c, w_ada, b_ada, norm1_w, norm2_w, w_in, conv_w, conv_b, dt_bias, a_log, d_skip, ssd_norm_w, w_ssd_out, q_norm_w, k_norm_w, w_attn_out, w_pool_mix, pool_scale, w_pool_out, w_out, w_ff1, w_ff2):
    b, s, d = x.shape
    depth = w_in.shape[0]
    mod = _adaln(c, w_ada, b_ada)[:, :b].reshape(depth, b, 6, 1, d)
    dt0 = COL_Q
    w_proj = jnp.concatenate([w_in[:, :, :dt0], w_in[:, :, dt0 + SSD_HEADS:]], axis=-1).astype(bf16)
    w_dt = _pad_lanes(w_in[:, :, dt0:dt0 + SSD_HEADS]).astype(bf16)
    cast = lambda w: w.astype(bf16)
    w_ssd_out, w_attn_out, w_pool_mix, w_pool_out, w_out, w_ff1, w_ff2 = map(
        cast, (w_ssd_out, w_attn_out, w_pool_mix, w_pool_out, w_out, w_ff1, w_ff2))
    for l in range(depth):
        sh1, sc1, g1, sh2, sc2, g2 = [mod[l, :, i] for i in range(6)]
        proj, h = _in_proj(x, norm1_w[l], sc1, sh1, w_proj[l])
        y_ssd = _ssd(h, proj, w_dt[l], conv_w[l], conv_b[l], dt_bias[l], a_log[l], d_skip[l], ssd_norm_w[l])
        y_attn = _attention(proj, q_norm_w[l], k_norm_w[l])
        y_pool = _pool(proj, w_pool_mix[l], pool_scale[l])
        x = _merge(x, y_ssd, y_attn, y_pool, proj, g1, w_ssd_out[l], w_attn_out[l], w_pool_out[l], w_out[l])
        x = _ffn(x, norm2_w[l], sc2, sh2, g2, w_ff1[l], w_ff2[l])
    return x
```

```python
import functools
import math

import numpy as np
import jax
import jax.numpy as jnp
from jax import lax
from jax.experimental import pallas as pl
from jax.experimental.pallas import tpu as pltpu

f32 = jnp.float32
bf16 = jnp.bfloat16

D_MODEL = 1024
SSD_D_INNER = 1024
SSD_HEAD_DIM = 64
SSD_HEADS = SSD_D_INNER // SSD_HEAD_DIM
SSD_GROUPS = 2
SSD_STATE = 128
SSD_CONV = 4
SSD_CHUNK = 128
SSD_BC = 2 * SSD_GROUPS * SSD_STATE
SSD_CONV_DIM = SSD_D_INNER + SSD_BC
ATTN_HEAD_DIM = 128
ATTN_HEADS_PER_GROUP = 4
DILATED_GROUPS = ((128, 1), (512, 4), (2048, 16))
ATTN_HEADS = ATTN_HEADS_PER_GROUP * len(DILATED_GROUPS)
ATTN_WIDTH = ATTN_HEADS * ATTN_HEAD_DIM
ATTN_OUT_WIDTH = ATTN_HEADS_PER_GROUP * ATTN_HEAD_DIM
POOL_WINDOWS = (2, 4, 8, 16)
POOL_WIDTH = 1024
POOL_GROUP_WIDTH = POOL_WIDTH // len(POOL_WINDOWS)
N_BRANCHES = 3
D_FF = 4 * D_MODEL
EPS = 1e-6
LOG2E = math.log2(math.e)

LANES = 128
SUBLANES = 8
VMEM_LIMIT_BYTES = 56 * 1024 * 1024

QK_WIDTH = 2 * ATTN_WIDTH
COL_Z = 0
COL_XS = COL_Z + SSD_D_INNER
COL_BC = COL_XS + SSD_D_INNER
COL_V = COL_BC + SSD_BC
COL_U = COL_V + ATTN_WIDTH
COL_GATES = COL_U + POOL_WIDTH
REST_WIDTH = COL_GATES + N_BRANCHES * D_MODEL
POOL_HALO = 16
CONV_HALO = 8
ATTN_UNROLL = 8


def _alibi_slopes(n):
    def pow2(k):
        start = 2.0 ** (-8.0 / k)
        return [start ** (i + 1) for i in range(k)]
    if math.log2(n).is_integer():
        s = pow2(n)
    else:
        c = 2 ** math.floor(math.log2(n))
        s = pow2(c) + pow2(2 * c)[0::2][: n - c]
    return np.sort(np.asarray(s, np.float32))[::-1].copy()


def _cparams(sem):
    return pltpu.CompilerParams(dimension_semantics=sem, vmem_limit_bytes=VMEM_LIMIT_BYTES)


def _sigmoid(x):
    return 0.5 * (1.0 + jnp.tanh(0.5 * x))


def _silu(x):
    return x * _sigmoid(x)


def _modulated_rms(x, nw, sc, sh):
    y = x * lax.rsqrt(jnp.mean(x * x, axis=-1, keepdims=True) + EPS) * nw
    return y * (1.0 + sc) + sh


def _adaln_kernel(c_ref, w_ref, b_ref, o_ref):
    cond = _silu(c_ref[...])
    o_ref[0] = jnp.dot(cond, w_ref[0], preferred_element_type=f32,
                       precision=lax.Precision.HIGHEST) + b_ref[0]


def _adaln(c, w_ada, b_ada):
    depth, d, n = w_ada.shape
    rows = SUBLANES * pl.cdiv(c.shape[0], SUBLANES)
    c_pad = jnp.pad(c, ((0, rows - c.shape[0]), (0, 0)))
    tn = n // 4
    return pl.pallas_call(
        _adaln_kernel,
        out_shape=jax.ShapeDtypeStruct((depth, rows, n), f32),
        grid=(depth, n // tn),
        in_specs=[pl.BlockSpec((rows, d), lambda l, j: (0, 0)),
                  pl.BlockSpec((1, d, tn), lambda l, j: (l, 0, j)),
                  pl.BlockSpec((1, 1, tn), lambda l, j: (l, 0, j))],
        out_specs=pl.BlockSpec((1, rows, tn), lambda l, j: (l, 0, j)),
        compiler_params=_cparams(("parallel", "parallel")),
        name="adaln",
    )(c_pad, w_ada, b_ada.reshape(depth, 1, n))


def _qk_proj_kernel(x_ref, nw_ref, sc_ref, sh_ref, w_ref, hw_ref, qk_ref, h_ref):
    e = ATTN_HEAD_DIM
    h_ref[0] = _modulated_rms(x_ref[0], nw_ref[...], sc_ref[0], sh_ref[0]).astype(bf16)
    for c in range(w_ref.shape[1] // (2 * e)):
        cols = slice(c * 2 * e, (c + 1) * 2 * e)
        r = jnp.dot(h_ref[0], w_ref[:, cols], preferred_element_type=f32)
        heads = []
        for half in range(2):
            y = r[:, half * e:(half + 1) * e]
            heads.append(y * lax.rsqrt(jnp.mean(y * y, axis=-1, keepdims=True) + EPS))
        qk_ref[0, :, cols] = (jnp.concatenate(heads, axis=1) * hw_ref[:, cols]).astype(qk_ref.dtype)


def _qk_proj(x, nw, sc, sh, w, head_w, tm=1024):
    b, s, d = x.shape
    n = w.shape[1]
    vec = pl.BlockSpec((1, 1, d), lambda i, j: (i, 0, 0))
    return pl.pallas_call(
        _qk_proj_kernel,
        out_shape=(jax.ShapeDtypeStruct((b, s, n), bf16), jax.ShapeDtypeStruct((b, s, d), bf16)),
        grid=(b, s // tm),
        in_specs=[pl.BlockSpec((1, tm, d), lambda i, j: (i, j, 0)),
                  pl.BlockSpec((1, d), lambda i, j: (0, 0)), vec, vec,
                  pl.BlockSpec((d, n), lambda i, j: (0, 0)),
                  pl.BlockSpec((1, n), lambda i, j: (0, 0))],
        out_specs=(pl.BlockSpec((1, tm, n), lambda i, j: (i, j, 0)),
                   pl.BlockSpec((1, tm, d), lambda i, j: (i, j, 0))),
        compiler_params=_cparams(("parallel", "parallel")),
        name="qk_proj",
    )(x, nw.reshape(1, d), sc, sh, w, head_w)


def _matmul_kernel(a_ref, w_ref, o_ref):
    o_ref[...] = jnp.dot(a_ref[...], w_ref[...], preferred_element_type=f32).astype(o_ref.dtype)


def _matmul(a, w, tm=1024, tn=2048):
    t, k = a.shape
    n = w.shape[1]
    return pl.pallas_call(
        _matmul_kernel,
        out_shape=jax.ShapeDtypeStruct((t, n), bf16),
        grid=(t // tm, n // tn),
        in_specs=[pl.BlockSpec((tm, k), lambda i, j: (i, 0)),
                  pl.BlockSpec((k, tn), lambda i, j: (0, j))],
        out_specs=pl.BlockSpec((tm, tn), lambda i, j: (i, j)),
        compiler_params=_cparams(("parallel", "parallel")),
        name="in_proj",
    )(a, w)


def _expand_heads(v):
    lane = lax.broadcasted_iota(jnp.int32, (v.shape[0], LANES), 1)
    tiles = [jnp.where(lane < SSD_HEAD_DIM, v[:, 2 * p:2 * p + 1], v[:, 2 * p + 1:2 * p + 2])
             for p in range(SSD_HEADS // 2)]
    return jnp.concatenate(tiles, axis=1)


def _ssd_kernel(h_ref, z_ref, xbc_ref, wdt_ref, cw_ref, cb_ref, dtb_ref, alog_ref, dsk_ref, nw_ref,
                o_ref, ext, conv, state):
    L = SSD_CHUNK
    N = SSD_STATE
    DI = SSD_D_INNER
    GW = DI // SSD_GROUPS

    @pl.when(pl.program_id(1) == 0)
    def _():
        ext[:, 0:CONV_HALO, :] = jnp.zeros((SSD_CONV_DIM // LANES, CONV_HALO, LANES), f32)
        state[...] = jnp.zeros_like(state)

    R = L // SUBLANES
    base = CONV_HALO - (SSD_CONV - 1)
    xbc_raw = xbc_ref[0].astype(f32)
    for c in range(SSD_CONV_DIM // LANES):
        cols = slice(c * LANES, (c + 1) * LANES)
        ext[c, CONV_HALO:CONV_HALO + L, :] = xbc_raw[:, cols]
        taps = [jnp.broadcast_to(cw_ref[k:k + 1, cols], (SUBLANES, LANES)) for k in range(SSD_CONV)]
        bias = jnp.broadcast_to(cb_ref[:, cols], (SUBLANES, LANES))
        rows = [ext[c, pl.ds(base + j, SUBLANES, stride=R), :] for j in range(R + SSD_CONV - 1)]
        for v in range(R):
            acc = bias
            for k in range(SSD_CONV):
                acc = acc + rows[v + k] * taps[k]
            conv[c, pl.ds(v, SUBLANES, stride=R), :] = _silu(acc)
        ext[c, 0:CONV_HALO, :] = ext[c, L:L + CONV_HALO, :]
    xs = jnp.concatenate([conv[c] for c in range(DI // LANES)], axis=1)
    bc = jnp.concatenate([conv[c] for c in range(DI // LANES, SSD_CONV_DIM // LANES)], axis=1)

    dtr = jnp.dot(h_ref[0], wdt_ref[...], preferred_element_type=f32) + dtb_ref[...]
    dt = jnp.maximum(dtr, 0.0) + jnp.log1p(jnp.exp(-jnp.abs(dtr)))
    a2 = dt * (-LOG2E * jnp.exp(alog_ref[...]))
    ti = lax.broadcasted_iota(jnp.int32, (L, L), 0)
    si = lax.broadcasted_iota(jnp.int32, (L, L), 1)
    causal = ti >= si
    a_cum = jnp.dot(causal.astype(f32), a2, preferred_element_type=f32,
                    precision=lax.Precision.HIGHEST)
    a_cum_t = a_cum.T

    dt_e = _expand_heads(dt)
    acum_e = _expand_heads(a_cum)
    alast_e = acum_e[L - 1:L, :]
    x_dt = xs * dt_e
    x_dec = (x_dt * jnp.exp2(alast_e - acum_e)).astype(bf16)
    x_dt_b = x_dt.astype(bf16)
    e_acum = jnp.exp2(acum_e)
    e_alast = jnp.exp2(alast_e)
    lane = lax.broadcasted_iota(jnp.int32, (L, LANES), 1)
    lo = lane < SSD_HEAD_DIM
    zero_b = jnp.zeros((L, LANES), bf16)

    ys = []
    for g in range(SSD_GROUPS):
        b_g = bc[:, g * N:(g + 1) * N]
        c_g = bc[:, (SSD_GROUPS + g) * N:(SSD_GROUPS + g + 1) * N].astype(bf16)
        cb = lax.dot_general(c_g, b_g.astype(bf16), (((1,), (1,)), ((), ())),
                             preferred_element_type=f32)
        st = state[g]
        y_off = jnp.dot(c_g, st.astype(bf16), preferred_element_type=f32) * e_acum[:, g * GW:(g + 1) * GW]
        new = jnp.dot(b_g.T.astype(bf16), x_dec[:, g * GW:(g + 1) * GW], preferred_element_type=f32)
        state[g] = e_alast[:, g * GW:(g + 1) * GW] * st + new
        tiles = []
        for p in range(GW // LANES):
            col0 = g * GW + p * LANES
            xp = x_dt_b[:, col0:col0 + LANES]
            acc = None
            for half in range(2):
                h = col0 // SSD_HEAD_DIM + half
                seg = a_cum[:, h:h + 1] - a_cum_t[h:h + 1, :]
                m_h = (cb * jnp.exp2(jnp.where(causal, seg, -jnp.inf))).astype(bf16)
                x_h = jnp.where(lo if half == 0 else jnp.logical_not(lo), xp, zero_b)
                part = jnp.dot(m_h, x_h, preferred_element_type=f32)
                acc = part if acc is None else acc + part
            tiles.append(acc)
        ys.append(jnp.concatenate(tiles, axis=1) + y_off)
    y = jnp.concatenate(ys, axis=1) + dsk_ref[...] * xs
    y = y * _silu(z_ref[0].astype(f32))
    outs = []
    for g in range(SSD_GROUPS):
        yg = y[:, g * GW:(g + 1) * GW]
        outs.append(yg * lax.rsqrt(jnp.mean(yg * yg, axis=-1, keepdims=True) + EPS))
    o_ref[0] = (jnp.concatenate(outs, axis=1) * nw_ref[...]).astype(o_ref.dtype)


def _pad_lanes(v):
    return jnp.pad(v, [(0, 0)] * (v.ndim - 1) + [(0, LANES - v.shape[-1])])


def _ssd(h, proj, w_dt, conv_w, conv_b, dt_bias, a_log, d_skip, ssd_norm_w):
    b, s, d = h.shape
    L = SSD_CHUNK
    di = SSD_D_INNER
    row = lambda v: v.reshape(1, -1)
    full = lambda shape: pl.BlockSpec(shape, lambda i, j: (0,) * len(shape))
    return pl.pallas_call(
        _ssd_kernel,
        out_shape=jax.ShapeDtypeStruct((b, s, di), bf16),
        grid=(b, s // L),
        in_specs=[pl.BlockSpec((1, L, d), lambda i, j: (i, j, 0)),
                  pl.BlockSpec((1, L, di), lambda i, j: (i, j, COL_Z // di)),
                  pl.BlockSpec((pl.Element(1), pl.Element(L), pl.Element(SSD_CONV_DIM)),
                               lambda i, j: (i, j * L, COL_XS)),
                  full((d, LANES)), full((SSD_CONV, SSD_CONV_DIM)), full((1, SSD_CONV_DIM)),
                  full((1, LANES)), full((1, LANES)), full((1, di)), full((1, di))],
        out_specs=pl.BlockSpec((1, L, di), lambda i, j: (i, j, 0)),
        scratch_shapes=[pltpu.VMEM((SSD_CONV_DIM // LANES, L + CONV_HALO, LANES), f32),
                        pltpu.VMEM((SSD_CONV_DIM // LANES, L, LANES), f32),
                        pltpu.VMEM((SSD_GROUPS, SSD_STATE, di // SSD_GROUPS), f32)],
        compiler_params=_cparams(("parallel", "arbitrary")),
        name="ssd",
    )(h, proj, proj, w_dt, conv_w, row(conv_b), row(_pad_lanes(dt_bias)), row(_pad_lanes(a_log)),
      row(jnp.repeat(d_skip, SSD_HEAD_DIM)), row(ssd_norm_w))


def _attn_group(gi, dil, slope, q, k, v, og, lse):
    seq = q.shape[0]
    steps = DILATED_GROUPS[gi][0] // dil
    e = ATTN_HEAD_DIM
    nb = seq // (dil * steps)
    qi = lax.broadcasted_iota(jnp.int32, (steps, 2 * steps), 0)
    kj = lax.broadcasted_iota(jnp.int32, (steps, 2 * steps), 1)
    neg = jnp.float32(-jnp.inf)
    rel = qi + steps - kj
    bias = jnp.where((rel >= 0) & (rel <= steps), (-slope * dil) * rel.astype(f32), neg)
    bias_first = jnp.where(kj >= steps, bias, neg)
    ones = jnp.ones((2 * steps, e), bf16)
    nt = (((1,), (1,)), ((), ()))

    def block(it):
        r = it // nb
        i = it % nb
        rows = pl.ds(pl.multiple_of(it * steps, steps), steps)
        band = pl.ds(pl.multiple_of(it * steps, steps), 2 * steps)
        vb = jnp.concatenate([v[band, :], ones], axis=1)
        s = lax.dot_general(q[rows, :], k[band, :], nt, preferred_element_type=f32)
        s = s + jnp.where(i > 0, bias, bias_first)
        m = jnp.max(s, axis=-1, keepdims=True)
        p = jnp.exp(s - m).astype(bf16)
        oe = jnp.dot(p, vb, preferred_element_type=f32)
        l = oe[:, e:]
        out_rows = rows if dil == 1 else pl.ds(r + i * (steps * dil), steps, stride=dil)
        og[gi, out_rows, :] = oe[:, :e] / l
        lse[gi, out_rows, :] = m + jnp.log(l)

    def body(it, carry):
        for u in range(ATTN_UNROLL):
            block(it * ATTN_UNROLL + u)
        return carry

    lax.fori_loop(0, dil * nb // ATTN_UNROLL, body, 0)


def _attn_kernel(slopes_ref, *refs):
    n_in = sum(3 * dil for _, dil in DILATED_GROUPS)
    ins, o_ref = refs[:n_in], refs[n_in]
    stage, og, lse = refs[n_in + 1:]
    j = pl.program_id(1)
    seq = o_ref.shape[1]
    rt = 512
    ng = len(DILATED_GROUPS)

    pad = stage.shape[1] - seq
    stage[1:3, 0:pad, :] = jnp.zeros((2, pad, stage.shape[2]), stage.dtype)
    off = 0
    for gi, (_, dil) in enumerate(DILATED_GROUPS):
        qkv = [ins[off + a * dil:off + (a + 1) * dil] for a in range(3)]
        off += 3 * dil
        n = seq // dil
        for a, part in enumerate(qkv):
            if a == 0 and dil == 1:
                continue
            lead = 0 if a == 0 else pad
            for r in range(dil):
                stage[a, lead + r * n:lead + (r + 1) * n, :] = part[r][0]
        q = qkv[0][0].at[0] if dil == 1 else stage.at[0]
        _attn_group(gi, dil, slopes_ref[gi, j], q, stage.at[1], stage.at[2], og, lse)

    def merge(t, carry):
        rows = pl.ds(pl.multiple_of(t * rt, rt), rt)
        ls = [lse[gi, rows, :] for gi in range(ng)]
        top = functools.reduce(jnp.maximum, ls)
        ws = [jnp.exp(x - top) for x in ls]
        num = functools.reduce(jnp.add, [w * og[gi, rows, :] for gi, w in enumerate(ws)])
        o_ref[0, rows, :] = (num / functools.reduce(jnp.add, ws)).astype(o_ref.dtype)
        return carry

    lax.fori_loop(0, seq // rt, merge, 0)


def _attention(qk, rest):
    b, s, _ = qk.shape
    e = ATTN_HEAD_DIM
    hg = ATTN_HEADS_PER_GROUP
    ng = len(DILATED_GROUPS)
    slopes = jnp.asarray(_alibi_slopes(ATTN_HEADS)).reshape(ng, hg)
    args, specs = [], []
    for gi, (win, dil) in enumerate(DILATED_GROUPS):
        assert s % win == 0 and win // dil == DILATED_GROUPS[0][0]
        assert (s // (win // dil)) % ATTN_UNROLL == 0
        for arr, col in ((qk, 0), (qk, ATTN_WIDTH), (rest, COL_V)):
            w = arr.shape[-1]
            view = arr.reshape(b, s // dil, dil * w)
            for r in range(dil):
                blk = (r * w + col) // e + gi * hg
                args.append(view)
                specs.append(pl.BlockSpec((1, s // dil, e), lambda i, j, blk=blk: (i, 0, blk + j)))
    return pl.pallas_call(
        _attn_kernel,
        out_shape=jax.ShapeDtypeStruct((b, s, ATTN_OUT_WIDTH), bf16),
        grid=(b, hg),
        in_specs=[pl.BlockSpec(memory_space=pltpu.SMEM)] + specs,
        out_specs=pl.BlockSpec((1, s, e), lambda i, j: (i, 0, j)),
        scratch_shapes=[pltpu.VMEM((3, s + DILATED_GROUPS[0][0], e), bf16)] + [pltpu.VMEM((ng, s, e), f32)] * 2,
        compiler_params=_cparams(("parallel", "parallel")),
        name="dilated_attn",
    )(slopes, *args)


def _pool_kernel(u_ref, halo_ref, wmix_ref, scale_ref, o_ref, ext):
    ts = u_ref.shape[1]
    gw = POOL_GROUP_WIDTH
    i = pl.program_id(1)
    halo = halo_ref[0].astype(f32)
    ext[0:POOL_HALO, :] = jnp.where(i > 0, halo, jnp.zeros_like(halo))
    ext[POOL_HALO:POOL_HALO + ts, :] = u_ref[0].astype(f32)
    pos = i * ts + lax.broadcasted_iota(jnp.int32, (ts, 1), 0)
    outs = []
    for gi, w in enumerate(POOL_WINDOWS):
        cols = slice(gi * gw, (gi + 1) * gw)
        cur = ext[pl.ds(POOL_HALO, ts), cols]
        tot = cur
        for k in range(1, w):
            tot = tot + ext[pl.ds(POOL_HALO - k, ts), cols]
        count = jnp.minimum(pos + 1, w).astype(f32)
        pooled = tot / count - cur
        outs.append(jnp.dot(pooled.astype(bf16), wmix_ref[gi], preferred_element_type=f32))
    o_ref[0] = (jnp.concatenate(outs, axis=1) * scale_ref[...]).astype(o_ref.dtype)


def _pool(proj, w_pool_mix, pool_scale, ts=512):
    b, s, _ = proj.shape
    pw = POOL_WIDTH
    per = ts // POOL_HALO
    return pl.pallas_call(
        _pool_kernel,
        out_shape=jax.ShapeDtypeStruct((b, s, pw), bf16),
        grid=(b, s // ts),
        in_specs=[pl.BlockSpec((1, ts, pw), lambda i, j: (i, j, COL_U // pw)),
                  pl.BlockSpec((1, POOL_HALO, pw),
                               lambda i, j: (i, jnp.maximum(j * per - 1, 0), COL_U // pw)),
                  pl.BlockSpec(w_pool_mix.shape, lambda i, j: (0, 0, 0)),
                  pl.BlockSpec((1, pw), lambda i, j: (0, 0))],
        out_specs=pl.BlockSpec((1, ts, pw), lambda i, j: (i, j, 0)),
        scratch_shapes=[pltpu.VMEM((ts + POOL_HALO, pw), f32)],
        compiler_params=_cparams(("parallel", "parallel")),
        name="pool",
    )(proj, proj, w_pool_mix, pool_scale.reshape(1, pw))


def _merge_kernel(x_ref, ys_ref, ya_ref, yp_ref, gs_ref, ga_ref, gp_ref, g1_ref,
                  ws_ref, wa_ref, wp_ref, wo_ref, o_ref):
    dot = functools.partial(jnp.dot, preferred_element_type=f32)
    gate = lambda ref: _sigmoid(ref[0].astype(f32))
    m = (gate(gs_ref) * dot(ys_ref[0], ws_ref[...])
         + gate(ga_ref) * dot(ya_ref[0], wa_ref[...])
         + gate(gp_ref) * dot(yp_ref[0], wp_ref[...]))
    o_ref[0] = x_ref[0] + g1_ref[0] * dot(m.astype(bf16), wo_ref[...])


def _merge(x, y_ssd, y_attn, y_pool, proj, g1, w_ssd_out, w_attn_out, w_pool_out, w_out, ts=512):
    b, s, d = x.shape
    tile = lambda w, col=0: pl.BlockSpec((1, ts, w), lambda i, j: (i, j, col))
    full = lambda w: pl.BlockSpec(w.shape, lambda i, j: (0, 0))
    gcol = COL_GATES // d
    return pl.pallas_call(
        _merge_kernel,
        out_shape=jax.ShapeDtypeStruct((b, s, d), f32),
        grid=(b, s // ts),
        in_specs=[tile(d), tile(y_ssd.shape[-1]), tile(y_attn.shape[-1]), tile(y_pool.shape[-1]),
                  tile(d, gcol), tile(d, gcol + 1), tile(d, gcol + 2),
                  pl.BlockSpec((1, 1, d), lambda i, j: (i, 0, 0)),
                  full(w_ssd_out), full(w_attn_out), full(w_pool_out), full(w_out)],
        out_specs=tile(d),
        compiler_params=_cparams(("parallel", "parallel")),
        name="merge",
    )(x, y_ssd, y_attn, y_pool, proj, proj, proj, g1, w_ssd_out, w_attn_out, w_pool_out, w_out)


def _ffn_kernel(x_ref, nw_ref, sc_ref, sh_ref, g2_ref, w1_ref, w2_ref, o_ref, hs, acc):
    k = pl.program_id(2)

    @pl.when(k == 0)
    def _():
        hs[...] = _modulated_rms(x_ref[0], nw_ref[...], sc_ref[0], sh_ref[0]).astype(bf16)
        acc[...] = jnp.zeros_like(acc)

    a = jnp.maximum(jnp.dot(hs[...], w1_ref[...], preferred_element_type=f32), 0.0)
    acc[...] += jnp.dot((a * a).astype(bf16), w2_ref[...], preferred_element_type=f32)

    @pl.when(k == pl.num_programs(2) - 1)
    def _():
        o_ref[0] = x_ref[0] + g2_ref[0] * acc[...]


def _ffn(x, nw, sc, sh, g2, w1, w2, ts=1024, tf=2048):
    b, s, d = x.shape
    ff = w1.shape[1]
    vec = pl.BlockSpec((1, 1, d), lambda i, j, k: (i, 0, 0))
    return pl.pallas_call(
        _ffn_kernel,
        out_shape=jax.ShapeDtypeStruct((b, s, d), f32),
        grid=(b, s // ts, ff // tf),
        in_specs=[pl.BlockSpec((1, ts, d), lambda i, j, k: (i, j, 0)),
                  pl.BlockSpec((1, d), lambda i, j, k: (0, 0)), vec, vec, vec,
                  pl.BlockSpec((d, tf), lambda i, j, k: (0, k)),
                  pl.BlockSpec((tf, d), lambda i, j, k: (k, 0))],
        out_specs=pl.BlockSpec((1, ts, d), lambda i, j, k: (i, j, 0)),
        scratch_shapes=[pltpu.VMEM((ts, d), bf16), pltpu.VMEM((ts, d), f32)],
        compiler_params=_cparams(("parallel", "parallel", "arbitrary")),
        name="ffn",
    )(x, nw.reshape(1, d), sc, sh, g2, w1, w2)


def kernel(x, c, w_ada, b_ada, norm1_w, norm2_w, w_in, conv_w, conv_b, dt_bias, a_log, d_skip, ssd_norm_w, w_ssd_out, q_norm_w, k_norm_w, w_attn_out, w_pool_mix, pool_scale, w_pool_out, w_out, w_ff1, w_ff2):
    b, s, d = x.shape
    depth = w_in.shape[0]
    mod = _adaln(c, w_ada, b_ada)[:, :b].reshape(depth, b, 6, 1, d)
    dt0 = SSD_D_INNER + SSD_CONV_DIM
    q0 = dt0 + SSD_HEADS
    v0 = q0 + QK_WIDTH
    w_qk = w_in[:, :, q0:v0].astype(bf16)
    w_rest = jnp.concatenate([w_in[:, :, :dt0], w_in[:, :, v0:]], axis=-1).astype(bf16)
    w_dt = _pad_lanes(w_in[:, :, dt0:q0]).astype(bf16)
    head_w = jnp.concatenate([jnp.tile(q_norm_w * ATTN_HEAD_DIM ** -0.5, (1, ATTN_HEADS)),
                              jnp.tile(k_norm_w, (1, ATTN_HEADS))], axis=-1)[:, None, :]
    cast = lambda w: w.astype(bf16)
    w_ssd_out, w_attn_out, w_pool_mix, w_pool_out, w_out, w_ff1, w_ff2 = map(
        cast, (w_ssd_out, w_attn_out, w_pool_mix, w_pool_out, w_out, w_ff1, w_ff2))
    for l in range(depth):
        sh1, sc1, g1, sh2, sc2, g2 = [mod[l, :, i] for i in range(6)]
        qk, h = _qk_proj(x, norm1_w[l], sc1, sh1, w_qk[l], head_w[l])
        rest = _matmul(h.reshape(b * s, d), w_rest[l]).reshape(b, s, REST_WIDTH)
        y_ssd = _ssd(h, rest, w_dt[l], conv_w[l], conv_b[l], dt_bias[l], a_log[l], d_skip[l], ssd_norm_w[l])
        y_attn = _attention(qk, rest)
        y_pool = _pool(rest, w_pool_mix[l], pool_scale[l])
        x = _merge(x, y_ssd, y_attn, y_pool, rest, g1, w_ssd_out[l], w_attn_out[l], w_pool_out[l], w_out[l])
        x = _ffn(x, norm2_w[l], sc2, sh2, g2, w_ff1[l], w_ff2[l])
    return x
```

```python
import functools
import math

import numpy as np
import jax
import jax.numpy as jnp
from jax import lax
from jax.experimental import pallas as pl
from jax.experimental.pallas import tpu as pltpu

f32 = jnp.float32
bf16 = jnp.bfloat16

D_MODEL = 1024
SSD_D_INNER = 1024
SSD_HEAD_DIM = 64
SSD_HEADS = SSD_D_INNER // SSD_HEAD_DIM
SSD_GROUPS = 2
SSD_STATE = 128
SSD_CONV = 4
SSD_CHUNK = 128
SSD_BC = 2 * SSD_GROUPS * SSD_STATE
SSD_CONV_DIM = SSD_D_INNER + SSD_BC
ATTN_HEAD_DIM = 128
ATTN_HEADS_PER_GROUP = 4
DILATED_GROUPS = ((128, 1), (512, 4), (2048, 16))
ATTN_HEADS = ATTN_HEADS_PER_GROUP * len(DILATED_GROUPS)
ATTN_WIDTH = ATTN_HEADS * ATTN_HEAD_DIM
ATTN_OUT_WIDTH = ATTN_HEADS_PER_GROUP * ATTN_HEAD_DIM
POOL_WINDOWS = (2, 4, 8, 16)
POOL_WIDTH = 1024
POOL_GROUP_WIDTH = POOL_WIDTH // len(POOL_WINDOWS)
N_BRANCHES = 3
D_FF = 4 * D_MODEL
EPS = 1e-6
LOG2E = math.log2(math.e)

LANES = 128
SUBLANES = 8
VMEM_LIMIT_BYTES = 56 * 1024 * 1024

ATTN_GROUP_WIDTH = 3 * ATTN_HEADS_PER_GROUP * ATTN_HEAD_DIM
QKV_WIDTH = 3 * ATTN_WIDTH
COL_Z = 0
COL_XS = COL_Z + SSD_D_INNER
COL_BC = COL_XS + SSD_D_INNER
COL_U = COL_BC + SSD_BC
COL_GATES = COL_U + POOL_WIDTH
REST_WIDTH = COL_GATES + N_BRANCHES * D_MODEL
POOL_HALO = 16
CONV_HALO = 8
ATTN_UNROLL = 8


def _alibi_slopes(n):
    def pow2(k):
        start = 2.0 ** (-8.0 / k)
        return [start ** (i + 1) for i in range(k)]
    if math.log2(n).is_integer():
        s = pow2(n)
    else:
        c = 2 ** math.floor(math.log2(n))
        s = pow2(c) + pow2(2 * c)[0::2][: n - c]
    return np.sort(np.asarray(s, np.float32))[::-1].copy()


def _cparams(sem):
    return pltpu.CompilerParams(dimension_semantics=sem, vmem_limit_bytes=VMEM_LIMIT_BYTES)


def _window(rows, width, row0, col):
    return pl.BlockSpec((pl.Element(1), pl.Element(rows), pl.Element(width)),
                        lambda i, j: (i, row0(j), col))


def _sigmoid(x):
    return 0.5 * (1.0 + jnp.tanh(0.5 * x))


def _silu(x):
    return x * _sigmoid(x)


def _modulated_rms(x, nw, sc, sh):
    y = x * lax.rsqrt(jnp.mean(x * x, axis=-1, keepdims=True) + EPS) * nw
    return y * (1.0 + sc) + sh


def _adaln_kernel(c_ref, w_ref, b_ref, o_ref):
    cond = _silu(c_ref[...])
    o_ref[0] = jnp.dot(cond, w_ref[0], preferred_element_type=f32,
                       precision=lax.Precision.HIGHEST) + b_ref[0]


def _adaln(c, w_ada, b_ada):
    depth, d, n = w_ada.shape
    rows = SUBLANES * pl.cdiv(c.shape[0], SUBLANES)
    c_pad = jnp.pad(c, ((0, rows - c.shape[0]), (0, 0)))
    tn = n // 4
    return pl.pallas_call(
        _adaln_kernel,
        out_shape=jax.ShapeDtypeStruct((depth, rows, n), f32),
        grid=(depth, n // tn),
        in_specs=[pl.BlockSpec((rows, d), lambda l, j: (0, 0)),
                  pl.BlockSpec((1, d, tn), lambda l, j: (l, 0, j)),
                  pl.BlockSpec((1, 1, tn), lambda l, j: (l, 0, j))],
        out_specs=pl.BlockSpec((1, rows, tn), lambda l, j: (l, 0, j)),
        compiler_params=_cparams(("parallel", "parallel")),
        name="adaln",
    )(c_pad, w_ada, b_ada.reshape(depth, 1, n))


def _qkv_proj_kernel(x_ref, nw_ref, sc_ref, sh_ref, w_ref, hw_ref, h_ref, *rest):
    outs, buf = rest[:-1], rest[-1]
    e = ATTN_HEAD_DIM
    gw = ATTN_GROUP_WIDTH
    tm = x_ref.shape[1]
    h_ref[0] = _modulated_rms(x_ref[0], nw_ref[...], sc_ref[0], sh_ref[0]).astype(bf16)
    for c in range(w_ref.shape[1] // (2 * e)):
        cols = slice(c * 2 * e, (c + 1) * 2 * e)
        gi, lc = divmod(c * 2 * e, gw)
        dil = DILATED_GROUPS[gi][1]
        y = jnp.dot(h_ref[0], w_ref[:, cols], preferred_element_type=f32)
        heads = [y[:, half * e:(half + 1) * e] for half in range(2)]
        if lc < 2 * gw // 3:
            heads = [t * lax.rsqrt(jnp.mean(t * t, axis=-1, keepdims=True) + EPS) for t in heads]
            heads = [t * hw_ref[:, c * 2 * e + half * e:c * 2 * e + (half + 1) * e]
                     for half, t in enumerate(heads)]
        for half, t in enumerate(heads):
            lcols = slice(lc + half * e, lc + (half + 1) * e)
            if dil == 1:
                outs[gi][0, 0, :, lcols] = t.astype(bf16)
            else:
                slot = (c % 2) * 2 + half
                buf[slot] = t
                for r in range(dil):
                    outs[gi][0, r, :, lcols] = buf[slot, pl.ds(r, tm // dil, stride=dil), :].astype(bf16)


def _qkv_proj(x, nw, sc, sh, w, head_w, tm=1024):
    b, s, d = x.shape
    n = w.shape[1]
    gw = ATTN_GROUP_WIDTH
    vec = pl.BlockSpec((1, 1, d), lambda i, j: (i, 0, 0))
    once = dict(pipeline_mode=pl.Buffered(1))
    dils = [dil for _, dil in DILATED_GROUPS]
    return pl.pallas_call(
        _qkv_proj_kernel,
        out_shape=[jax.ShapeDtypeStruct((b, s, d), bf16)]
                  + [jax.ShapeDtypeStruct((b, dil, s // dil, gw), bf16) for dil in dils],
        grid=(b, s // tm),
        in_specs=[pl.BlockSpec((1, tm, d), lambda i, j: (i, j, 0)),
                  pl.BlockSpec((1, d), lambda i, j: (0, 0)), vec, vec,
                  pl.BlockSpec((d, n), lambda i, j: (0, 0), **once),
                  pl.BlockSpec((1, n), lambda i, j: (0, 0))],
        out_specs=[pl.BlockSpec((1, tm, d), lambda i, j: (i, j, 0))]
                  + [pl.BlockSpec((1, dil, tm // dil, gw), lambda i, j: (i, 0, j, 0)) for dil in dils],
        scratch_shapes=[pltpu.VMEM((4, tm, ATTN_HEAD_DIM), f32)],
        compiler_params=_cparams(("parallel", "parallel")),
        name="qkv_proj",
    )(x, nw.reshape(1, d), sc, sh, w, head_w)


def _matmul_kernel(a_ref, w_ref, o_ref):
    o_ref[...] = jnp.dot(a_ref[...], w_ref[...], preferred_element_type=f32).astype(o_ref.dtype)


def _matmul(a, w, tm=1024, tn=REST_WIDTH // 2):
    t, k = a.shape
    n = w.shape[1]
    return pl.pallas_call(
        _matmul_kernel,
        out_shape=jax.ShapeDtypeStruct((t, n), bf16),
        grid=(t // tm, n // tn),
        in_specs=[pl.BlockSpec((tm, k), lambda i, j: (i, 0)),
                  pl.BlockSpec((k, tn), lambda i, j: (0, j))],
        out_specs=pl.BlockSpec((tm, tn), lambda i, j: (i, j)),
        compiler_params=_cparams(("parallel", "parallel")),
        name="in_proj",
    )(a, w)


def _expand_heads(v):
    lane = lax.broadcasted_iota(jnp.int32, (v.shape[0], LANES), 1)
    tiles = [jnp.where(lane < SSD_HEAD_DIM, v[:, 2 * p:2 * p + 1], v[:, 2 * p + 1:2 * p + 2])
             for p in range(SSD_HEADS // 2)]
    return jnp.concatenate(tiles, axis=1)


def _ssd_kernel(h_ref, z_ref, xbc_ref, wdt_ref, cw_ref, cb_ref, dtb_ref, alog_ref, dsk_ref, nw_ref,
                o_ref, ext, conv, state):
    L = SSD_CHUNK
    N = SSD_STATE
    DI = SSD_D_INNER
    GW = DI // SSD_GROUPS

    @pl.when(pl.program_id(1) == 0)
    def _():
        ext[:, 0:CONV_HALO, :] = jnp.zeros((SSD_CONV_DIM // LANES, CONV_HALO, LANES), f32)
        state[...] = jnp.zeros_like(state)

    R = L // SUBLANES
    base = CONV_HALO - (SSD_CONV - 1)
    xbc_raw = xbc_ref[0].astype(f32)
    for c in range(SSD_CONV_DIM // LANES):
        cols = slice(c * LANES, (c + 1) * LANES)
        ext[c, CONV_HALO:CONV_HALO + L, :] = xbc_raw[:, cols]
        taps = [jnp.broadcast_to(cw_ref[k:k + 1, cols], (SUBLANES, LANES)) for k in range(SSD_CONV)]
        bias = jnp.broadcast_to(cb_ref[:, cols], (SUBLANES, LANES))
        rows = [ext[c, pl.ds(base + j, SUBLANES, stride=R), :] for j in range(R + SSD_CONV - 1)]
        for v in range(R):
            acc = bias
            for k in range(SSD_CONV):
                acc = acc + rows[v + k] * taps[k]
            conv[c, pl.ds(v, SUBLANES, stride=R), :] = _silu(acc)
        ext[c, 0:CONV_HALO, :] = ext[c, L:L + CONV_HALO, :]
    xs = jnp.concatenate([conv[c] for c in range(DI // LANES)], axis=1)
    bc = jnp.concatenate([conv[c] for c in range(DI // LANES, SSD_CONV_DIM // LANES)], axis=1)

    dtr = jnp.dot(h_ref[0], wdt_ref[...], preferred_element_type=f32) + dtb_ref[...]
    dt = jnp.maximum(dtr, 0.0) + jnp.log1p(jnp.exp(-jnp.abs(dtr)))
    a2 = dt * (-LOG2E * jnp.exp(alog_ref[...]))
    ti = lax.broadcasted_iota(jnp.int32, (L, L), 0)
    si = lax.broadcasted_iota(jnp.int32, (L, L), 1)
    causal = ti >= si
    a_cum = jnp.dot(causal.astype(f32), a2, preferred_element_type=f32,
                    precision=lax.Precision.HIGHEST)
    a_cum_t = a_cum.T

    dt_e = _expand_heads(dt)
    acum_e = _expand_heads(a_cum)
    alast_e = acum_e[L - 1:L, :]
    x_dt = xs * dt_e
    x_dec = (x_dt * jnp.exp2(alast_e - acum_e)).astype(bf16)
    x_dt_b = x_dt.astype(bf16)
    e_acum = jnp.exp2(acum_e)
    e_alast = jnp.exp2(alast_e)
    lane = lax.broadcasted_iota(jnp.int32, (L, LANES), 1)
    lo = lane < SSD_HEAD_DIM
    zero_b = jnp.zeros((L, LANES), bf16)

    ys = []
    for g in range(SSD_GROUPS):
        b_g = bc[:, g * N:(g + 1) * N]
        c_g = bc[:, (SSD_GROUPS + g) * N:(SSD_GROUPS + g + 1) * N].astype(bf16)
        cb = lax.dot_general(c_g, b_g.astype(bf16), (((1,), (1,)), ((), ())),
                             preferred_element_type=f32)
        st = state[g]
        y_off = jnp.dot(c_g, st.astype(bf16), preferred_element_type=f32) * e_acum[:, g * GW:(g + 1) * GW]
        new = jnp.dot(b_g.T.astype(bf16), x_dec[:, g * GW:(g + 1) * GW], preferred_element_type=f32)
        state[g] = e_alast[:, g * GW:(g + 1) * GW] * st + new
        tiles = []
        for p in range(GW // LANES):
            col0 = g * GW + p * LANES
            xp = x_dt_b[:, col0:col0 + LANES]
            acc = None
            for half in range(2):
                h = col0 // SSD_HEAD_DIM + half
                seg = a_cum[:, h:h + 1] - a_cum_t[h:h + 1, :]
                m_h = (cb * jnp.exp2(jnp.where(causal, seg, -jnp.inf))).astype(bf16)
                x_h = jnp.where(lo if half == 0 else jnp.logical_not(lo), xp, zero_b)
                part = jnp.dot(m_h, x_h, preferred_element_type=f32)
                acc = part if acc is None else acc + part
            tiles.append(acc)
        ys.append(jnp.concatenate(tiles, axis=1) + y_off)
    y = jnp.concatenate(ys, axis=1) + dsk_ref[...] * xs
    y = y * _silu(z_ref[0].astype(f32))
    outs = []
    for g in range(SSD_GROUPS):
        yg = y[:, g * GW:(g + 1) * GW]
        outs.append(yg * lax.rsqrt(jnp.mean(yg * yg, axis=-1, keepdims=True) + EPS))
    o_ref[0] = (jnp.concatenate(outs, axis=1) * nw_ref[...]).astype(o_ref.dtype)


def _pad_lanes(v):
    return jnp.pad(v, [(0, 0)] * (v.ndim - 1) + [(0, LANES - v.shape[-1])])


def _ssd(h, proj, w_dt, conv_w, conv_b, dt_bias, a_log, d_skip, ssd_norm_w):
    b, s, d = h.shape
    L = SSD_CHUNK
    di = SSD_D_INNER
    row = lambda v: v.reshape(1, -1)
    full = lambda shape: pl.BlockSpec(shape, lambda i, j: (0,) * len(shape))
    return pl.pallas_call(
        _ssd_kernel,
        out_shape=jax.ShapeDtypeStruct((b, s, di), bf16),
        grid=(b, s // L),
        in_specs=[pl.BlockSpec((1, L, d), lambda i, j: (i, j, 0)),
                  pl.BlockSpec((1, L, di), lambda i, j: (i, j, COL_Z // di)),
                  _window(L, SSD_CONV_DIM, lambda j: j * L, COL_XS),
                  full((d, LANES)), full((SSD_CONV, SSD_CONV_DIM)), full((1, SSD_CONV_DIM)),
                  full((1, LANES)), full((1, LANES)), full((1, di)), full((1, di))],
        out_specs=pl.BlockSpec((1, L, di), lambda i, j: (i, j, 0)),
        scratch_shapes=[pltpu.VMEM((SSD_CONV_DIM // LANES, L + CONV_HALO, LANES), f32),
                        pltpu.VMEM((SSD_CONV_DIM // LANES, L, LANES), f32),
                        pltpu.VMEM((SSD_GROUPS, SSD_STATE, di // SSD_GROUPS), f32)],
        compiler_params=_cparams(("parallel", "arbitrary")),
        name="ssd",
    )(h, proj, proj, w_dt, conv_w, row(conv_b), row(_pad_lanes(dt_bias)), row(_pad_lanes(a_log)),
      row(jnp.repeat(d_skip, SSD_HEAD_DIM)), row(ssd_norm_w))


def _attn_group(gi, dil, slope, q, k, v, og, lse):
    seq = q.shape[0] * q.shape[1]
    steps = DILATED_GROUPS[gi][0] // dil
    e = ATTN_HEAD_DIM
    nb = seq // (dil * steps)
    qi = lax.broadcasted_iota(jnp.int32, (steps, 2 * steps), 0)
    kj = lax.broadcasted_iota(jnp.int32, (steps, 2 * steps), 1)
    neg = jnp.float32(-jnp.inf)
    rel = qi + steps - kj
    bias = jnp.where((rel >= 0) & (rel <= steps), (-slope * dil) * rel.astype(f32), neg)
    bias_first = jnp.where(kj >= steps, bias, neg)
    ones = jnp.ones((2 * steps, e), bf16)
    nt = (((1,), (1,)), ((), ()))

    def block(it):
        r = it // nb
        i = it % nb
        rows = pl.ds(pl.multiple_of(it * steps, steps), steps)
        band = pl.ds(pl.multiple_of(it * steps, steps), 2 * steps)
        vb = jnp.concatenate([v[band, :], ones], axis=1)
        qb = q[r, pl.ds(pl.multiple_of(i * steps, steps), steps), :]
        s = lax.dot_general(qb, k[band, :], nt, preferred_element_type=f32)
        s = s + jnp.where(i > 0, bias, bias_first)
        m = jnp.max(s, axis=-1, keepdims=True)
        p = jnp.exp(s - m).astype(bf16)
        oe = jnp.dot(p, vb, preferred_element_type=f32)
        l = oe[:, e:]
        out_rows = rows if dil == 1 else pl.ds(r + i * (steps * dil), steps, stride=dil)
        og[gi, out_rows, :] = oe[:, :e] / l
        lse[gi, out_rows, :] = m + jnp.log(l)

    def body(it, carry):
        for u in range(ATTN_UNROLL):
            block(it * ATTN_UNROLL + u)
        return carry

    lax.fori_loop(0, dil * nb // ATTN_UNROLL, body, 0)


def _attn_kernel(slopes_ref, *refs):
    ng = len(DILATED_GROUPS)
    ins, o_ref = refs[:3 * ng], refs[3 * ng]
    stage, og, lse = refs[3 * ng + 1:]
    j = pl.program_id(1)
    seq = o_ref.shape[1]
    rt = 512

    pad = stage.shape[1] - seq
    stage[:, 0:pad, :] = jnp.zeros((2, pad, stage.shape[2]), stage.dtype)
    for gi, (_, dil) in enumerate(DILATED_GROUPS):
        q, k, v = ins[3 * gi:3 * gi + 3]
        n = seq // dil
        for a, part in enumerate((k, v)):
            for r in range(dil):
                stage[a, pad + r * n:pad + (r + 1) * n, :] = part[0, r]
        _attn_group(gi, dil, slopes_ref[gi, j], q.at[0], stage.at[0], stage.at[1], og, lse)

    def merge(t, carry):
        rows = pl.ds(pl.multiple_of(t * rt, rt), rt)
        ls = [lse[gi, rows, :] for gi in range(ng)]
        top = functools.reduce(jnp.maximum, ls)
        ws = [jnp.exp(x - top) for x in ls]
        num = functools.reduce(jnp.add, [w * og[gi, rows, :] for gi, w in enumerate(ws)])
        o_ref[0, rows, :] = (num / functools.reduce(jnp.add, ws)).astype(o_ref.dtype)
        return carry

    lax.fori_loop(0, seq // rt, merge, 0)


def _attention(groups):
    e = ATTN_HEAD_DIM
    hg = ATTN_HEADS_PER_GROUP
    ng = len(DILATED_GROUPS)
    b = groups[0].shape[0]
    s = groups[0].shape[1] * groups[0].shape[2]
    slopes = jnp.asarray(_alibi_slopes(ATTN_HEADS)).reshape(ng, hg)
    args, specs = [], []
    for arr, (win, dil) in zip(groups, DILATED_GROUPS):
        assert s % win == 0 and win // dil == DILATED_GROUPS[0][0]
        assert (s // (win // dil)) % ATTN_UNROLL == 0
        for a in range(3):
            args.append(arr)
            specs.append(pl.BlockSpec((1, dil, s // dil, e), lambda i, j, a=a: (i, 0, 0, a * hg + j)))
    return pl.pallas_call(
        _attn_kernel,
        out_shape=jax.ShapeDtypeStruct((b, s, ATTN_OUT_WIDTH), bf16),
        grid=(b, hg),
        in_specs=[pl.BlockSpec(memory_space=pltpu.SMEM)] + specs,
        out_specs=pl.BlockSpec((1, s, e), lambda i, j: (i, 0, j)),
        scratch_shapes=[pltpu.VMEM((2, s + DILATED_GROUPS[0][0], e), bf16)] + [pltpu.VMEM((ng, s, e), f32)] * 2,
        compiler_params=_cparams(("parallel", "parallel")),
        name="dilated_attn",
    )(slopes, *args)


def _pool_kernel(u_ref, halo_ref, wmix_ref, scale_ref, o_ref, ext):
    ts = u_ref.shape[1]
    gw = POOL_GROUP_WIDTH
    i = pl.program_id(1)
    halo = halo_ref[0].astype(f32)
    ext[0:POOL_HALO, :] = jnp.where(i > 0, halo, jnp.zeros_like(halo))
    ext[POOL_HALO:POOL_HALO + ts, :] = u_ref[0].astype(f32)
    pos = i * ts + lax.broadcasted_iota(jnp.int32, (ts, 1), 0)
    outs = []
    for gi, w in enumerate(POOL_WINDOWS):
        cols = slice(gi * gw, (gi + 1) * gw)
        cur = ext[pl.ds(POOL_HALO, ts), cols]
        tot = cur
        for k in range(1, w):
            tot = tot + ext[pl.ds(POOL_HALO - k, ts), cols]
        count = jnp.minimum(pos + 1, w).astype(f32)
        pooled = tot / count - cur
        outs.append(jnp.dot(pooled.astype(bf16), wmix_ref[gi], preferred_element_type=f32))
    o_ref[0] = (jnp.concatenate(outs, axis=1) * scale_ref[...]).astype(o_ref.dtype)


def _pool(proj, w_pool_mix, pool_scale, ts=512):
    b, s, _ = proj.shape
    pw = POOL_WIDTH
    return pl.pallas_call(
        _pool_kernel,
        out_shape=jax.ShapeDtypeStruct((b, s, pw), bf16),
        grid=(b, s // ts),
        in_specs=[_window(ts, pw, lambda j: j * ts, COL_U),
                  _window(POOL_HALO, pw,
                          lambda j: pl.multiple_of(jnp.maximum(j * ts - POOL_HALO, 0), POOL_HALO), COL_U),
                  pl.BlockSpec(w_pool_mix.shape, lambda i, j: (0, 0, 0)),
                  pl.BlockSpec((1, pw), lambda i, j: (0, 0))],
        out_specs=pl.BlockSpec((1, ts, pw), lambda i, j: (i, j, 0)),
        scratch_shapes=[pltpu.VMEM((ts + POOL_HALO, pw), f32)],
        compiler_params=_cparams(("parallel", "parallel")),
        name="pool",
    )(proj, proj, w_pool_mix, pool_scale.reshape(1, pw))


def _merge_kernel(x_ref, ys_ref, ya_ref, yp_ref, gs_ref, ga_ref, gp_ref, g1_ref,
                  ws_ref, wa_ref, wp_ref, wo_ref, o_ref):
    dot = functools.partial(jnp.dot, preferred_element_type=f32)
    gate = lambda ref: _sigmoid(ref[0].astype(f32))
    m = (gate(gs_ref) * dot(ys_ref[0], ws_ref[...])
         + gate(ga_ref) * dot(ya_ref[0], wa_ref[...])
         + gate(gp_ref) * dot(yp_ref[0], wp_ref[...]))
    o_ref[0] = x_ref[0] + g1_ref[0] * dot(m.astype(bf16), wo_ref[...])


def _merge(x, y_ssd, y_attn, y_pool, proj, g1, w_ssd_out, w_attn_out, w_pool_out, w_out, ts=512):
    b, s, d = x.shape
    tile = lambda w, col=0: pl.BlockSpec((1, ts, w), lambda i, j: (i, j, col))
    full = lambda w: pl.BlockSpec(w.shape, lambda i, j: (0, 0))
    gate = lambda g: _window(ts, d, lambda j: j * ts, COL_GATES + g * d)
    return pl.pallas_call(
        _merge_kernel,
        out_shape=jax.ShapeDtypeStruct((b, s, d), f32),
        grid=(b, s // ts),
        in_specs=[tile(d), tile(y_ssd.shape[-1]), tile(y_attn.shape[-1]), tile(y_pool.shape[-1]),
                  gate(0), gate(1), gate(2),
                  pl.BlockSpec((1, 1, d), lambda i, j: (i, 0, 0)),
                  full(w_ssd_out), full(w_attn_out), full(w_pool_out), full(w_out)],
        out_specs=tile(d),
        compiler_params=_cparams(("parallel", "parallel")),
        name="merge",
    )(x, y_ssd, y_attn, y_pool, proj, proj, proj, g1, w_ssd_out, w_attn_out, w_pool_out, w_out)


def _ffn_kernel(x_ref, nw_ref, sc_ref, sh_ref, g2_ref, w1_ref, w2_ref, o_ref, hs, acc):
    k = pl.program_id(2)

    @pl.when(k == 0)
    def _():
        hs[...] = _modulated_rms(x_ref[0], nw_ref[...], sc_ref[0], sh_ref[0]).astype(bf16)
        acc[...] = jnp.zeros_like(acc)

    a = jnp.maximum(jnp.dot(hs[...], w1_ref[...], preferred_element_type=f32), 0.0)
    acc[...] += jnp.dot((a * a).astype(bf16), w2_ref[...], preferred_element_type=f32)

    @pl.when(k == pl.num_programs(2) - 1)
    def _():
        o_ref[0] = x_ref[0] + g2_ref[0] * acc[...]


def _ffn(x, nw, sc, sh, g2, w1, w2, ts=1024, tf=2048):
    b, s, d = x.shape
    ff = w1.shape[1]
    vec = pl.BlockSpec((1, 1, d), lambda i, j, k: (i, 0, 0))
    return pl.pallas_call(
        _ffn_kernel,
        out_shape=jax.ShapeDtypeStruct((b, s, d), f32),
        grid=(b, s // ts, ff // tf),
        in_specs=[pl.BlockSpec((1, ts, d), lambda i, j, k: (i, j, 0)),
                  pl.BlockSpec((1, d), lambda i, j, k: (0, 0)), vec, vec, vec,
                  pl.BlockSpec((d, tf), lambda i, j, k: (0, k)),
                  pl.BlockSpec((tf, d), lambda i, j, k: (k, 0))],
        out_specs=pl.BlockSpec((1, ts, d), lambda i, j, k: (i, j, 0)),
        scratch_shapes=[pltpu.VMEM((ts, d), bf16), pltpu.VMEM((ts, d), f32)],
        compiler_params=_cparams(("parallel", "parallel", "arbitrary")),
        name="ffn",
    )(x, nw.reshape(1, d), sc, sh, g2, w1, w2)


def kernel(x, c, w_ada, b_ada, norm1_w, norm2_w, w_in, conv_w, conv_b, dt_bias, a_log, d_skip, ssd_norm_w, w_ssd_out, q_norm_w, k_norm_w, w_attn_out, w_pool_mix, pool_scale, w_pool_out, w_out, w_ff1, w_ff2):
    b, s, d = x.shape
    depth = w_in.shape[0]
    mod = _adaln(c, w_ada, b_ada)[:, :b].reshape(depth, b, 6, 1, d)
    dt0 = SSD_D_INNER + SSD_CONV_DIM
    q0 = dt0 + SSD_HEADS
    u0 = q0 + QKV_WIDTH
    gcols = ATTN_HEADS_PER_GROUP * ATTN_HEAD_DIM
    w_qkv = jnp.concatenate(
        [w_in[:, :, q0 + a * ATTN_WIDTH + g * gcols:q0 + a * ATTN_WIDTH + (g + 1) * gcols]
         for g in range(len(DILATED_GROUPS)) for a in range(3)], axis=-1).astype(bf16)
    w_rest = jnp.concatenate([w_in[:, :, :dt0], w_in[:, :, u0:]], axis=-1).astype(bf16)
    w_dt = _pad_lanes(w_in[:, :, dt0:q0]).astype(bf16)
    per_group = jnp.concatenate([jnp.tile(q_norm_w * ATTN_HEAD_DIM ** -0.5, (1, ATTN_HEADS_PER_GROUP)),
                                 jnp.tile(k_norm_w, (1, ATTN_HEADS_PER_GROUP)),
                                 jnp.ones((depth, gcols), f32)], axis=-1)
    head_w = jnp.tile(per_group, (1, len(DILATED_GROUPS)))[:, None, :]
    cast = lambda w: w.astype(bf16)
    w_ssd_out, w_attn_out, w_pool_mix, w_pool_out, w_out, w_ff1, w_ff2 = map(
        cast, (w_ssd_out, w_attn_out, w_pool_mix, w_pool_out, w_out, w_ff1, w_ff2))
    for l in range(depth):
        sh1, sc1, g1, sh2, sc2, g2 = [mod[l, :, i] for i in range(6)]
        h, *qkv_groups = _qkv_proj(x, norm1_w[l], sc1, sh1, w_qkv[l], head_w[l])
        rest = _matmul(h.reshape(b * s, d), w_rest[l]).reshape(b, s, REST_WIDTH)
        y_ssd = _ssd(h, rest, w_dt[l], conv_w[l], conv_b[l], dt_bias[l], a_log[l], d_skip[l], ssd_norm_w[l])
        y_attn = _attention(qkv_groups)
        y_pool = _pool(rest, w_pool_mix[l], pool_scale[l])
        x = _merge(x, y_ssd, y_attn, y_pool, rest, g1, w_ssd_out[l], w_attn_out[l], w_pool_out[l], w_out[l])
        x = _ffn(x, norm2_w[l], sc2, sh2, g2, w_ff1[l], w_ff2[l])
    return x
```

```python
import functools
import math

import numpy as np
import jax
import jax.numpy as jnp
from jax import lax
from jax.experimental import pallas as pl
from jax.experimental.pallas import tpu as pltpu

f32 = jnp.float32
bf16 = jnp.bfloat16

D_MODEL = 1024
SSD_D_INNER = 1024
SSD_HEAD_DIM = 64
SSD_HEADS = SSD_D_INNER // SSD_HEAD_DIM
SSD_GROUPS = 2
SSD_STATE = 128
SSD_CONV = 4
SSD_CHUNK = 128
SSD_BC = 2 * SSD_GROUPS * SSD_STATE
SSD_CONV_DIM = SSD_D_INNER + SSD_BC
ATTN_HEAD_DIM = 128
ATTN_HEADS_PER_GROUP = 4
DILATED_GROUPS = ((128, 1), (512, 4), (2048, 16))
ATTN_HEADS = ATTN_HEADS_PER_GROUP * len(DILATED_GROUPS)
ATTN_WIDTH = ATTN_HEADS * ATTN_HEAD_DIM
ATTN_OUT_WIDTH = ATTN_HEADS_PER_GROUP * ATTN_HEAD_DIM
POOL_WINDOWS = (2, 4, 8, 16)
POOL_WIDTH = 1024
POOL_GROUP_WIDTH = POOL_WIDTH // len(POOL_WINDOWS)
N_BRANCHES = 3
D_FF = 4 * D_MODEL
EPS = 1e-6
LOG2E = math.log2(math.e)

LANES = 128
SUBLANES = 8
VMEM_LIMIT_BYTES = 56 * 1024 * 1024

ATTN_GROUP_WIDTH = 3 * ATTN_HEADS_PER_GROUP * ATTN_HEAD_DIM
QKV_WIDTH = 3 * ATTN_WIDTH
UG_TILE = 512
UG_TILES = (POOL_WIDTH + N_BRANCHES * D_MODEL) // UG_TILE
TILES_PER_BRANCH = D_MODEL // UG_TILE
UG_PIECES = 2
POOL_HALO = 16
CONV_HALO = 8
ATTN_UNROLL = 8


def _alibi_slopes(n):
    def pow2(k):
        start = 2.0 ** (-8.0 / k)
        return [start ** (i + 1) for i in range(k)]
    if math.log2(n).is_integer():
        s = pow2(n)
    else:
        c = 2 ** math.floor(math.log2(n))
        s = pow2(c) + pow2(2 * c)[0::2][: n - c]
    return np.sort(np.asarray(s, np.float32))[::-1].copy()


def _cparams(sem):
    return pltpu.CompilerParams(dimension_semantics=sem, vmem_limit_bytes=VMEM_LIMIT_BYTES)


def _sigmoid(x):
    return 0.5 * (1.0 + jnp.tanh(0.5 * x))


def _silu(x):
    return x * _sigmoid(x)


def _modulated_rms(x, nw, sc, sh):
    y = x * lax.rsqrt(jnp.mean(x * x, axis=-1, keepdims=True) + EPS) * nw
    return y * (1.0 + sc) + sh


def _adaln_kernel(c_ref, w_ref, b_ref, o_ref):
    cond = _silu(c_ref[...])
    o_ref[0] = jnp.dot(cond, w_ref[0], preferred_element_type=f32,
                       precision=lax.Precision.HIGHEST) + b_ref[0]


def _adaln(c, w_ada, b_ada):
    depth, d, n = w_ada.shape
    rows = SUBLANES * pl.cdiv(c.shape[0], SUBLANES)
    c_pad = jnp.pad(c, ((0, rows - c.shape[0]), (0, 0)))
    tn = n // 4
    return pl.pallas_call(
        _adaln_kernel,
        out_shape=jax.ShapeDtypeStruct((depth, rows, n), f32),
        grid=(depth, n // tn),
        in_specs=[pl.BlockSpec((rows, d), lambda l, j: (0, 0)),
                  pl.BlockSpec((1, d, tn), lambda l, j: (l, 0, j)),
                  pl.BlockSpec((1, 1, tn), lambda l, j: (l, 0, j))],
        out_specs=pl.BlockSpec((1, rows, tn), lambda l, j: (l, 0, j)),
        compiler_params=_cparams(("parallel", "parallel")),
        name="adaln",
    )(c_pad, w_ada, b_ada.reshape(depth, 1, n))


def _qkv_proj_kernel(x_ref, nw_ref, sc_ref, sh_ref, w_ref, hw_ref, h_ref, *rest):
    outs, buf = rest[:-1], rest[-1]
    e = ATTN_HEAD_DIM
    gw = ATTN_GROUP_WIDTH
    tm = x_ref.shape[1]
    h_ref[0] = _modulated_rms(x_ref[0], nw_ref[...], sc_ref[0], sh_ref[0]).astype(bf16)
    for c in range(w_ref.shape[1] // (2 * e)):
        cols = slice(c * 2 * e, (c + 1) * 2 * e)
        gi, lc = divmod(c * 2 * e, gw)
        dil = DILATED_GROUPS[gi][1]
        y = jnp.dot(h_ref[0], w_ref[:, cols], preferred_element_type=f32)
        heads = [y[:, half * e:(half + 1) * e] for half in range(2)]
        if lc < 2 * gw // 3:
            heads = [t * lax.rsqrt(jnp.mean(t * t, axis=-1, keepdims=True) + EPS) for t in heads]
            heads = [t * hw_ref[:, c * 2 * e + half * e:c * 2 * e + (half + 1) * e]
                     for half, t in enumerate(heads)]
        for half, t in enumerate(heads):
            lcols = slice(lc + half * e, lc + (half + 1) * e)
            if dil == 1:
                outs[gi][0, 0, :, lcols] = t.astype(bf16)
            else:
                slot = (c % 2) * 2 + half
                buf[slot] = t
                for r in range(dil):
                    outs[gi][0, r, :, lcols] = buf[slot, pl.ds(r, tm // dil, stride=dil), :].astype(bf16)


def _qkv_proj(x, nw, sc, sh, w, head_w, tm=1024):
    b, s, d = x.shape
    n = w.shape[1]
    gw = ATTN_GROUP_WIDTH
    vec = pl.BlockSpec((1, 1, d), lambda i, j: (i, 0, 0))
    once = dict(pipeline_mode=pl.Buffered(1))
    dils = [dil for _, dil in DILATED_GROUPS]
    return pl.pallas_call(
        _qkv_proj_kernel,
        out_shape=[jax.ShapeDtypeStruct((b, s, d), bf16)]
                  + [jax.ShapeDtypeStruct((b, dil, s // dil, gw), bf16) for dil in dils],
        grid=(b, s // tm),
        in_specs=[pl.BlockSpec((1, tm, d), lambda i, j: (i, j, 0)),
                  pl.BlockSpec((1, d), lambda i, j: (0, 0)), vec, vec,
                  pl.BlockSpec((d, n), lambda i, j: (0, 0), **once),
                  pl.BlockSpec((1, n), lambda i, j: (0, 0))],
        out_specs=[pl.BlockSpec((1, tm, d), lambda i, j: (i, j, 0))]
                  + [pl.BlockSpec((1, dil, tm // dil, gw), lambda i, j: (i, 0, j, 0)) for dil in dils],
        scratch_shapes=[pltpu.VMEM((4, tm, ATTN_HEAD_DIM), f32)],
        compiler_params=_cparams(("parallel", "parallel")),
        name="qkv_proj",
    )(x, nw.reshape(1, d), sc, sh, w, head_w)


def _expand_heads(v):
    lane = lax.broadcasted_iota(jnp.int32, (v.shape[0], LANES), 1)
    tiles = [jnp.where(lane < SSD_HEAD_DIM, v[:, 2 * p:2 * p + 1], v[:, 2 * p + 1:2 * p + 2])
             for p in range(SSD_HEADS // 2)]
    return jnp.concatenate(tiles, axis=1)


def _ssd_chunk(h_c, z_c, xbc_c, wdt_ref, cw_ref, cb_ref, dtb_ref, alog_ref, dsk_ref, nw_ref, ext, conv, state,
               side=()):
    side = list(side)
    run_side = lambda: (side.pop(0) or (lambda: None))() if side else None
    L = SSD_CHUNK
    N = SSD_STATE
    DI = SSD_D_INNER
    GW = DI // SSD_GROUPS

    R = L // SUBLANES
    base = CONV_HALO - (SSD_CONV - 1)
    xbc_raw = xbc_c.astype(f32)
    run_side()
    for c in range(SSD_CONV_DIM // LANES):
        cols = slice(c * LANES, (c + 1) * LANES)
        ext[c, CONV_HALO:CONV_HALO + L, :] = xbc_raw[:, cols]
        taps = [jnp.broadcast_to(cw_ref[k:k + 1, cols], (SUBLANES, LANES)) for k in range(SSD_CONV)]
        bias = jnp.broadcast_to(cb_ref[:, cols], (SUBLANES, LANES))
        rows = [ext[c, pl.ds(base + j, SUBLANES, stride=R), :] for j in range(R + SSD_CONV - 1)]
        for v in range(R):
            acc = bias
            for k in range(SSD_CONV):
                acc = acc + rows[v + k] * taps[k]
            conv[c, pl.ds(v, SUBLANES, stride=R), :] = _silu(acc)
        ext[c, 0:CONV_HALO, :] = ext[c, L:L + CONV_HALO, :]
    xs = jnp.concatenate([conv[c] for c in range(DI // LANES)], axis=1)
    bc = jnp.concatenate([conv[c] for c in range(DI // LANES, SSD_CONV_DIM // LANES)], axis=1)

    dtr = jnp.dot(h_c, wdt_ref[...], preferred_element_type=f32) + dtb_ref[...]
    dt = jnp.maximum(dtr, 0.0) + jnp.log1p(jnp.exp(-jnp.abs(dtr)))
    a2 = dt * (-LOG2E * jnp.exp(alog_ref[...]))
    ti = lax.broadcasted_iota(jnp.int32, (L, L), 0)
    si = lax.broadcasted_iota(jnp.int32, (L, L), 1)
    causal = ti >= si
    a_cum = jnp.dot(causal.astype(f32), a2, preferred_element_type=f32,
                    precision=lax.Precision.HIGHEST)
    a_cum_t = a_cum.T

    dt_e = _expand_heads(dt)
    acum_e = _expand_heads(a_cum)
    alast_e = acum_e[L - 1:L, :]
    x_dt = xs * dt_e
    x_dec = (x_dt * jnp.exp2(alast_e - acum_e)).astype(bf16)
    x_dt_b = x_dt.astype(bf16)
    e_acum = jnp.exp2(acum_e)
    e_alast = jnp.exp2(alast_e)
    lane = lax.broadcasted_iota(jnp.int32, (L, LANES), 1)
    lo = lane < SSD_HEAD_DIM
    zero_b = jnp.zeros((L, LANES), bf16)

    ys = []
    for g in range(SSD_GROUPS):
        run_side()
        b_g = bc[:, g * N:(g + 1) * N]
        c_g = bc[:, (SSD_GROUPS + g) * N:(SSD_GROUPS + g + 1) * N].astype(bf16)
        cb = lax.dot_general(c_g, b_g.astype(bf16), (((1,), (1,)), ((), ())),
                             preferred_element_type=f32)
        st = state[g]
        y_off = jnp.dot(c_g, st.astype(bf16), preferred_element_type=f32) * e_acum[:, g * GW:(g + 1) * GW]
        new = jnp.dot(b_g.T.astype(bf16), x_dec[:, g * GW:(g + 1) * GW], preferred_element_type=f32)
        state[g] = e_alast[:, g * GW:(g + 1) * GW] * st + new
        tiles = []
        for p in range(GW // LANES):
            col0 = g * GW + p * LANES
            xp = x_dt_b[:, col0:col0 + LANES]
            acc = None
            for half in range(2):
                h = col0 // SSD_HEAD_DIM + half
                seg = a_cum[:, h:h + 1] - a_cum_t[h:h + 1, :]
                m_h = (cb * jnp.exp2(jnp.where(causal, seg, -jnp.inf))).astype(bf16)
                x_h = jnp.where(lo if half == 0 else jnp.logical_not(lo), xp, zero_b)
                part = jnp.dot(m_h, x_h, preferred_element_type=f32)
                acc = part if acc is None else acc + part
            tiles.append(acc)
        ys.append(jnp.concatenate(tiles, axis=1) + y_off)
    run_side()
    y = jnp.concatenate(ys, axis=1) + dsk_ref[...] * xs
    y = y * _silu(z_c.astype(f32))
    outs = []
    for g in range(SSD_GROUPS):
        yg = y[:, g * GW:(g + 1) * GW]
        outs.append(yg * lax.rsqrt(jnp.mean(yg * yg, axis=-1, keepdims=True) + EPS))
    return (jnp.concatenate(outs, axis=1) * nw_ref[...]).astype(bf16)


def _ssd_proj_kernel(h_ref, wz_ref, wug_ref, wdt_ref, cw_ref, cb_ref, dtb_ref, alog_ref, dsk_ref, nw_ref,
                     y_ref, ug_ref, zx, ext, conv, state):
    L = SSD_CHUNK
    DI = SSD_D_INNER

    @pl.when(pl.program_id(1) == 0)
    def _():
        ext[:, 0:CONV_HALO, :] = jnp.zeros((SSD_CONV_DIM // LANES, CONV_HALO, LANES), f32)
        state[...] = jnp.zeros_like(state)

    zx[...] = jnp.dot(h_ref[0], wz_ref[...], preferred_element_type=f32).astype(zx.dtype)

    def chunk(c, carry):
        rows = pl.ds(pl.multiple_of(c * L, L), L)
        def piece(n0, n1):
            def run():
                ug_ref[0, c, :, n0:n1] = jnp.dot(h_ref[0], wug_ref[c, :, n0:n1],
                                                 preferred_element_type=f32).astype(ug_ref.dtype)
            return run

        tw = ug_ref.shape[3]
        pieces = [piece(p * tw // UG_PIECES, (p + 1) * tw // UG_PIECES) for p in range(UG_PIECES)]
        side = [pieces[0], pieces[1]]
        y_ref[0, rows, :] = _ssd_chunk(h_ref[0, rows, :], zx[rows, 0:DI], zx[rows, DI:], wdt_ref, cw_ref,
                                       cb_ref, dtb_ref, alog_ref, dsk_ref, nw_ref, ext, conv, state, side)
        return carry

    lax.fori_loop(0, ug_ref.shape[1], chunk, 0)


def _pad_lanes(v):
    return jnp.pad(v, [(0, 0)] * (v.ndim - 1) + [(0, LANES - v.shape[-1])])


def _ssd_proj(h, w_z, w_ug, w_dt, conv_w, conv_b, dt_bias, a_log, d_skip, ssd_norm_w):
    b, s, d = h.shape
    L = SSD_CHUNK
    di = SSD_D_INNER
    tiles, _, tw = w_ug.shape
    tm = tiles * L
    row = lambda v: v.reshape(1, -1)
    once = dict(pipeline_mode=pl.Buffered(1))
    full = lambda shape: pl.BlockSpec(shape, lambda i, j: (0,) * len(shape))
    return pl.pallas_call(
        _ssd_proj_kernel,
        out_shape=(jax.ShapeDtypeStruct((b, s, di), bf16), jax.ShapeDtypeStruct((b, tiles, s, tw), bf16)),
        grid=(b, s // tm),
        in_specs=[pl.BlockSpec((1, tm, d), lambda i, j: (i, j, 0)),
                  pl.BlockSpec(w_z.shape, lambda i, j: (0, 0), **once),
                  pl.BlockSpec(w_ug.shape, lambda i, j: (0, 0, 0), **once),
                  full((d, LANES)), full((SSD_CONV, SSD_CONV_DIM)), full((1, SSD_CONV_DIM)),
                  full((1, LANES)), full((1, LANES)), full((1, di)), full((1, di))],
        out_specs=(pl.BlockSpec((1, tm, di), lambda i, j: (i, j, 0)),
                   pl.BlockSpec((1, tiles, tm, tw), lambda i, j: (i, 0, j, 0))),
        scratch_shapes=[pltpu.VMEM((tm, di + SSD_CONV_DIM), bf16),
                        pltpu.VMEM((SSD_CONV_DIM // LANES, L + CONV_HALO, LANES), f32),
                        pltpu.VMEM((SSD_CONV_DIM // LANES, L, LANES), f32),
                        pltpu.VMEM((SSD_GROUPS, SSD_STATE, di // SSD_GROUPS), f32)],
        compiler_params=_cparams(("parallel", "arbitrary")),
        name="ssd_proj",
    )(h, w_z, w_ug, w_dt, conv_w, row(conv_b), row(_pad_lanes(dt_bias)), row(_pad_lanes(a_log)),
      row(jnp.repeat(d_skip, SSD_HEAD_DIM)), row(ssd_norm_w))


def _attn_group(gi, dil, slope, q, k, v, og, lse):
    seq = q.shape[0] * q.shape[1]
    steps = DILATED_GROUPS[gi][0] // dil
    e = ATTN_HEAD_DIM
    nb = seq // (dil * steps)
    qi = lax.broadcasted_iota(jnp.int32, (steps, 2 * steps), 0)
    kj = lax.broadcasted_iota(jnp.int32, (steps, 2 * steps), 1)
    neg = jnp.float32(-jnp.inf)
    rel = qi + steps - kj
    bias = jnp.where((rel >= 0) & (rel <= steps), (-slope * dil) * rel.astype(f32), neg)
    bias_first = jnp.where(kj >= steps, bias, neg)
    ones = jnp.ones((2 * steps, e), bf16)
    nt = (((1,), (1,)), ((), ()))

    def block(it):
        r = it // nb
        i = it % nb
        rows = pl.ds(pl.multiple_of(it * steps, steps), steps)
        band = pl.ds(pl.multiple_of(it * steps, steps), 2 * steps)
        vb = jnp.concatenate([v[band, :], ones], axis=1)
        qb = q[r, pl.ds(pl.multiple_of(i * steps, steps), steps), :]
        s = lax.dot_general(qb, k[band, :], nt, preferred_element_type=f32)
        s = s + jnp.where(i > 0, bias, bias_first)
        m = jnp.max(s, axis=-1, keepdims=True)
        p = jnp.exp(s - m).astype(bf16)
        oe = jnp.dot(p, vb, preferred_element_type=f32)
        l = oe[:, e:]
        out_rows = rows if dil == 1 else pl.ds(r + i * (steps * dil), steps, stride=dil)
        og[gi, out_rows, :] = oe[:, :e] / l
        lse[gi, out_rows, :] = m + jnp.log(l)

    def body(it, carry):
        for u in range(ATTN_UNROLL):
            block(it * ATTN_UNROLL + u)
        return carry

    lax.fori_loop(0, dil * nb // ATTN_UNROLL, body, 0)


def _attn_kernel(slopes_ref, *refs):
    ng = len(DILATED_GROUPS)
    ins, o_ref = refs[:3 * ng], refs[3 * ng]
    stage, og, lse = refs[3 * ng + 1:]
    j = pl.program_id(1)
    seq = o_ref.shape[1]
    rt = 512

    pad = stage.shape[1] - seq
    stage[:, 0:pad, :] = jnp.zeros((2, pad, stage.shape[2]), stage.dtype)
    for gi, (_, dil) in enumerate(DILATED_GROUPS):
        q, k, v = ins[3 * gi:3 * gi + 3]
        n = seq // dil
        for a, part in enumerate((k, v)):
            for r in range(dil):
                stage[a, pad + r * n:pad + (r + 1) * n, :] = part[0, r]
        _attn_group(gi, dil, slopes_ref[gi, j], q.at[0], stage.at[0], stage.at[1], og, lse)

    def merge(t, carry):
        rows = pl.ds(pl.multiple_of(t * rt, rt), rt)
        ls = [lse[gi, rows, :] for gi in range(ng)]
        top = functools.reduce(jnp.maximum, ls)
        ws = [jnp.exp(x - top) for x in ls]
        num = functools.reduce(jnp.add, [w * og[gi, rows, :] for gi, w in enumerate(ws)])
        o_ref[0, rows, :] = (num / functools.reduce(jnp.add, ws)).astype(o_ref.dtype)
        return carry

    lax.fori_loop(0, seq // rt, merge, 0)


def _attention(groups):
    e = ATTN_HEAD_DIM
    hg = ATTN_HEADS_PER_GROUP
    ng = len(DILATED_GROUPS)
    b = groups[0].shape[0]
    s = groups[0].shape[1] * groups[0].shape[2]
    slopes = jnp.asarray(_alibi_slopes(ATTN_HEADS)).reshape(ng, hg)
    args, specs = [], []
    for arr, (win, dil) in zip(groups, DILATED_GROUPS):
        assert s % win == 0 and win // dil == DILATED_GROUPS[0][0]
        assert (s // (win // dil)) % ATTN_UNROLL == 0
        for a in range(3):
            args.append(arr)
            specs.append(pl.BlockSpec((1, dil, s // dil, e), lambda i, j, a=a: (i, 0, 0, a * hg + j)))
    return pl.pallas_call(
        _attn_kernel,
        out_shape=jax.ShapeDtypeStruct((b, s, ATTN_OUT_WIDTH), bf16),
        grid=(b, hg),
        in_specs=[pl.BlockSpec(memory_space=pltpu.SMEM)] + specs,
        out_specs=pl.BlockSpec((1, s, e), lambda i, j: (i, 0, j)),
        scratch_shapes=[pltpu.VMEM((2, s + DILATED_GROUPS[0][0], e), bf16)] + [pltpu.VMEM((ng, s, e), f32)] * 2,
        compiler_params=_cparams(("parallel", "parallel")),
        name="dilated_attn",
    )(slopes, *args)


def _join_tiles(ref):
    return jnp.concatenate([ref[0, t] for t in range(ref.shape[1])], axis=1).astype(f32)


def _pooled(u_ref, halo_ref, wmix_ref, scale_ref, ext):
    ts = u_ref.shape[2]
    gw = POOL_GROUP_WIDTH
    i = pl.program_id(1)
    halo = _join_tiles(halo_ref)
    ext[0:POOL_HALO, :] = jnp.where(i > 0, halo, jnp.zeros_like(halo))
    ext[POOL_HALO:POOL_HALO + ts, :] = _join_tiles(u_ref)
    pos = i * ts + lax.broadcasted_iota(jnp.int32, (ts, 1), 0)
    outs = []
    for gi, w in enumerate(POOL_WINDOWS):
        cols = slice(gi * gw, (gi + 1) * gw)
        cur = ext[pl.ds(POOL_HALO, ts), cols]
        tot = cur
        for k in range(1, w):
            tot = tot + ext[pl.ds(POOL_HALO - k, ts), cols]
        count = jnp.minimum(pos + 1, w).astype(f32)
        pooled = tot / count - cur
        outs.append(jnp.dot(pooled.astype(bf16), wmix_ref[gi], preferred_element_type=f32))
    return (jnp.concatenate(outs, axis=1) * scale_ref[...]).astype(bf16)


def _merge_kernel(x_ref, ys_ref, ya_ref, u_ref, halo_ref, gs_ref, ga_ref, gp_ref, g1_ref,
                  ws_ref, wa_ref, wmix_ref, pscale_ref, wp_ref, wo_ref, o_ref, ext):
    dot = functools.partial(jnp.dot, preferred_element_type=f32)
    gate = lambda ref: _sigmoid(_join_tiles(ref))
    y_pool = _pooled(u_ref, halo_ref, wmix_ref, pscale_ref, ext)
    m = (gate(gs_ref) * dot(ys_ref[0], ws_ref[...])
         + gate(ga_ref) * dot(ya_ref[0], wa_ref[...])
         + gate(gp_ref) * dot(y_pool, wp_ref[...]))
    o_ref[0] = x_ref[0] + g1_ref[0] * dot(m.astype(bf16), wo_ref[...])


def _merge(x, y_ssd, y_attn, ug, g1, w_ssd_out, w_attn_out, w_pool_mix, pool_scale, w_pool_out, w_out, ts=512):
    b, s, d = x.shape
    tpb = TILES_PER_BRANCH
    assert POOL_WIDTH == tpb * UG_TILE and ts % POOL_HALO == 0
    tile = lambda w: pl.BlockSpec((1, ts, w), lambda i, j: (i, j, 0))
    full = lambda w: pl.BlockSpec(w.shape, lambda i, j: (0,) * w.ndim)
    branch = lambda t: pl.BlockSpec((1, tpb, ts, UG_TILE), lambda i, j: (i, t, j, 0))
    per = ts // POOL_HALO
    halo = pl.BlockSpec((1, tpb, POOL_HALO, UG_TILE), lambda i, j: (i, 0, jnp.maximum(j * per - 1, 0), 0))
    return pl.pallas_call(
        _merge_kernel,
        out_shape=jax.ShapeDtypeStruct((b, s, d), f32),
        grid=(b, s // ts),
        in_specs=[tile(d), tile(y_ssd.shape[-1]), tile(y_attn.shape[-1]),
                  branch(0), halo, branch(1), branch(2), branch(3),
                  pl.BlockSpec((1, 1, d), lambda i, j: (i, 0, 0)),
                  full(w_ssd_out), full(w_attn_out), full(w_pool_mix),
                  pl.BlockSpec((1, POOL_WIDTH), lambda i, j: (0, 0)), full(w_pool_out), full(w_out)],
        out_specs=tile(d),
        scratch_shapes=[pltpu.VMEM((ts + POOL_HALO, POOL_WIDTH), f32)],
        compiler_params=_cparams(("parallel", "parallel")),
        name="merge",
    )(x, y_ssd, y_attn, ug, ug, ug, ug, ug, g1, w_ssd_out, w_attn_out, w_pool_mix,
      pool_scale.reshape(1, POOL_WIDTH), w_pool_out, w_out)


def _ffn_kernel(x_ref, nw_ref, sc_ref, sh_ref, g2_ref, w1_ref, w2_ref, o_ref, hs, acc):
    k = pl.program_id(2)

    @pl.when(k == 0)
    def _():
        hs[...] = _modulated_rms(x_ref[0], nw_ref[...], sc_ref[0], sh_ref[0]).astype(bf16)
        acc[...] = jnp.zeros_like(acc)

    a = jnp.maximum(jnp.dot(hs[...], w1_ref[...], preferred_element_type=f32), 0.0)
    acc[...] += jnp.dot((a * a).astype(bf16), w2_ref[...], preferred_element_type=f32)

    @pl.when(k == pl.num_programs(2) - 1)
    def _():
        o_ref[0] = x_ref[0] + g2_ref[0] * acc[...]


def _ffn(x, nw, sc, sh, g2, w1, w2, ts=1024, tf=2048):
    b, s, d = x.shape
    ff = w1.shape[1]
    vec = pl.BlockSpec((1, 1, d), lambda i, j, k: (i, 0, 0))
    return pl.pallas_call(
        _ffn_kernel,
        out_shape=jax.ShapeDtypeStruct((b, s, d), f32),
        grid=(b, s // ts, ff // tf),
        in_specs=[pl.BlockSpec((1, ts, d), lambda i, j, k: (i, j, 0)),
                  pl.BlockSpec((1, d), lambda i, j, k: (0, 0)), vec, vec, vec,
                  pl.BlockSpec((d, tf), lambda i, j, k: (0, k)),
                  pl.BlockSpec((tf, d), lambda i, j, k: (k, 0))],
        out_specs=pl.BlockSpec((1, ts, d), lambda i, j, k: (i, j, 0)),
        scratch_shapes=[pltpu.VMEM((ts, d), bf16), pltpu.VMEM((ts, d), f32)],
        compiler_params=_cparams(("parallel", "parallel", "arbitrary")),
        name="ffn",
    )(x, nw.reshape(1, d), sc, sh, g2, w1, w2)


def kernel(x, c, w_ada, b_ada, norm1_w, norm2_w, w_in, conv_w, conv_b, dt_bias, a_log, d_skip, ssd_norm_w, w_ssd_out, q_norm_w, k_norm_w, w_attn_out, w_pool_mix, pool_scale, w_pool_out, w_out, w_ff1, w_ff2):
    b, s, d = x.shape
    depth = w_in.shape[0]
    mod = _adaln(c, w_ada, b_ada)[:, :b].reshape(depth, b, 6, 1, d)
    dt0 = SSD_D_INNER + SSD_CONV_DIM
    q0 = dt0 + SSD_HEADS
    u0 = q0 + QKV_WIDTH
    gcols = ATTN_HEADS_PER_GROUP * ATTN_HEAD_DIM
    w_qkv = jnp.concatenate(
        [w_in[:, :, q0 + a * ATTN_WIDTH + g * gcols:q0 + a * ATTN_WIDTH + (g + 1) * gcols]
         for g in range(len(DILATED_GROUPS)) for a in range(3)], axis=-1).astype(bf16)
    w_z = w_in[:, :, :dt0].astype(bf16)
    w_ug = w_in[:, :, u0:].astype(bf16).reshape(depth, d, UG_TILES, UG_TILE).transpose(0, 2, 1, 3)
    w_dt = _pad_lanes(w_in[:, :, dt0:q0]).astype(bf16)
    per_group = jnp.concatenate([jnp.tile(q_norm_w * ATTN_HEAD_DIM ** -0.5, (1, ATTN_HEADS_PER_GROUP)),
                                 jnp.tile(k_norm_w, (1, ATTN_HEADS_PER_GROUP)),
                                 jnp.ones((depth, gcols), f32)], axis=-1)
    head_w = jnp.tile(per_group, (1, len(DILATED_GROUPS)))[:, None, :]
    cast = lambda w: w.astype(bf16)
    w_ssd_out, w_attn_out, w_pool_mix, w_pool_out, w_out, w_ff1, w_ff2 = map(
        cast, (w_ssd_out, w_attn_out, w_pool_mix, w_pool_out, w_out, w_ff1, w_ff2))
    for l in range(depth):
        sh1, sc1, g1, sh2, sc2, g2 = [mod[l, :, i] for i in range(6)]
        h, *qkv_groups = _qkv_proj(x, norm1_w[l], sc1, sh1, w_qkv[l], head_w[l])
        y_ssd, ug = _ssd_proj(h, w_z[l], w_ug[l], w_dt[l], conv_w[l], conv_b[l], dt_bias[l], a_log[l],
                              d_skip[l], ssd_norm_w[l])
        y_attn = _attention(qkv_groups)
        x = _merge(x, y_ssd, y_attn, ug, g1, w_ssd_out[l], w_attn_out[l], w_pool_mix[l], pool_scale[l],
                   w_pool_out[l], w_out[l])
        x = _ffn(x, norm2_w[l], sc2, sh2, g2, w_ff1[l], w_ff2[l])
    return x
```

```python
import functools
import math

import numpy as np
import jax
import jax.numpy as jnp
from jax import lax
from jax.experimental import pallas as pl
from jax.experimental.pallas import tpu as pltpu

f32 = jnp.float32
bf16 = jnp.bfloat16

D_MODEL = 1024
SSD_D_INNER = 1024
SSD_HEAD_DIM = 64
SSD_HEADS = SSD_D_INNER // SSD_HEAD_DIM
SSD_GROUPS = 2
SSD_STATE = 128
SSD_CONV = 4
SSD_CHUNK = 128
SSD_BC = 2 * SSD_GROUPS * SSD_STATE
SSD_CONV_DIM = SSD_D_INNER + SSD_BC
ATTN_HEAD_DIM = 128
ATTN_HEADS_PER_GROUP = 4
DILATED_GROUPS = ((128, 1), (512, 4), (2048, 16))
ATTN_HEADS = ATTN_HEADS_PER_GROUP * len(DILATED_GROUPS)
ATTN_WIDTH = ATTN_HEADS * ATTN_HEAD_DIM
ATTN_OUT_WIDTH = ATTN_HEADS_PER_GROUP * ATTN_HEAD_DIM
POOL_WINDOWS = (2, 4, 8, 16)
POOL_WIDTH = 1024
POOL_GROUP_WIDTH = POOL_WIDTH // len(POOL_WINDOWS)
N_BRANCHES = 3
D_FF = 4 * D_MODEL
EPS = 1e-6
LOG2E = math.log2(math.e)

LANES = 128
SUBLANES = 8
VMEM_LIMIT_BYTES = 56 * 1024 * 1024

ATTN_GROUP_WIDTH = 3 * ATTN_HEADS_PER_GROUP * ATTN_HEAD_DIM
QKV_WIDTH = 3 * ATTN_WIDTH
UG_TILE = 512
UG_TILES = (POOL_WIDTH + N_BRANCHES * D_MODEL) // UG_TILE
TILES_PER_BRANCH = D_MODEL // UG_TILE
UG_PIECES = 2
POOL_HALO = 32
CONV_HALO = 8
ATTN_UNROLL = 16


def _alibi_slopes(n):
    def pow2(k):
        start = 2.0 ** (-8.0 / k)
        return [start ** (i + 1) for i in range(k)]
    if math.log2(n).is_integer():
        s = pow2(n)
    else:
        c = 2 ** math.floor(math.log2(n))
        s = pow2(c) + pow2(2 * c)[0::2][: n - c]
    return np.sort(np.asarray(s, np.float32))[::-1].copy()


def _cparams(sem):
    return pltpu.CompilerParams(dimension_semantics=sem, vmem_limit_bytes=VMEM_LIMIT_BYTES)


def _sigmoid(x):
    return 0.5 * (1.0 + jnp.tanh(0.5 * x))


def _silu(x):
    return x * _sigmoid(x)


def _modulated_rms(x, nw, sc, sh):
    y = x * lax.rsqrt(jnp.mean(x * x, axis=-1, keepdims=True) + EPS) * nw
    return y * (1.0 + sc) + sh


def _adaln_kernel(c_ref, w_ref, b_ref, o_ref):
    cond = _silu(c_ref[...])
    o_ref[0] = jnp.dot(cond, w_ref[0], preferred_element_type=f32,
                       precision=lax.Precision.HIGHEST) + b_ref[0]


def _adaln(c, w_ada, b_ada):
    depth, d, n = w_ada.shape
    rows = SUBLANES * pl.cdiv(c.shape[0], SUBLANES)
    c_pad = jnp.pad(c, ((0, rows - c.shape[0]), (0, 0)))
    tn = n // 4
    return pl.pallas_call(
        _adaln_kernel,
        out_shape=jax.ShapeDtypeStruct((depth, rows, n), f32),
        grid=(depth, n // tn),
        in_specs=[pl.BlockSpec((rows, d), lambda l, j: (0, 0)),
                  pl.BlockSpec((1, d, tn), lambda l, j: (l, 0, j)),
                  pl.BlockSpec((1, 1, tn), lambda l, j: (l, 0, j))],
        out_specs=pl.BlockSpec((1, rows, tn), lambda l, j: (l, 0, j)),
        compiler_params=_cparams(("parallel", "parallel")),
        name="adaln",
    )(c_pad, w_ada, b_ada.reshape(depth, 1, n))


def _qkv_proj_kernel(x_ref, nw_ref, sc_ref, sh_ref, w_ref, hw_ref, h_ref, *rest):
    ng = len(DILATED_GROUPS)
    outs, (hf, hp) = rest[:ng], rest[ng:]
    e = ATTN_HEAD_DIM
    gw = ATTN_GROUP_WIDTH
    tm, d = x_ref.shape[1], x_ref.shape[2]
    hn = _modulated_rms(x_ref[0], nw_ref[...], sc_ref[0], sh_ref[0])
    h_ref[0] = hn.astype(bf16)
    for ct in range(d // LANES):
        hf[ct] = hn[:, ct * LANES:(ct + 1) * LANES]
    lhs = []
    for gi, (_, dil) in enumerate(DILATED_GROUPS):
        if dil == 1:
            lhs.append(h_ref.at[0])
            continue
        n = tm // dil
        slot = len([1 for _, dd in DILATED_GROUPS[:gi] if dd > 1])
        for ct in range(d // LANES):
            for r in range(dil):
                hp[slot, r * n:(r + 1) * n, ct * LANES:(ct + 1) * LANES] = (
                    hf[ct, pl.ds(r, n, stride=dil), :].astype(bf16))
        lhs.append(hp.at[slot])
    for c in range(w_ref.shape[1] // (2 * e)):
        cols = slice(c * 2 * e, (c + 1) * 2 * e)
        gi, lc = divmod(c * 2 * e, gw)
        dil = DILATED_GROUPS[gi][1]
        n = tm // dil
        y = jnp.dot(lhs[gi][...], w_ref[:, cols], preferred_element_type=f32)
        if lc < 2 * gw // 3:
            heads = [y[:, half * e:(half + 1) * e] for half in range(2)]
            heads = [t * lax.rsqrt(jnp.mean(t * t, axis=-1, keepdims=True) + EPS) for t in heads]
            y = jnp.concatenate(heads, axis=1) * hw_ref[:, cols]
        y = y.astype(bf16)
        for r in range(dil):
            outs[gi][0, r, :, lc:lc + 2 * e] = y[r * n:(r + 1) * n, :]


def _qkv_proj(x, nw, sc, sh, w, head_w, tm=1024):
    b, s, d = x.shape
    n = w.shape[1]
    gw = ATTN_GROUP_WIDTH
    vec = pl.BlockSpec((1, 1, d), lambda i, j: (i, 0, 0))
    once = dict(pipeline_mode=pl.Buffered(1))
    dils = [dil for _, dil in DILATED_GROUPS]
    return pl.pallas_call(
        _qkv_proj_kernel,
        out_shape=[jax.ShapeDtypeStruct((b, s, d), bf16)]
                  + [jax.ShapeDtypeStruct((b, dil, s // dil, gw), bf16) for dil in dils],
        grid=(b, s // tm),
        in_specs=[pl.BlockSpec((1, tm, d), lambda i, j: (i, j, 0)),
                  pl.BlockSpec((1, d), lambda i, j: (0, 0)), vec, vec,
                  pl.BlockSpec((d, n), lambda i, j: (0, 0), **once),
                  pl.BlockSpec((1, n), lambda i, j: (0, 0))],
        out_specs=[pl.BlockSpec((1, tm, d), lambda i, j: (i, j, 0))]
                  + [pl.BlockSpec((1, dil, tm // dil, gw), lambda i, j: (i, 0, j, 0)) for dil in dils],
        scratch_shapes=[pltpu.VMEM((d // LANES, tm, LANES), f32),
                        pltpu.VMEM((len([1 for dil in dils if dil > 1]), tm, d), bf16)],
        compiler_params=_cparams(("parallel", "parallel")),
        name="qkv_proj",
    )(x, nw.reshape(1, d), sc, sh, w, head_w)


def _expand_heads(v):
    lane = lax.broadcasted_iota(jnp.int32, (v.shape[0], LANES), 1)
    tiles = [jnp.where(lane < SSD_HEAD_DIM, v[:, 2 * p:2 * p + 1], v[:, 2 * p + 1:2 * p + 2])
             for p in range(SSD_HEADS // 2)]
    return jnp.concatenate(tiles, axis=1)


def _ssd_chunk(h_c, z_c, xbc_c, wdt_ref, cw_ref, cb_ref, dtb_ref, alog_ref, dsk_ref, nw_ref, ext, conv, state,
               side=()):
    side = list(side)
    run_side = lambda: (side.pop(0) or (lambda: None))() if side else None
    L = SSD_CHUNK
    N = SSD_STATE
    DI = SSD_D_INNER
    GW = DI // SSD_GROUPS

    R = L // SUBLANES
    base = CONV_HALO - (SSD_CONV - 1)
    xbc_raw = xbc_c.astype(f32)
    run_side()
    for c in range(SSD_CONV_DIM // LANES):
        cols = slice(c * LANES, (c + 1) * LANES)
        ext[c, CONV_HALO:CONV_HALO + L, :] = xbc_raw[:, cols]
        taps = [jnp.broadcast_to(cw_ref[k:k + 1, cols], (SUBLANES, LANES)) for k in range(SSD_CONV)]
        bias = jnp.broadcast_to(cb_ref[:, cols], (SUBLANES, LANES))
        rows = [ext[c, pl.ds(base + j, SUBLANES, stride=R), :] for j in range(R + SSD_CONV - 1)]
        for v in range(R):
            acc = bias
            for k in range(SSD_CONV):
                acc = acc + rows[v + k] * taps[k]
            conv[c, pl.ds(v, SUBLANES, stride=R), :] = _silu(acc)
        ext[c, 0:CONV_HALO, :] = ext[c, L:L + CONV_HALO, :]
    xs = jnp.concatenate([conv[c] for c in range(DI // LANES)], axis=1)
    bc = jnp.concatenate([conv[c] for c in range(DI // LANES, SSD_CONV_DIM // LANES)], axis=1)

    dtr = jnp.dot(h_c, wdt_ref[...], preferred_element_type=f32) + dtb_ref[...]
    dt = jnp.maximum(dtr, 0.0) + jnp.log1p(jnp.exp(-jnp.abs(dtr)))
    a2 = dt * (-LOG2E * jnp.exp(alog_ref[...]))
    ti = lax.broadcasted_iota(jnp.int32, (L, L), 0)
    si = lax.broadcasted_iota(jnp.int32, (L, L), 1)
    causal = ti >= si
    a_cum = jnp.dot(causal.astype(f32), a2, preferred_element_type=f32,
                    precision=lax.Precision.HIGHEST)
    a_cum_t = a_cum.T

    dt_e = _expand_heads(dt)
    acum_e = _expand_heads(a_cum)
    alast_e = acum_e[L - 1:L, :]
    x_dt = xs * dt_e
    x_dec = (x_dt * jnp.exp2(alast_e - acum_e)).astype(bf16)
    x_dt_b = x_dt.astype(bf16)
    e_acum = jnp.exp2(acum_e)
    e_alast = jnp.exp2(alast_e)
    lane = lax.broadcasted_iota(jnp.int32, (L, LANES), 1)
    lo = lane < SSD_HEAD_DIM
    zero_b = jnp.zeros((L, LANES), bf16)

    ys = []
    for g in range(SSD_GROUPS):
        run_side()
        b_g = bc[:, g * N:(g + 1) * N]
        c_g = bc[:, (SSD_GROUPS + g) * N:(SSD_GROUPS + g + 1) * N].astype(bf16)
        cb = lax.dot_general(c_g, b_g.astype(bf16), (((1,), (1,)), ((), ())),
                             preferred_element_type=f32)
        st = state[g]
        y_off = jnp.dot(c_g, st.astype(bf16), preferred_element_type=f32) * e_acum[:, g * GW:(g + 1) * GW]
        new = jnp.dot(b_g.T.astype(bf16), x_dec[:, g * GW:(g + 1) * GW], preferred_element_type=f32)
        state[g] = e_alast[:, g * GW:(g + 1) * GW] * st + new
        tiles = []
        for p in range(GW // LANES):
            col0 = g * GW + p * LANES
            xp = x_dt_b[:, col0:col0 + LANES]
            acc = None
            for half in range(2):
                h = col0 // SSD_HEAD_DIM + half
                seg = a_cum[:, h:h + 1] - a_cum_t[h:h + 1, :]
                m_h = (cb * jnp.exp2(jnp.where(causal, seg, -jnp.inf))).astype(bf16)
                x_h = jnp.where(lo if half == 0 else jnp.logical_not(lo), xp, zero_b)
                part = jnp.dot(m_h, x_h, preferred_element_type=f32)
                acc = part if acc is None else acc + part
            tiles.append(acc)
        ys.append(jnp.concatenate(tiles, axis=1) + y_off)
    run_side()
    y = jnp.concatenate(ys, axis=1) + dsk_ref[...] * xs
    y = y * _silu(z_c.astype(f32))
    outs = []
    for g in range(SSD_GROUPS):
        yg = y[:, g * GW:(g + 1) * GW]
        outs.append(yg * lax.rsqrt(jnp.mean(yg * yg, axis=-1, keepdims=True) + EPS))
    return (jnp.concatenate(outs, axis=1) * nw_ref[...]).astype(bf16)


def _ssd_proj_kernel(h_ref, wz_ref, wug_ref, wdt_ref, cw_ref, cb_ref, dtb_ref, alog_ref, dsk_ref, nw_ref,
                     y_ref, ug_ref, zx, ext, conv, state):
    L = SSD_CHUNK
    DI = SSD_D_INNER

    @pl.when(pl.program_id(1) == 0)
    def _():
        ext[:, 0:CONV_HALO, :] = jnp.zeros((SSD_CONV_DIM // LANES, CONV_HALO, LANES), f32)
        state[...] = jnp.zeros_like(state)

    zx[...] = jnp.dot(h_ref[0], wz_ref[...], preferred_element_type=f32).astype(zx.dtype)

    def chunk(c, carry):
        rows = pl.ds(pl.multiple_of(c * L, L), L)
        def piece(n0, n1):
            def run():
                ug_ref[0, c, :, n0:n1] = jnp.dot(h_ref[0], wug_ref[c, :, n0:n1],
                                                 preferred_element_type=f32).astype(ug_ref.dtype)
            return run

        tw = ug_ref.shape[3]
        pieces = [piece(p * tw // UG_PIECES, (p + 1) * tw // UG_PIECES) for p in range(UG_PIECES)]
        side = [pieces[0], pieces[1]]
        y_ref[0, rows, :] = _ssd_chunk(h_ref[0, rows, :], zx[rows, 0:DI], zx[rows, DI:], wdt_ref, cw_ref,
                                       cb_ref, dtb_ref, alog_ref, dsk_ref, nw_ref, ext, conv, state, side)
        return carry

    lax.fori_loop(0, ug_ref.shape[1], chunk, 0)


def _pad_lanes(v):
    return jnp.pad(v, [(0, 0)] * (v.ndim - 1) + [(0, LANES - v.shape[-1])])


def _ssd_proj(h, w_z, w_ug, w_dt, conv_w, conv_b, dt_bias, a_log, d_skip, ssd_norm_w):
    b, s, d = h.shape
    L = SSD_CHUNK
    di = SSD_D_INNER
    tiles, _, tw = w_ug.shape
    tm = tiles * L
    row = lambda v: v.reshape(1, -1)
    once = dict(pipeline_mode=pl.Buffered(1))
    full = lambda shape: pl.BlockSpec(shape, lambda i, j: (0,) * len(shape))
    return pl.pallas_call(
        _ssd_proj_kernel,
        out_shape=(jax.ShapeDtypeStruct((b, s, di), bf16), jax.ShapeDtypeStruct((b, tiles, s, tw), bf16)),
        grid=(b, s // tm),
        in_specs=[pl.BlockSpec((1, tm, d), lambda i, j: (i, j, 0)),
                  pl.BlockSpec(w_z.shape, lambda i, j: (0, 0), **once),
                  pl.BlockSpec(w_ug.shape, lambda i, j: (0, 0, 0), **once),
                  full((d, LANES)), full((SSD_CONV, SSD_CONV_DIM)), full((1, SSD_CONV_DIM)),
                  full((1, LANES)), full((1, LANES)), full((1, di)), full((1, di))],
        out_specs=(pl.BlockSpec((1, tm, di), lambda i, j: (i, j, 0)),
                   pl.BlockSpec((1, tiles, tm, tw), lambda i, j: (i, 0, j, 0))),
        scratch_shapes=[pltpu.VMEM((tm, di + SSD_CONV_DIM), bf16),
                        pltpu.VMEM((SSD_CONV_DIM // LANES, L + CONV_HALO, LANES), f32),
                        pltpu.VMEM((SSD_CONV_DIM // LANES, L, LANES), f32),
                        pltpu.VMEM((SSD_GROUPS, SSD_STATE, di // SSD_GROUPS), f32)],
        compiler_params=_cparams(("parallel", "arbitrary")),
        name="ssd_proj",
    )(h, w_z, w_ug, w_dt, conv_w, row(conv_b), row(_pad_lanes(dt_bias)), row(_pad_lanes(a_log)),
      row(jnp.repeat(d_skip, SSD_HEAD_DIM)), row(ssd_norm_w))


def _attn_group(gi, dil, slope, q, k, v, og, lse):
    seq = q.shape[0] * q.shape[1]
    steps = DILATED_GROUPS[gi][0] // dil
    e = ATTN_HEAD_DIM
    nb = seq // (dil * steps)
    qi = lax.broadcasted_iota(jnp.int32, (steps, 2 * steps), 0)
    kj = lax.broadcasted_iota(jnp.int32, (steps, 2 * steps), 1)
    neg = jnp.float32(-jnp.inf)
    rel = qi + steps - kj
    bias = jnp.where((rel >= 0) & (rel <= steps), (-slope * dil) * rel.astype(f32), neg)
    bias_first = jnp.where(kj >= steps, bias, neg)
    ones = jnp.ones((2 * steps, e), bf16)
    nt = (((1,), (1,)), ((), ()))

    def block(it):
        r = it // nb
        i = it % nb
        rows = pl.ds(pl.multiple_of(it * steps, steps), steps)
        band = pl.ds(pl.multiple_of(it * steps, steps), 2 * steps)
        vb = jnp.concatenate([v[band, :], ones], axis=1)
        qb = q[r, pl.ds(pl.multiple_of(i * steps, steps), steps), :]
        s = lax.dot_general(qb, k[band, :], nt, preferred_element_type=f32)
        s = s + jnp.where(i > 0, bias, bias_first)
        m = jnp.max(s, axis=-1, keepdims=True)
        p = jnp.exp(s - m).astype(bf16)
        oe = jnp.dot(p, vb, preferred_element_type=f32)
        l = oe[:, e:]
        out_rows = rows if dil == 1 else pl.ds(r + i * (steps * dil), steps, stride=dil)
        og[gi, out_rows, :] = oe[:, :e] / l
        lse[gi, out_rows, :] = m + jnp.log(l)

    def body(it, carry):
        for u in range(ATTN_UNROLL):
            block(it * ATTN_UNROLL + u)
        return carry

    lax.fori_loop(0, dil * nb // ATTN_UNROLL, body, 0)


def _attn_kernel(slopes_ref, *refs):
    ng = len(DILATED_GROUPS)
    ins, o_ref = refs[:3 * ng], refs[3 * ng]
    stage, og, lse = refs[3 * ng + 1:]
    j = pl.program_id(1)
    seq = o_ref.shape[1]
    rt = 512

    pad = stage.shape[1] - seq
    stage[:, 0:pad, :] = jnp.zeros((2, pad, stage.shape[2]), stage.dtype)
    for gi, (_, dil) in enumerate(DILATED_GROUPS):
        q, k, v = ins[3 * gi:3 * gi + 3]
        n = seq // dil
        for a, part in enumerate((k, v)):
            for r in range(dil):
                stage[a, pad + r * n:pad + (r + 1) * n, :] = part[0, r]
        _attn_group(gi, dil, slopes_ref[gi, j], q.at[0], stage.at[0], stage.at[1], og, lse)

    def merge(t, carry):
        rows = pl.ds(pl.multiple_of(t * rt, rt), rt)
        ls = [lse[gi, rows, :] for gi in range(ng)]
        top = functools.reduce(jnp.maximum, ls)
        ws = [jnp.exp(x - top) for x in ls]
        num = functools.reduce(jnp.add, [w * og[gi, rows, :] for gi, w in enumerate(ws)])
        o_ref[0, rows, :] = (num / functools.reduce(jnp.add, ws)).astype(o_ref.dtype)
        return carry

    lax.fori_loop(0, seq // rt, merge, 0)


def _attention(groups):
    e = ATTN_HEAD_DIM
    hg = ATTN_HEADS_PER_GROUP
    ng = len(DILATED_GROUPS)
    b = groups[0].shape[0]
    s = groups[0].shape[1] * groups[0].shape[2]
    slopes = jnp.asarray(_alibi_slopes(ATTN_HEADS)).reshape(ng, hg)
    args, specs = [], []
    for arr, (win, dil) in zip(groups, DILATED_GROUPS):
        assert s % win == 0 and win // dil == DILATED_GROUPS[0][0]
        assert (s // (win // dil)) % ATTN_UNROLL == 0
        for a in range(3):
            args.append(arr)
            specs.append(pl.BlockSpec((1, dil, s // dil, e), lambda i, j, a=a: (i, 0, 0, a * hg + j)))
    return pl.pallas_call(
        _attn_kernel,
        out_shape=jax.ShapeDtypeStruct((b, s, ATTN_OUT_WIDTH), bf16),
        grid=(b, hg),
        in_specs=[pl.BlockSpec(memory_space=pltpu.SMEM)] + specs,
        out_specs=pl.BlockSpec((1, s, e), lambda i, j: (i, 0, j)),
        scratch_shapes=[pltpu.VMEM((2, s + DILATED_GROUPS[0][0], e), bf16)] + [pltpu.VMEM((ng, s, e), f32)] * 2,
        compiler_params=_cparams(("parallel", "parallel")),
        name="dilated_attn",
    )(slopes, *args)


def _join_tiles(ref):
    return jnp.concatenate([ref[0, t] for t in range(ref.shape[1])], axis=1).astype(f32)


def _pooled(u_ref, halo_ref, wmix_ref, scale_ref, ext, buf):
    ts = u_ref.shape[2]
    gw = POOL_GROUP_WIDTH
    n = POOL_HALO + ts
    i = pl.program_id(1)
    halo = _join_tiles(halo_ref)
    ext[0:POOL_HALO, :] = jnp.where(i > 0, halo, jnp.zeros_like(halo))
    ext[POOL_HALO:n, :] = _join_tiles(u_ref)
    src, w, lo = ext, 1, 0
    for gi, win in enumerate(POOL_WINDOWS):
        assert win == 2 * w
        lo = SUBLANES * pl.cdiv(lo + w, SUBLANES)
        dst = buf.at[gi % 2]
        dst[lo:n, gi * gw:] = src[lo:n, gi * gw:] + src[pl.ds(lo - w, n - lo), gi * gw:]
        src, w = dst, win
    assert lo <= POOL_HALO
    pos = i * ts + lax.broadcasted_iota(jnp.int32, (ts, 1), 0)
    outs = []
    for gi, win in enumerate(POOL_WINDOWS):
        cols = slice(gi * gw, (gi + 1) * gw)
        count = jnp.minimum(pos + 1, win).astype(f32)
        pooled = buf[gi % 2, POOL_HALO:n, cols] / count - ext[POOL_HALO:n, cols]
        outs.append(jnp.dot(pooled.astype(bf16), wmix_ref[gi], preferred_element_type=f32))
    return (jnp.concatenate(outs, axis=1) * scale_ref[...]).astype(bf16)


def _merge_kernel(x_ref, ys_ref, ya_ref, u_ref, halo_ref, gs_ref, ga_ref, gp_ref, g1_ref,
                  ws_ref, wa_ref, wmix_ref, pscale_ref, wp_ref, wo_ref, o_ref, ext, buf):
    dot = functools.partial(jnp.dot, preferred_element_type=f32)
    gate = lambda ref: _sigmoid(_join_tiles(ref))
    y_pool = _pooled(u_ref, halo_ref, wmix_ref, pscale_ref, ext, buf)
    m = (gate(gs_ref) * dot(ys_ref[0], ws_ref[...])
         + gate(ga_ref) * dot(ya_ref[0], wa_ref[...])
         + gate(gp_ref) * dot(y_pool, wp_ref[...]))
    o_ref[0] = x_ref[0] + g1_ref[0] * dot(m.astype(bf16), wo_ref[...])


def _merge(x, y_ssd, y_attn, ug, g1, w_ssd_out, w_attn_out, w_pool_mix, pool_scale, w_pool_out, w_out, ts=512):
    b, s, d = x.shape
    tpb = TILES_PER_BRANCH
    assert POOL_WIDTH == tpb * UG_TILE and ts % POOL_HALO == 0
    tile = lambda w: pl.BlockSpec((1, ts, w), lambda i, j: (i, j, 0))
    full = lambda w: pl.BlockSpec(w.shape, lambda i, j: (0,) * w.ndim)
    branch = lambda t: pl.BlockSpec((1, tpb, ts, UG_TILE), lambda i, j: (i, t, j, 0))
    per = ts // POOL_HALO
    halo = pl.BlockSpec((1, tpb, POOL_HALO, UG_TILE), lambda i, j: (i, 0, jnp.maximum(j * per - 1, 0), 0))
    return pl.pallas_call(
        _merge_kernel,
        out_shape=jax.ShapeDtypeStruct((b, s, d), f32),
        grid=(b, s // ts),
        in_specs=[tile(d), tile(y_ssd.shape[-1]), tile(y_attn.shape[-1]),
                  branch(0), halo, branch(1), branch(2), branch(3),
                  pl.BlockSpec((1, 1, d), lambda i, j: (i, 0, 0)),
                  full(w_ssd_out), full(w_attn_out), full(w_pool_mix),
                  pl.BlockSpec((1, POOL_WIDTH), lambda i, j: (0, 0)), full(w_pool_out), full(w_out)],
        out_specs=tile(d),
        scratch_shapes=[pltpu.VMEM((ts + POOL_HALO, POOL_WIDTH), f32),
                        pltpu.VMEM((2, ts + POOL_HALO, POOL_WIDTH), f32)],
        compiler_params=_cparams(("parallel", "parallel")),
        name="merge",
    )(x, y_ssd, y_attn, ug, ug, ug, ug, ug, g1, w_ssd_out, w_attn_out, w_pool_mix,
      pool_scale.reshape(1, POOL_WIDTH), w_pool_out, w_out)


def _ffn_kernel(x_ref, nw_ref, sc_ref, sh_ref, g2_ref, w1_ref, w2_ref, o_ref, hs, acc):
    k = pl.program_id(2)

    @pl.when(k == 0)
    def _():
        hs[...] = _modulated_rms(x_ref[0], nw_ref[...], sc_ref[0], sh_ref[0]).astype(bf16)
        acc[...] = jnp.zeros_like(acc)

    a = jnp.maximum(jnp.dot(hs[...], w1_ref[...], preferred_element_type=f32), 0.0)
    acc[...] += jnp.dot((a * a).astype(bf16), w2_ref[...], preferred_element_type=f32)

    @pl.when(k == pl.num_programs(2) - 1)
    def _():
        o_ref[0] = x_ref[0] + g2_ref[0] * acc[...]


def _ffn(x, nw, sc, sh, g2, w1, w2, ts=1024, tf=2048):
    b, s, d = x.shape
    ff = w1.shape[1]
    vec = pl.BlockSpec((1, 1, d), lambda i, j, k: (i, 0, 0))
    return pl.pallas_call(
        _ffn_kernel,
        out_shape=jax.ShapeDtypeStruct((b, s, d), f32),
        grid=(b, s // ts, ff // tf),
        in_specs=[pl.BlockSpec((1, ts, d), lambda i, j, k: (i, j, 0)),
                  pl.BlockSpec((1, d), lambda i, j, k: (0, 0)), vec, vec, vec,
                  pl.BlockSpec((d, tf), lambda i, j, k: (0, k)),
                  pl.BlockSpec((tf, d), lambda i, j, k: (k, 0))],
        out_specs=pl.BlockSpec((1, ts, d), lambda i, j, k: (i, j, 0)),
        scratch_shapes=[pltpu.VMEM((ts, d), bf16), pltpu.VMEM((ts, d), f32)],
        compiler_params=_cparams(("parallel", "parallel", "arbitrary")),
        name="ffn",
    )(x, nw.reshape(1, d), sc, sh, g2, w1, w2)


def kernel(x, c, w_ada, b_ada, norm1_w, norm2_w, w_in, conv_w, conv_b, dt_bias, a_log, d_skip, ssd_norm_w, w_ssd_out, q_norm_w, k_norm_w, w_attn_out, w_pool_mix, pool_scale, w_pool_out, w_out, w_ff1, w_ff2):
    b, s, d = x.shape
    depth = w_in.shape[0]
    mod = _adaln(c, w_ada, b_ada)[:, :b].reshape(depth, b, 6, 1, d)
    dt0 = SSD_D_INNER + SSD_CONV_DIM
    q0 = dt0 + SSD_HEADS
    u0 = q0 + QKV_WIDTH
    gcols = ATTN_HEADS_PER_GROUP * ATTN_HEAD_DIM
    w_qkv = jnp.concatenate(
        [w_in[:, :, q0 + a * ATTN_WIDTH + g * gcols:q0 + a * ATTN_WIDTH + (g + 1) * gcols]
         for g in range(len(DILATED_GROUPS)) for a in range(3)], axis=-1).astype(bf16)
    w_z = w_in[:, :, :dt0].astype(bf16)
    w_ug = w_in[:, :, u0:].astype(bf16).reshape(depth, d, UG_TILES, UG_TILE).transpose(0, 2, 1, 3)
    w_dt = _pad_lanes(w_in[:, :, dt0:q0]).astype(bf16)
    per_group = jnp.concatenate([jnp.tile(q_norm_w * ATTN_HEAD_DIM ** -0.5, (1, ATTN_HEADS_PER_GROUP)),
                                 jnp.tile(k_norm_w, (1, ATTN_HEADS_PER_GROUP)),
                                 jnp.ones((depth, gcols), f32)], axis=-1)
    head_w = jnp.tile(per_group, (1, len(DILATED_GROUPS)))[:, None, :]
    cast = lambda w: w.astype(bf16)
    w_ssd_out, w_attn_out, w_pool_mix, w_pool_out, w_out, w_ff1, w_ff2 = map(
        cast, (w_ssd_out, w_attn_out, w_pool_mix, w_pool_out, w_out, w_ff1, w_ff2))
    for l in range(depth):
        sh1, sc1, g1, sh2, sc2, g2 = [mod[l, :, i] for i in range(6)]
        h, *qkv_groups = _qkv_proj(x, norm1_w[l], sc1, sh1, w_qkv[l], head_w[l])
        y_ssd, ug = _ssd_proj(h, w_z[l], w_ug[l], w_dt[l], conv_w[l], conv_b[l], dt_bias[l], a_log[l],
                              d_skip[l], ssd_norm_w[l])
        y_attn = _attention(qkv_groups)
        x = _merge(x, y_ssd, y_attn, ug, g1, w_ssd_out[l], w_attn_out[l], w_pool_mix[l], pool_scale[l],
                   w_pool_out[l], w_out[l])
        x = _ffn(x, norm2_w[l], sc2, sh2, g2, w_ff1[l], w_ff2[l])
    return x
```

```python
import functools
import math

import numpy as np
import jax
import jax.numpy as jnp
from jax import lax
from jax.experimental import pallas as pl
from jax.experimental.pallas import tpu as pltpu

f32 = jnp.float32
bf16 = jnp.bfloat16

D_MODEL = 1024
SSD_D_INNER = 1024
SSD_HEAD_DIM = 64
SSD_HEADS = SSD_D_INNER // SSD_HEAD_DIM
SSD_GROUPS = 2
SSD_STATE = 128
SSD_CONV = 4
SSD_CHUNK = 128
SSD_BC = 2 * SSD_GROUPS * SSD_STATE
SSD_CONV_DIM = SSD_D_INNER + SSD_BC
ATTN_HEAD_DIM = 128
ATTN_HEADS_PER_GROUP = 4
DILATED_GROUPS = ((128, 1), (512, 4), (2048, 16))
ATTN_HEADS = ATTN_HEADS_PER_GROUP * len(DILATED_GROUPS)
ATTN_WIDTH = ATTN_HEADS * ATTN_HEAD_DIM
ATTN_OUT_WIDTH = ATTN_HEADS_PER_GROUP * ATTN_HEAD_DIM
POOL_WINDOWS = (2, 4, 8, 16)
POOL_WIDTH = 1024
POOL_GROUP_WIDTH = POOL_WIDTH // len(POOL_WINDOWS)
N_BRANCHES = 3
D_FF = 4 * D_MODEL
EPS = 1e-6
LOG2E = math.log2(math.e)

LANES = 128
SUBLANES = 8
VMEM_LIMIT_BYTES = 56 * 1024 * 1024

ATTN_GROUP_WIDTH = 3 * ATTN_HEADS_PER_GROUP * ATTN_HEAD_DIM
QKV_WIDTH = 3 * ATTN_WIDTH
UG_TILE = 512
UG_TILES = (POOL_WIDTH + N_BRANCHES * D_MODEL) // UG_TILE
TILES_PER_BRANCH = D_MODEL // UG_TILE
UG_PIECES = 2
POOL_HALO = 32
CONV_HALO = 8
ATTN_UNROLL = 16


def _alibi_slopes(n):
    def pow2(k):
        start = 2.0 ** (-8.0 / k)
        return [start ** (i + 1) for i in range(k)]
    if math.log2(n).is_integer():
        s = pow2(n)
    else:
        c = 2 ** math.floor(math.log2(n))
        s = pow2(c) + pow2(2 * c)[0::2][: n - c]
    return np.sort(np.asarray(s, np.float32))[::-1].copy()


def _cparams(sem):
    return pltpu.CompilerParams(dimension_semantics=sem, vmem_limit_bytes=VMEM_LIMIT_BYTES)


def _sigmoid(x):
    return 0.5 * (1.0 + jnp.tanh(0.5 * x))


def _silu(x):
    return x * _sigmoid(x)


def _modulated_rms(x, nw, sc, sh):
    y = x * lax.rsqrt(jnp.mean(x * x, axis=-1, keepdims=True) + EPS) * nw
    return y * (1.0 + sc) + sh


MOD_SHIFT1, MOD_SCALE1, MOD_GATE1, MOD_SHIFT2, MOD_SCALE2, MOD_GATE2 = range(6)


def _mod_spec(mod, l, which):
    return pl.BlockSpec((1, 1) + mod.shape[2:], lambda *_: (l, which, 0, 0))


def _mod_row(ref):
    return ref[0, 0, pl.ds(pl.program_id(0), 1), :]


def _layer_spec(w, l, **kw):
    return pl.BlockSpec((1,) + w.shape[1:], lambda *_: (l,) + (0,) * (w.ndim - 1), **kw)


def _adaln_kernel(c_ref, w_ref, b_ref, o_ref):
    cond = _silu(c_ref[...])
    o_ref[0] = jnp.dot(cond, w_ref[0], preferred_element_type=f32,
                       precision=lax.Precision.HIGHEST) + b_ref[0]


def _adaln(c, w_ada, b_ada):
    depth, d, n = w_ada.shape
    rows = SUBLANES * pl.cdiv(c.shape[0], SUBLANES)
    c_pad = jnp.pad(c, ((0, rows - c.shape[0]), (0, 0)))
    tn = n // 4
    return pl.pallas_call(
        _adaln_kernel,
        out_shape=jax.ShapeDtypeStruct((depth, rows, n), f32),
        grid=(depth, n // tn),
        in_specs=[pl.BlockSpec((rows, d), lambda l, j: (0, 0)),
                  pl.BlockSpec((1, d, tn), lambda l, j: (l, 0, j)),
                  pl.BlockSpec((1, 1, tn), lambda l, j: (l, 0, j))],
        out_specs=pl.BlockSpec((1, rows, tn), lambda l, j: (l, 0, j)),
        compiler_params=_cparams(("parallel", "parallel")),
        name="adaln",
    )(c_pad, w_ada, b_ada.reshape(depth, 1, n))


def _qkv_proj_kernel(x_ref, nw_ref, sc_ref, sh_ref, w_ref, hw_ref, h_ref, *rest):
    ng = len(DILATED_GROUPS)
    outs, (hf, hp) = rest[:ng], rest[ng:]
    e = ATTN_HEAD_DIM
    gw = ATTN_GROUP_WIDTH
    tm, d = x_ref.shape[1], x_ref.shape[2]
    hn = _modulated_rms(x_ref[0], nw_ref[...], _mod_row(sc_ref), _mod_row(sh_ref))
    h_ref[0] = hn.astype(bf16)
    for ct in range(d // LANES):
        hf[ct] = hn[:, ct * LANES:(ct + 1) * LANES]
    lhs = []
    for gi, (_, dil) in enumerate(DILATED_GROUPS):
        if dil == 1:
            lhs.append(h_ref.at[0])
            continue
        n = tm // dil
        slot = len([1 for _, dd in DILATED_GROUPS[:gi] if dd > 1])
        for ct in range(d // LANES):
            for r in range(dil):
                hp[slot, r * n:(r + 1) * n, ct * LANES:(ct + 1) * LANES] = (
                    hf[ct, pl.ds(r, n, stride=dil), :].astype(bf16))
        lhs.append(hp.at[slot])
    tw = w_ref.shape[3]
    for c in range(w_ref.shape[1] * tw // (2 * e)):
        cols = slice(c * 2 * e, (c + 1) * 2 * e)
        gi, lc = divmod(c * 2 * e, gw)
        dil = DILATED_GROUPS[gi][1]
        n = tm // dil
        t, tc = divmod(c * 2 * e, tw)
        y = jnp.dot(lhs[gi][...], w_ref[0, t, :, tc:tc + 2 * e], preferred_element_type=f32)
        if lc < 2 * gw // 3:
            heads = [y[:, half * e:(half + 1) * e] for half in range(2)]
            heads = [t * lax.rsqrt(jnp.mean(t * t, axis=-1, keepdims=True) + EPS) for t in heads]
            y = jnp.concatenate(heads, axis=1) * hw_ref[:, cols]
        y = y.astype(bf16)
        for r in range(dil):
            outs[gi][0, r, :, lc:lc + 2 * e] = y[r * n:(r + 1) * n, :]


def _qkv_proj(l, x, nw, mod, w, head_w, tm=1024):
    b, s, d = x.shape
    n = w.shape[1] * w.shape[3]
    gw = ATTN_GROUP_WIDTH
    dils = [dil for _, dil in DILATED_GROUPS]
    return pl.pallas_call(
        _qkv_proj_kernel,
        out_shape=[jax.ShapeDtypeStruct((b, s, d), bf16)]
                  + [jax.ShapeDtypeStruct((b, dil, s // dil, gw), bf16) for dil in dils],
        grid=(b, s // tm),
        in_specs=[pl.BlockSpec((1, tm, d), lambda i, j: (i, j, 0)),
                  pl.BlockSpec((1, d), lambda i, j: (0, 0)),
                  _mod_spec(mod, l, MOD_SCALE1), _mod_spec(mod, l, MOD_SHIFT1),
                  _layer_spec(w, l, pipeline_mode=pl.Buffered(1)),
                  pl.BlockSpec((1, n), lambda i, j: (0, 0))],
        out_specs=[pl.BlockSpec((1, tm, d), lambda i, j: (i, j, 0))]
                  + [pl.BlockSpec((1, dil, tm // dil, gw), lambda i, j: (i, 0, j, 0)) for dil in dils],
        scratch_shapes=[pltpu.VMEM((d // LANES, tm, LANES), f32),
                        pltpu.VMEM((len([1 for dil in dils if dil > 1]), tm, d), bf16)],
        compiler_params=_cparams(("parallel", "parallel")),
        name="qkv_proj",
    )(x, nw.reshape(1, d), mod, mod, w, head_w)


def _expand_heads(v):
    lane = lax.broadcasted_iota(jnp.int32, (v.shape[0], LANES), 1)
    tiles = [jnp.where(lane < SSD_HEAD_DIM, v[:, 2 * p:2 * p + 1], v[:, 2 * p + 1:2 * p + 2])
             for p in range(SSD_HEADS // 2)]
    return jnp.concatenate(tiles, axis=1)


def _ssd_chunk(h_c, z_c, xbc_c, wdt_ref, cw_ref, cb_ref, dtb_ref, alog_ref, dsk_ref, nw_ref, ext, conv, state,
               side=()):
    side = list(side)
    run_side = lambda: (side.pop(0) or (lambda: None))() if side else None
    L = SSD_CHUNK
    N = SSD_STATE
    DI = SSD_D_INNER
    GW = DI // SSD_GROUPS

    R = L // SUBLANES
    base = CONV_HALO - (SSD_CONV - 1)
    xbc_raw = xbc_c.astype(f32)
    run_side()
    for c in range(SSD_CONV_DIM // LANES):
        cols = slice(c * LANES, (c + 1) * LANES)
        ext[c, CONV_HALO:CONV_HALO + L, :] = xbc_raw[:, cols]
        taps = [jnp.broadcast_to(cw_ref[k:k + 1, cols], (SUBLANES, LANES)) for k in range(SSD_CONV)]
        bias = jnp.broadcast_to(cb_ref[:, cols], (SUBLANES, LANES))
        rows = [ext[c, pl.ds(base + j, SUBLANES, stride=R), :] for j in range(R + SSD_CONV - 1)]
        for v in range(R):
            acc = bias
            for k in range(SSD_CONV):
                acc = acc + rows[v + k] * taps[k]
            conv[c, pl.ds(v, SUBLANES, stride=R), :] = _silu(acc)
        ext[c, 0:CONV_HALO, :] = ext[c, L:L + CONV_HALO, :]
    xs = jnp.concatenate([conv[c] for c in range(DI // LANES)], axis=1)
    bc = jnp.concatenate([conv[c] for c in range(DI // LANES, SSD_CONV_DIM // LANES)], axis=1)

    dtr = jnp.dot(h_c, wdt_ref[0], preferred_element_type=f32) + dtb_ref[...]
    dt = jnp.maximum(dtr, 0.0) + jnp.log1p(jnp.exp(-jnp.abs(dtr)))
    a2 = dt * (-LOG2E * jnp.exp(alog_ref[...]))
    ti = lax.broadcasted_iota(jnp.int32, (L, L), 0)
    si = lax.broadcasted_iota(jnp.int32, (L, L), 1)
    causal = ti >= si
    a_cum = jnp.dot(causal.astype(f32), a2, preferred_element_type=f32,
                    precision=lax.Precision.HIGHEST)
    a_cum_t = a_cum.T

    dt_e = _expand_heads(dt)
    acum_e = _expand_heads(a_cum)
    alast_e = acum_e[L - 1:L, :]
    x_dt = xs * dt_e
    x_dec = (x_dt * jnp.exp2(alast_e - acum_e)).astype(bf16)
    x_dt_b = x_dt.astype(bf16)
    e_acum = jnp.exp2(acum_e)
    e_alast = jnp.exp2(alast_e)
    lane = lax.broadcasted_iota(jnp.int32, (L, LANES), 1)
    lo = lane < SSD_HEAD_DIM
    zero_b = jnp.zeros((L, LANES), bf16)

    ys = []
    for g in range(SSD_GROUPS):
        run_side()
        b_g = bc[:, g * N:(g + 1) * N]
        c_g = bc[:, (SSD_GROUPS + g) * N:(SSD_GROUPS + g + 1) * N].astype(bf16)
        cb = lax.dot_general(c_g, b_g.astype(bf16), (((1,), (1,)), ((), ())),
                             preferred_element_type=f32)
        st = state[g]
        y_off = jnp.dot(c_g, st.astype(bf16), preferred_element_type=f32) * e_acum[:, g * GW:(g + 1) * GW]
        new = jnp.dot(b_g.T.astype(bf16), x_dec[:, g * GW:(g + 1) * GW], preferred_element_type=f32)
        state[g] = e_alast[:, g * GW:(g + 1) * GW] * st + new
        tiles = []
        for p in range(GW // LANES):
            col0 = g * GW + p * LANES
            xp = x_dt_b[:, col0:col0 + LANES]
            ms, xh = [], []
            for half in range(2):
                h = col0 // SSD_HEAD_DIM + half
                seg = a_cum[:, h:h + 1] - a_cum_t[h:h + 1, :]
                ms.append((cb * jnp.exp2(jnp.where(causal, seg, -jnp.inf))).astype(bf16))
                xh.append(jnp.where(lo if half == 0 else jnp.logical_not(lo), xp, zero_b))
            tiles.append(jnp.dot(jnp.concatenate(ms, axis=1), jnp.concatenate(xh, axis=0),
                                 preferred_element_type=f32))
        ys.append(jnp.concatenate(tiles, axis=1) + y_off)
    run_side()
    y = jnp.concatenate(ys, axis=1) + dsk_ref[...] * xs
    y = y * _silu(z_c.astype(f32))
    outs = []
    for g in range(SSD_GROUPS):
        yg = y[:, g * GW:(g + 1) * GW]
        outs.append(yg * lax.rsqrt(jnp.mean(yg * yg, axis=-1, keepdims=True) + EPS))
    return (jnp.concatenate(outs, axis=1) * nw_ref[...]).astype(bf16)


def _ssd_proj_kernel(h_ref, wz_ref, wug_ref, wdt_ref, cw_ref, cb_ref, dtb_ref, alog_ref, dsk_ref, nw_ref,
                     y_ref, ug_ref, zx, ext, conv, state):
    L = SSD_CHUNK
    DI = SSD_D_INNER

    @pl.when(pl.program_id(1) == 0)
    def _():
        ext[:, 0:CONV_HALO, :] = jnp.zeros((SSD_CONV_DIM // LANES, CONV_HALO, LANES), f32)
        state[...] = jnp.zeros_like(state)

    zx[...] = jnp.dot(h_ref[0], wz_ref[0], preferred_element_type=f32).astype(zx.dtype)

    def chunk(c, carry):
        rows = pl.ds(pl.multiple_of(c * L, L), L)
        def piece(n0, n1):
            def run():
                ug_ref[0, c, :, n0:n1] = jnp.dot(h_ref[0], wug_ref[0, c, :, n0:n1],
                                                 preferred_element_type=f32).astype(ug_ref.dtype)
            return run

        tw = ug_ref.shape[3]
        pieces = [piece(p * tw // UG_PIECES, (p + 1) * tw // UG_PIECES) for p in range(UG_PIECES)]
        side = [pieces[0], pieces[1]]
        y_ref[0, rows, :] = _ssd_chunk(h_ref[0, rows, :], zx[rows, 0:DI], zx[rows, DI:], wdt_ref, cw_ref,
                                       cb_ref, dtb_ref, alog_ref, dsk_ref, nw_ref, ext, conv, state, side)
        return carry

    lax.fori_loop(0, ug_ref.shape[1], chunk, 0)


def _pad_lanes(v):
    return jnp.pad(v, [(0, 0)] * (v.ndim - 1) + [(0, LANES - v.shape[-1])])


def _ssd_proj(l, h, w_z, w_ug, w_dt, conv_w, conv_b, dt_bias, a_log, d_skip, ssd_norm_w):
    b, s, d = h.shape
    L = SSD_CHUNK
    di = SSD_D_INNER
    _, tiles, _, tw = w_ug.shape
    tm = tiles * L
    row = lambda v: v.reshape(1, -1)
    once = dict(pipeline_mode=pl.Buffered(1))
    full = lambda shape: pl.BlockSpec(shape, lambda i, j: (0,) * len(shape))
    return pl.pallas_call(
        _ssd_proj_kernel,
        out_shape=(jax.ShapeDtypeStruct((b, s, di), bf16), jax.ShapeDtypeStruct((b, tiles, s, tw), bf16)),
        grid=(b, s // tm),
        in_specs=[pl.BlockSpec((1, tm, d), lambda i, j: (i, j, 0)),
                  _layer_spec(w_z, l, **once), _layer_spec(w_ug, l, **once), _layer_spec(w_dt, l),
                  full((SSD_CONV, SSD_CONV_DIM)), full((1, SSD_CONV_DIM)),
                  full((1, LANES)), full((1, LANES)), full((1, di)), full((1, di))],
        out_specs=(pl.BlockSpec((1, tm, di), lambda i, j: (i, j, 0)),
                   pl.BlockSpec((1, tiles, tm, tw), lambda i, j: (i, 0, j, 0))),
        scratch_shapes=[pltpu.VMEM((tm, di + SSD_CONV_DIM), bf16),
                        pltpu.VMEM((SSD_CONV_DIM // LANES, L + CONV_HALO, LANES), f32),
                        pltpu.VMEM((SSD_CONV_DIM // LANES, L, LANES), f32),
                        pltpu.VMEM((SSD_GROUPS, SSD_STATE, di // SSD_GROUPS), f32)],
        compiler_params=_cparams(("parallel", "arbitrary")),
        name="ssd_proj",
    )(h, w_z, w_ug, w_dt, conv_w, row(conv_b), row(_pad_lanes(dt_bias)), row(_pad_lanes(a_log)),
      row(jnp.repeat(d_skip, SSD_HEAD_DIM)), row(ssd_norm_w))


def _attn_group(gi, dil, slope, q, k, v, og, lse):
    seq = q.shape[0] * q.shape[1]
    steps = DILATED_GROUPS[gi][0] // dil
    e = ATTN_HEAD_DIM
    nb = seq // (dil * steps)
    qi = lax.broadcasted_iota(jnp.int32, (steps, 2 * steps), 0)
    kj = lax.broadcasted_iota(jnp.int32, (steps, 2 * steps), 1)
    neg = jnp.float32(-jnp.inf)
    rel = qi + steps - kj
    bias = jnp.where((rel >= 0) & (rel <= steps), (-slope * dil) * rel.astype(f32), neg)
    bias_first = jnp.where(kj >= steps, bias, neg)
    ones = jnp.ones((2 * steps, e), bf16)
    nt = (((1,), (1,)), ((), ()))

    def block(it):
        r = it // nb
        i = it % nb
        rows = pl.ds(pl.multiple_of(it * steps, steps), steps)
        band = pl.ds(pl.multiple_of(it * steps, steps), 2 * steps)
        vb = jnp.concatenate([v[band, :], ones], axis=1)
        qb = q[r, pl.ds(pl.multiple_of(i * steps, steps), steps), :]
        s = lax.dot_general(qb, k[band, :], nt, preferred_element_type=f32)
        s = s + jnp.where(i > 0, bias, bias_first)
        m = jnp.max(s, axis=-1, keepdims=True)
        p = jnp.exp(s - m).astype(bf16)
        oe = jnp.dot(p, vb, preferred_element_type=f32)
        l = oe[:, e:]
        out_rows = rows if dil == 1 else pl.ds(r + i * (steps * dil), steps, stride=dil)
        og[gi, out_rows, :] = oe[:, :e] / l
        lse[gi, out_rows, :] = m + jnp.log(l)

    def body(it, carry):
        for u in range(ATTN_UNROLL):
            block(it * ATTN_UNROLL + u)
        return carry

    lax.fori_loop(0, dil * nb // ATTN_UNROLL, body, 0)


def _attn_kernel(slopes_ref, *refs):
    ng = len(DILATED_GROUPS)
    ins, o_ref = refs[:3 * ng], refs[3 * ng]
    stage, og, lse = refs[3 * ng + 1:]
    j = pl.program_id(1)
    seq = o_ref.shape[1]
    rt = 512

    pad = stage.shape[1] - seq
    stage[:, 0:pad, :] = jnp.zeros((2, pad, stage.shape[2]), stage.dtype)
    for gi, (_, dil) in enumerate(DILATED_GROUPS):
        q, k, v = ins[3 * gi:3 * gi + 3]
        n = seq // dil
        for a, part in enumerate((k, v)):
            for r in range(dil):
                stage[a, pad + r * n:pad + (r + 1) * n, :] = part[0, r]
        _attn_group(gi, dil, slopes_ref[gi, j], q.at[0], stage.at[0], stage.at[1], og, lse)

    def merge(t, carry):
        rows = pl.ds(pl.multiple_of(t * rt, rt), rt)
        ls = [lse[gi, rows, :] for gi in range(ng)]
        top = functools.reduce(jnp.maximum, ls)
        ws = [jnp.exp(x - top) for x in ls]
        num = functools.reduce(jnp.add, [w * og[gi, rows, :] for gi, w in enumerate(ws)])
        o_ref[0, rows, :] = (num / functools.reduce(jnp.add, ws)).astype(o_ref.dtype)
        return carry

    lax.fori_loop(0, seq // rt, merge, 0)


def _attention(groups):
    e = ATTN_HEAD_DIM
    hg = ATTN_HEADS_PER_GROUP
    ng = len(DILATED_GROUPS)
    b = groups[0].shape[0]
    s = groups[0].shape[1] * groups[0].shape[2]
    slopes = jnp.asarray(_alibi_slopes(ATTN_HEADS)).reshape(ng, hg)
    args, specs = [], []
    for arr, (win, dil) in zip(groups, DILATED_GROUPS):
        assert s % win == 0 and win // dil == DILATED_GROUPS[0][0]
        assert (s // (win // dil)) % ATTN_UNROLL == 0
        for a in range(3):
            args.append(arr)
            specs.append(pl.BlockSpec((1, dil, s // dil, e), lambda i, j, a=a: (i, 0, 0, a * hg + j)))
    return pl.pallas_call(
        _attn_kernel,
        out_shape=jax.ShapeDtypeStruct((b, s, ATTN_OUT_WIDTH), bf16),
        grid=(b, hg),
        in_specs=[pl.BlockSpec(memory_space=pltpu.SMEM)] + specs,
        out_specs=pl.BlockSpec((1, s, e), lambda i, j: (i, 0, j)),
        scratch_shapes=[pltpu.VMEM((2, s + DILATED_GROUPS[0][0], e), bf16)] + [pltpu.VMEM((ng, s, e), f32)] * 2,
        compiler_params=_cparams(("parallel", "parallel")),
        name="dilated_attn",
    )(slopes, *args)


def _join_tiles(ref):
    return jnp.concatenate([ref[0, t] for t in range(ref.shape[1])], axis=1).astype(f32)


def _pooled(u_ref, halo_ref, wmix_ref, scale_ref, ext, buf):
    ts = u_ref.shape[2]
    gw = POOL_GROUP_WIDTH
    n = POOL_HALO + ts
    i = pl.program_id(1)
    halo = _join_tiles(halo_ref)
    ext[0:POOL_HALO, :] = jnp.where(i > 0, halo, jnp.zeros_like(halo))
    ext[POOL_HALO:n, :] = _join_tiles(u_ref)
    src, w, lo = ext, 1, 0
    for gi, win in enumerate(POOL_WINDOWS):
        assert win == 2 * w
        lo = SUBLANES * pl.cdiv(lo + w, SUBLANES)
        dst = buf.at[gi % 2]
        dst[lo:n, gi * gw:] = src[lo:n, gi * gw:] + src[pl.ds(lo - w, n - lo), gi * gw:]
        src, w = dst, win
    assert lo <= POOL_HALO
    pos = i * ts + lax.broadcasted_iota(jnp.int32, (ts, 1), 0)
    outs = []
    for gi, win in enumerate(POOL_WINDOWS):
        cols = slice(gi * gw, (gi + 1) * gw)
        count = jnp.minimum(pos + 1, win).astype(f32)
        pooled = buf[gi % 2, POOL_HALO:n, cols] / count - ext[POOL_HALO:n, cols]
        outs.append(jnp.dot(pooled.astype(bf16), wmix_ref[gi], preferred_element_type=f32))
    return (jnp.concatenate(outs, axis=1) * scale_ref[...]).astype(bf16)


def _merge_kernel(x_ref, ys_ref, ya_ref, u_ref, halo_ref, gs_ref, ga_ref, gp_ref, g1_ref,
                  ws_ref, wa_ref, wmix_ref, pscale_ref, wp_ref, wo_ref, o_ref, ext, buf):
    dot = functools.partial(jnp.dot, preferred_element_type=f32)
    gate = lambda ref: _sigmoid(_join_tiles(ref))
    y_pool = _pooled(u_ref, halo_ref, wmix_ref.at[0], pscale_ref, ext, buf)
    m = (gate(gs_ref) * dot(ys_ref[0], ws_ref[0])
         + gate(ga_ref) * dot(ya_ref[0], wa_ref[0])
         + gate(gp_ref) * dot(y_pool, wp_ref[0]))
    o_ref[0] = x_ref[0] + _mod_row(g1_ref) * dot(m.astype(bf16), wo_ref[0])


def _merge(l, x, y_ssd, y_attn, ug, mod, w_ssd_out, w_attn_out, w_pool_mix, pool_scale, w_pool_out, w_out,
           ts=512):
    b, s, d = x.shape
    tpb = TILES_PER_BRANCH
    assert POOL_WIDTH == tpb * UG_TILE and ts % POOL_HALO == 0
    tile = lambda w: pl.BlockSpec((1, ts, w), lambda i, j: (i, j, 0))
    full = lambda w: _layer_spec(w, l)
    branch = lambda t: pl.BlockSpec((1, tpb, ts, UG_TILE), lambda i, j: (i, t, j, 0))
    per = ts // POOL_HALO
    halo = pl.BlockSpec((1, tpb, POOL_HALO, UG_TILE), lambda i, j: (i, 0, jnp.maximum(j * per - 1, 0), 0))
    return pl.pallas_call(
        _merge_kernel,
        out_shape=jax.ShapeDtypeStruct((b, s, d), f32),
        grid=(b, s // ts),
        in_specs=[tile(d), tile(y_ssd.shape[-1]), tile(y_attn.shape[-1]),
                  branch(0), halo, branch(1), branch(2), branch(3),
                  _mod_spec(mod, l, MOD_GATE1),
                  full(w_ssd_out), full(w_attn_out), full(w_pool_mix),
                  pl.BlockSpec((1, POOL_WIDTH), lambda i, j: (0, 0)), full(w_pool_out), full(w_out)],
        out_specs=tile(d),
        scratch_shapes=[pltpu.VMEM((ts + POOL_HALO, POOL_WIDTH), f32),
                        pltpu.VMEM((2, ts + POOL_HALO, POOL_WIDTH), f32)],
        compiler_params=_cparams(("parallel", "parallel")),
        name="merge",
    )(x, y_ssd, y_attn, ug, ug, ug, ug, ug, mod, w_ssd_out, w_attn_out, w_pool_mix,
      pool_scale.reshape(1, POOL_WIDTH), w_pool_out, w_out)


def _ffn_kernel(x_ref, nw_ref, sc_ref, sh_ref, g2_ref, w1_ref, w2_ref, o_ref, hs, acc):
    k = pl.program_id(2)

    @pl.when(k == 0)
    def _():
        hs[...] = _modulated_rms(x_ref[0], nw_ref[...], _mod_row(sc_ref), _mod_row(sh_ref)).astype(bf16)
        acc[...] = jnp.zeros_like(acc)

    a = jnp.maximum(jnp.dot(hs[...], w1_ref[0], preferred_element_type=f32), 0.0)
    acc[...] += jnp.dot((a * a).astype(bf16), w2_ref[0], preferred_element_type=f32)

    @pl.when(k == pl.num_programs(2) - 1)
    def _():
        o_ref[0] = x_ref[0] + _mod_row(g2_ref) * acc[...]


def _ffn(l, x, nw, mod, w1, w2, ts=1024, tf=2048):
    b, s, d = x.shape
    ff = w1.shape[2]
    return pl.pallas_call(
        _ffn_kernel,
        out_shape=jax.ShapeDtypeStruct((b, s, d), f32),
        grid=(b, s // ts, ff // tf),
        in_specs=[pl.BlockSpec((1, ts, d), lambda i, j, k: (i, j, 0)),
                  pl.BlockSpec((1, d), lambda i, j, k: (0, 0)),
                  _mod_spec(mod, l, MOD_SCALE2), _mod_spec(mod, l, MOD_SHIFT2), _mod_spec(mod, l, MOD_GATE2),
                  pl.BlockSpec((1, d, tf), lambda i, j, k: (l, 0, k)),
                  pl.BlockSpec((1, tf, d), lambda i, j, k: (l, k, 0))],
        out_specs=pl.BlockSpec((1, ts, d), lambda i, j, k: (i, j, 0)),
        scratch_shapes=[pltpu.VMEM((ts, d), bf16), pltpu.VMEM((ts, d), f32)],
        compiler_params=_cparams(("parallel", "parallel", "arbitrary")),
        name="ffn",
    )(x, nw.reshape(1, d), mod, mod, mod, w1, w2)


def _repack_kernel(w_ref, o_ref, *, shift):
    tw = o_ref.shape[3]
    o_ref[0, 0] = w_ref[0, :, shift:shift + tw].astype(o_ref.dtype)


def _repack(w_in, col0, tiles, tw, src_tile=lambda k: k):
    depth, d, n = w_in.shape
    shift = col0 % LANES
    base = col0 - shift
    window = tw + (LANES if shift else 0)
    overhang = max(0, base + tiles * tw + (window - tw) - n)
    return pl.pallas_call(
        functools.partial(_repack_kernel, shift=shift),
        out_shape=jax.ShapeDtypeStruct((depth, tiles, d, tw), bf16),
        grid=(depth, tiles),
        in_specs=[pl.BlockSpec((pl.Element(1), pl.Element(d), pl.Element(window, (0, overhang))),
                               lambda l, k: (l, 0, pl.multiple_of(base + src_tile(k) * tw, LANES)))],
        out_specs=pl.BlockSpec((1, 1, d, tw), lambda l, k: (l, k, 0, 0)),
        compiler_params=_cparams(("parallel", "parallel")),
        name="repack",
    )(w_in)


def kernel(x, c, w_ada, b_ada, norm1_w, norm2_w, w_in, conv_w, conv_b, dt_bias, a_log, d_skip, ssd_norm_w, w_ssd_out, q_norm_w, k_norm_w, w_attn_out, w_pool_mix, pool_scale, w_pool_out, w_out, w_ff1, w_ff2):
    b, s, d = x.shape
    depth = w_in.shape[0]
    ng = len(DILATED_GROUPS)
    mod = _adaln(c, w_ada, b_ada)
    mod = mod.reshape(depth, mod.shape[1], 6, d).transpose(0, 2, 1, 3)
    dt0 = SSD_D_INNER + SSD_CONV_DIM
    q0 = dt0 + SSD_HEADS
    u0 = q0 + QKV_WIDTH
    gcols = ATTN_HEADS_PER_GROUP * ATTN_HEAD_DIM
    w_qkv = _repack(w_in, q0, 3 * ng, gcols, lambda k: (k % 3) * ng + k // 3)
    w_ug = _repack(w_in, u0, UG_TILES, UG_TILE)
    w_z = w_in[:, :, :dt0].astype(bf16)
    w_dt = _pad_lanes(w_in[:, :, dt0:q0]).astype(bf16)
    per_group = jnp.concatenate([jnp.tile(q_norm_w * ATTN_HEAD_DIM ** -0.5, (1, ATTN_HEADS_PER_GROUP)),
                                 jnp.tile(k_norm_w, (1, ATTN_HEADS_PER_GROUP)),
                                 jnp.ones((depth, gcols), f32)], axis=-1)
    head_w = jnp.tile(per_group, (1, ng))[:, None, :]
    cast = lambda w: w.astype(bf16)
    w_ssd_out, w_attn_out, w_pool_mix, w_pool_out, w_out, w_ff1, w_ff2 = map(
        cast, (w_ssd_out, w_attn_out, w_pool_mix, w_pool_out, w_out, w_ff1, w_ff2))
    for l in range(depth):
        h, *qkv_groups = _qkv_proj(l, x, norm1_w[l], mod, w_qkv, head_w[l])
        y_ssd, ug = _ssd_proj(l, h, w_z, w_ug, w_dt, conv_w[l], conv_b[l], dt_bias[l], a_log[l],
                              d_skip[l], ssd_norm_w[l])
        y_attn = _attention(qkv_groups)
        x = _merge(l, x, y_ssd, y_attn, ug, mod, w_ssd_out, w_attn_out, w_pool_mix, pool_scale[l],
                   w_pool_out, w_out)
        x = _ffn(l, x, norm2_w[l], mod, w_ff1, w_ff2)
    return x
```

```python
import functools
import math

import numpy as np
import jax
import jax.numpy as jnp
from jax import lax
from jax.experimental import pallas as pl
from jax.experimental.pallas import tpu as pltpu

f32 = jnp.float32
bf16 = jnp.bfloat16

D_MODEL = 1024
SSD_D_INNER = 1024
SSD_HEAD_DIM = 64
SSD_HEADS = SSD_D_INNER // SSD_HEAD_DIM
SSD_GROUPS = 2
SSD_STATE = 128
SSD_CONV = 4
SSD_CHUNK = 128
SSD_BC = 2 * SSD_GROUPS * SSD_STATE
SSD_CONV_DIM = SSD_D_INNER + SSD_BC
ATTN_HEAD_DIM = 128
ATTN_HEADS_PER_GROUP = 4
DILATED_GROUPS = ((128, 1), (512, 4), (2048, 16))
ATTN_HEADS = ATTN_HEADS_PER_GROUP * len(DILATED_GROUPS)
ATTN_WIDTH = ATTN_HEADS * ATTN_HEAD_DIM
ATTN_OUT_WIDTH = ATTN_HEADS_PER_GROUP * ATTN_HEAD_DIM
POOL_WINDOWS = (2, 4, 8, 16)
POOL_WIDTH = 1024
POOL_GROUP_WIDTH = POOL_WIDTH // len(POOL_WINDOWS)
N_BRANCHES = 3
D_FF = 4 * D_MODEL
EPS = 1e-6
LOG2E = math.log2(math.e)

LANES = 128
SUBLANES = 8
VMEM_LIMIT_BYTES = 56 * 1024 * 1024

ATTN_GROUP_WIDTH = 3 * ATTN_HEADS_PER_GROUP * ATTN_HEAD_DIM
QKV_WIDTH = 3 * ATTN_WIDTH
UG_TILE = 512
UG_TILES = (POOL_WIDTH + N_BRANCHES * D_MODEL) // UG_TILE
TILES_PER_BRANCH = D_MODEL // UG_TILE
UG_PIECES = 2
POOL_HALO = 32
CONV_HALO = 8
ATTN_UNROLL = 16


def _alibi_slopes(n):
    def pow2(k):
        start = 2.0 ** (-8.0 / k)
        return [start ** (i + 1) for i in range(k)]
    if math.log2(n).is_integer():
        s = pow2(n)
    else:
        c = 2 ** math.floor(math.log2(n))
        s = pow2(c) + pow2(2 * c)[0::2][: n - c]
    return np.sort(np.asarray(s, np.float32))[::-1].copy()


def _cparams(sem):
    return pltpu.CompilerParams(dimension_semantics=sem, vmem_limit_bytes=VMEM_LIMIT_BYTES)


def _sigmoid(x):
    return 0.5 * (1.0 + jnp.tanh(0.5 * x))


def _silu(x):
    return x * _sigmoid(x)


def _modulated_rms(x, nw, sc, sh):
    y = x * lax.rsqrt(jnp.mean(x * x, axis=-1, keepdims=True) + EPS) * nw
    return y * (1.0 + sc) + sh


MOD_SHIFT1, MOD_SCALE1, MOD_GATE1, MOD_SHIFT2, MOD_SCALE2, MOD_GATE2 = range(6)


def _mod_spec(mod, l, which):
    return pl.BlockSpec((1, 1) + mod.shape[2:], lambda *_: (l, which, 0, 0))


def _mod_row(ref):
    return ref[0, 0, pl.ds(pl.program_id(0), 1), :]


def _layer_spec(w, l, **kw):
    return pl.BlockSpec((1,) + w.shape[1:], lambda *_: (l,) + (0,) * (w.ndim - 1), **kw)


def _adaln_kernel(c_ref, w_ref, b_ref, o_ref):
    cond = _silu(c_ref[...])
    o_ref[0] = jnp.dot(cond, w_ref[0], preferred_element_type=f32,
                       precision=lax.Precision.HIGHEST) + b_ref[0]


def _adaln(c, w_ada, b_ada):
    depth, d, n = w_ada.shape
    rows = SUBLANES * pl.cdiv(c.shape[0], SUBLANES)
    c_pad = jnp.pad(c, ((0, rows - c.shape[0]), (0, 0)))
    tn = n // 4
    return pl.pallas_call(
        _adaln_kernel,
        out_shape=jax.ShapeDtypeStruct((depth, rows, n), f32),
        grid=(depth, n // tn),
        in_specs=[pl.BlockSpec((rows, d), lambda l, j: (0, 0)),
                  pl.BlockSpec((1, d, tn), lambda l, j: (l, 0, j)),
                  pl.BlockSpec((1, 1, tn), lambda l, j: (l, 0, j))],
        out_specs=pl.BlockSpec((1, rows, tn), lambda l, j: (l, 0, j)),
        compiler_params=_cparams(("parallel", "parallel")),
        name="adaln",
    )(c_pad, w_ada, b_ada.reshape(depth, 1, n))


def _qkv_proj_kernel(x_ref, nw_ref, sc_ref, sh_ref, w_ref, hw_ref, h_ref, *rest):
    ng = len(DILATED_GROUPS)
    outs, (hf, hp) = rest[:ng], rest[ng:]
    e = ATTN_HEAD_DIM
    gw = ATTN_GROUP_WIDTH
    tm, d = x_ref.shape[1], x_ref.shape[2]
    hn = _modulated_rms(x_ref[0], nw_ref[...], _mod_row(sc_ref), _mod_row(sh_ref))
    h_ref[0] = hn.astype(bf16)
    for ct in range(d // LANES):
        hf[ct] = hn[:, ct * LANES:(ct + 1) * LANES]
    lhs = []
    for gi, (_, dil) in enumerate(DILATED_GROUPS):
        if dil == 1:
            lhs.append(h_ref.at[0])
            continue
        n = tm // dil
        slot = len([1 for _, dd in DILATED_GROUPS[:gi] if dd > 1])
        for ct in range(d // LANES):
            for r in range(dil):
                hp[slot, r * n:(r + 1) * n, ct * LANES:(ct + 1) * LANES] = (
                    hf[ct, pl.ds(r, n, stride=dil), :].astype(bf16))
        lhs.append(hp.at[slot])
    tw = w_ref.shape[3]
    for c in range(w_ref.shape[1] * tw // (2 * e)):
        cols = slice(c * 2 * e, (c + 1) * 2 * e)
        gi, lc = divmod(c * 2 * e, gw)
        dil = DILATED_GROUPS[gi][1]
        n = tm // dil
        t, tc = divmod(c * 2 * e, tw)
        y = jnp.dot(lhs[gi][...], w_ref[0, t, :, tc:tc + 2 * e], preferred_element_type=f32)
        if lc < 2 * gw // 3:
            heads = [y[:, half * e:(half + 1) * e] for half in range(2)]
            heads = [t * lax.rsqrt(jnp.mean(t * t, axis=-1, keepdims=True) + EPS) for t in heads]
            y = jnp.concatenate(heads, axis=1) * hw_ref[:, cols]
        y = y.astype(bf16)
        for r in range(dil):
            outs[gi][0, r, :, lc:lc + 2 * e] = y[r * n:(r + 1) * n, :]


def _qkv_proj(l, x, nw, mod, w, head_w, tm=1024):
    b, s, d = x.shape
    n = w.shape[1] * w.shape[3]
    gw = ATTN_GROUP_WIDTH
    dils = [dil for _, dil in DILATED_GROUPS]
    return pl.pallas_call(
        _qkv_proj_kernel,
        out_shape=[jax.ShapeDtypeStruct((b, s, d), bf16)]
                  + [jax.ShapeDtypeStruct((b, dil, s // dil, gw), bf16) for dil in dils],
        grid=(b, s // tm),
        in_specs=[pl.BlockSpec((1, tm, d), lambda i, j: (i, j, 0)),
                  pl.BlockSpec((1, d), lambda i, j: (0, 0)),
                  _mod_spec(mod, l, MOD_SCALE1), _mod_spec(mod, l, MOD_SHIFT1),
                  _layer_spec(w, l, pipeline_mode=pl.Buffered(1)),
                  pl.BlockSpec((1, n), lambda i, j: (0, 0))],
        out_specs=[pl.BlockSpec((1, tm, d), lambda i, j: (i, j, 0))]
                  + [pl.BlockSpec((1, dil, tm // dil, gw), lambda i, j: (i, 0, j, 0)) for dil in dils],
        scratch_shapes=[pltpu.VMEM((d // LANES, tm, LANES), f32),
                        pltpu.VMEM((len([1 for dil in dils if dil > 1]), tm, d), bf16)],
        compiler_params=_cparams(("parallel", "parallel")),
        name="qkv_proj",
    )(x, nw.reshape(1, d), mod, mod, w, head_w)


def _expand_heads(v):
    lane = lax.broadcasted_iota(jnp.int32, (v.shape[0], LANES), 1)
    tiles = [jnp.where(lane < SSD_HEAD_DIM, v[:, 2 * p:2 * p + 1], v[:, 2 * p + 1:2 * p + 2])
             for p in range(SSD_HEADS // 2)]
    return jnp.concatenate(tiles, axis=1)


def _ssd_chunk(h_c, z_c, xbc_c, wdt_ref, cw_ref, cb_ref, dtb_ref, alog_ref, dsk_ref, nw_ref, ext, conv, state,
               side=()):
    side = list(side)
    run_side = lambda: (side.pop(0) or (lambda: None))() if side else None
    L = SSD_CHUNK
    N = SSD_STATE
    DI = SSD_D_INNER
    GW = DI // SSD_GROUPS

    R = L // SUBLANES
    base = CONV_HALO - (SSD_CONV - 1)
    xbc_raw = xbc_c.astype(f32)
    run_side()
    for c in range(SSD_CONV_DIM // LANES):
        cols = slice(c * LANES, (c + 1) * LANES)
        ext[c, CONV_HALO:CONV_HALO + L, :] = xbc_raw[:, cols]
        taps = [jnp.broadcast_to(cw_ref[k:k + 1, cols], (SUBLANES, LANES)) for k in range(SSD_CONV)]
        bias = jnp.broadcast_to(cb_ref[:, cols], (SUBLANES, LANES))
        rows = [ext[c, pl.ds(base + j, SUBLANES, stride=R), :] for j in range(R + SSD_CONV - 1)]
        for v in range(R):
            acc = bias
            for k in range(SSD_CONV):
                acc = acc + rows[v + k] * taps[k]
            conv[c, pl.ds(v, SUBLANES, stride=R), :] = _silu(acc)
        ext[c, 0:CONV_HALO, :] = ext[c, L:L + CONV_HALO, :]
    xs = jnp.concatenate([conv[c] for c in range(DI // LANES)], axis=1)
    bc = jnp.concatenate([conv[c] for c in range(DI // LANES, SSD_CONV_DIM // LANES)], axis=1)

    dtr = jnp.dot(h_c, wdt_ref[0], preferred_element_type=f32) + dtb_ref[...]
    dt = jnp.maximum(dtr, 0.0) + jnp.log1p(jnp.exp(-jnp.abs(dtr)))
    a2 = dt * (-LOG2E * jnp.exp(alog_ref[...]))
    ti = lax.broadcasted_iota(jnp.int32, (L, L), 0)
    si = lax.broadcasted_iota(jnp.int32, (L, L), 1)
    causal = ti >= si
    a_cum = jnp.dot(causal.astype(f32), a2, preferred_element_type=f32,
                    precision=lax.Precision.HIGHEST)
    a_cum_t = a_cum.T

    dt_e = _expand_heads(dt)
    acum_e = _expand_heads(a_cum)
    alast_e = acum_e[L - 1:L, :]
    x_dt = xs * dt_e
    x_dec = (x_dt * jnp.exp2(alast_e - acum_e)).astype(bf16)
    x_dt_b = x_dt.astype(bf16)
    e_acum = jnp.exp2(acum_e)
    e_alast = jnp.exp2(alast_e)
    lane = lax.broadcasted_iota(jnp.int32, (L, LANES), 1)
    lo = lane < SSD_HEAD_DIM
    zero_b = jnp.zeros((L, LANES), bf16)

    ys = []
    for g in range(SSD_GROUPS):
        run_side()
        b_g = bc[:, g * N:(g + 1) * N]
        c_g = bc[:, (SSD_GROUPS + g) * N:(SSD_GROUPS + g + 1) * N].astype(bf16)
        cb = lax.dot_general(c_g, b_g.astype(bf16), (((1,), (1,)), ((), ())),
                             preferred_element_type=f32)
        st = state[g]
        y_off = jnp.dot(c_g, st.astype(bf16), preferred_element_type=f32) * e_acum[:, g * GW:(g + 1) * GW]
        new = jnp.dot(b_g.T.astype(bf16), x_dec[:, g * GW:(g + 1) * GW], preferred_element_type=f32)
        state[g] = e_alast[:, g * GW:(g + 1) * GW] * st + new
        tiles = []
        for p in range(GW // LANES):
            col0 = g * GW + p * LANES
            xp = x_dt_b[:, col0:col0 + LANES]
            ms, xh = [], []
            for half in range(2):
                h = col0 // SSD_HEAD_DIM + half
                seg = a_cum[:, h:h + 1] - a_cum_t[h:h + 1, :]
                ms.append((cb * jnp.exp2(jnp.where(causal, seg, -jnp.inf))).astype(bf16))
                xh.append(jnp.where(lo if half == 0 else jnp.logical_not(lo), xp, zero_b))
            tiles.append(jnp.dot(jnp.concatenate(ms, axis=1), jnp.concatenate(xh, axis=0),
                                 preferred_element_type=f32))
        ys.append(jnp.concatenate(tiles, axis=1) + y_off)
    run_side()
    y = jnp.concatenate(ys, axis=1) + dsk_ref[...] * xs
    y = y * _silu(z_c.astype(f32))
    outs = []
    for g in range(SSD_GROUPS):
        yg = y[:, g * GW:(g + 1) * GW]
        outs.append(yg * lax.rsqrt(jnp.mean(yg * yg, axis=-1, keepdims=True) + EPS))
    return (jnp.concatenate(outs, axis=1) * nw_ref[...]).astype(bf16)


def _ssd_proj_kernel(h_ref, wz_ref, wug_ref, wdt_ref, cw_ref, cb_ref, dtb_ref, alog_ref, dsk_ref, nw_ref,
                     y_ref, ug_ref, zx, ext, conv, state):
    L = SSD_CHUNK
    DI = SSD_D_INNER

    @pl.when(pl.program_id(1) == 0)
    def _():
        ext[:, 0:CONV_HALO, :] = jnp.zeros((SSD_CONV_DIM // LANES, CONV_HALO, LANES), f32)
        state[...] = jnp.zeros_like(state)

    zx[...] = jnp.dot(h_ref[0], wz_ref[0], preferred_element_type=f32).astype(zx.dtype)

    def chunk(c, carry):
        rows = pl.ds(pl.multiple_of(c * L, L), L)
        def piece(n0, n1):
            def run():
                ug_ref[0, c, :, n0:n1] = jnp.dot(h_ref[0], wug_ref[0, c, :, n0:n1],
                                                 preferred_element_type=f32).astype(ug_ref.dtype)
            return run

        tw = ug_ref.shape[3]
        pieces = [piece(p * tw // UG_PIECES, (p + 1) * tw // UG_PIECES) for p in range(UG_PIECES)]
        side = [pieces[0], pieces[1]]
        y_ref[0, rows, :] = _ssd_chunk(h_ref[0, rows, :], zx[rows, 0:DI], zx[rows, DI:], wdt_ref, cw_ref,
                                       cb_ref, dtb_ref, alog_ref, dsk_ref, nw_ref, ext, conv, state, side)
        return carry

    lax.fori_loop(0, ug_ref.shape[1], chunk, 0)


def _pad_lanes(v):
    return jnp.pad(v, [(0, 0)] * (v.ndim - 1) + [(0, LANES - v.shape[-1])])


def _ssd_proj(l, h, w_z, w_ug, w_dt, conv_w, conv_b, dt_bias, a_log, d_skip, ssd_norm_w):
    b, s, d = h.shape
    L = SSD_CHUNK
    di = SSD_D_INNER
    _, tiles, _, tw = w_ug.shape
    tm = tiles * L
    row = lambda v: v.reshape(1, -1)
    once = dict(pipeline_mode=pl.Buffered(1))
    full = lambda shape: pl.BlockSpec(shape, lambda i, j: (0,) * len(shape))
    return pl.pallas_call(
        _ssd_proj_kernel,
        out_shape=(jax.ShapeDtypeStruct((b, s, di), bf16), jax.ShapeDtypeStruct((b, tiles, s, tw), bf16)),
        grid=(b, s // tm),
        in_specs=[pl.BlockSpec((1, tm, d), lambda i, j: (i, j, 0)),
                  _layer_spec(w_z, l, **once), _layer_spec(w_ug, l, **once), _layer_spec(w_dt, l),
                  full((SSD_CONV, SSD_CONV_DIM)), full((1, SSD_CONV_DIM)),
                  full((1, LANES)), full((1, LANES)), full((1, di)), full((1, di))],
        out_specs=(pl.BlockSpec((1, tm, di), lambda i, j: (i, j, 0)),
                   pl.BlockSpec((1, tiles, tm, tw), lambda i, j: (i, 0, j, 0))),
        scratch_shapes=[pltpu.VMEM((tm, di + SSD_CONV_DIM), bf16),
                        pltpu.VMEM((SSD_CONV_DIM // LANES, L + CONV_HALO, LANES), f32),
                        pltpu.VMEM((SSD_CONV_DIM // LANES, L, LANES), f32),
                        pltpu.VMEM((SSD_GROUPS, SSD_STATE, di // SSD_GROUPS), f32)],
        compiler_params=_cparams(("parallel", "arbitrary")),
        name="ssd_proj",
    )(h, w_z, w_ug, w_dt, conv_w, row(conv_b), row(_pad_lanes(dt_bias)), row(_pad_lanes(a_log)),
      row(jnp.repeat(d_skip, SSD_HEAD_DIM)), row(ssd_norm_w))


def _attn_group(gi, dil, slope, q, k, v, og, lse):
    seq = q.shape[0] * q.shape[1]
    steps = DILATED_GROUPS[gi][0] // dil
    e = ATTN_HEAD_DIM
    nb = seq // (dil * steps)
    qi = lax.broadcasted_iota(jnp.int32, (steps, 2 * steps), 0)
    kj = lax.broadcasted_iota(jnp.int32, (steps, 2 * steps), 1)
    neg = jnp.float32(-jnp.inf)
    rel = qi + steps - kj
    bias = jnp.where((rel >= 0) & (rel <= steps), (-slope * dil) * rel.astype(f32), neg)
    bias_first = jnp.where(kj >= steps, bias, neg)
    ones = jnp.ones((2 * steps, e), bf16)
    nt = (((1,), (1,)), ((), ()))

    def block(it):
        r = it // nb
        i = it % nb
        rows = pl.ds(pl.multiple_of(it * steps, steps), steps)
        band = pl.ds(pl.multiple_of(it * steps, steps), 2 * steps)
        vb = jnp.concatenate([v[band, :], ones], axis=1)
        qb = q[r, pl.ds(pl.multiple_of(i * steps, steps), steps), :]
        s = lax.dot_general(qb, k[band, :], nt, preferred_element_type=f32)
        s = s + jnp.where(i > 0, bias, bias_first)
        m = jnp.max(s, axis=-1, keepdims=True)
        p = jnp.exp(s - m).astype(bf16)
        oe = jnp.dot(p, vb, preferred_element_type=f32)
        l = oe[:, e:]
        out_rows = rows if dil == 1 else pl.ds(r + i * (steps * dil), steps, stride=dil)
        og[gi, out_rows, :] = oe[:, :e] / l
        lse[gi, out_rows, :] = m + jnp.log(l)

    def body(it, carry):
        for u in range(ATTN_UNROLL):
            block(it * ATTN_UNROLL + u)
        return carry

    lax.fori_loop(0, dil * nb // ATTN_UNROLL, body, 0)


def _attn_kernel(slopes_ref, *refs):
    ng = len(DILATED_GROUPS)
    ins, o_ref = refs[:3 * ng], refs[3 * ng]
    stage, og, lse = refs[3 * ng + 1:]
    j = pl.program_id(1)
    seq = o_ref.shape[1]
    rt = 512

    pad = stage.shape[1] - seq
    stage[:, 0:pad, :] = jnp.zeros((2, pad, stage.shape[2]), stage.dtype)
    for gi, (_, dil) in enumerate(DILATED_GROUPS):
        q, k, v = ins[3 * gi:3 * gi + 3]
        n = seq // dil
        for a, part in enumerate((k, v)):
            for r in range(dil):
                stage[a, pad + r * n:pad + (r + 1) * n, :] = part[0, r]
        _attn_group(gi, dil, slopes_ref[gi, j], q.at[0], stage.at[0], stage.at[1], og, lse)

    def merge(t, carry):
        rows = pl.ds(pl.multiple_of(t * rt, rt), rt)
        ls = [lse[gi, rows, :] for gi in range(ng)]
        top = functools.reduce(jnp.maximum, ls)
        ws = [jnp.exp(x - top) for x in ls]
        num = functools.reduce(jnp.add, [w * og[gi, rows, :] for gi, w in enumerate(ws)])
        o_ref[0, rows, :] = (num / functools.reduce(jnp.add, ws)).astype(o_ref.dtype)
        return carry

    lax.fori_loop(0, seq // rt, merge, 0)


def _attention(groups):
    e = ATTN_HEAD_DIM
    hg = ATTN_HEADS_PER_GROUP
    ng = len(DILATED_GROUPS)
    b = groups[0].shape[0]
    s = groups[0].shape[1] * groups[0].shape[2]
    slopes = jnp.asarray(_alibi_slopes(ATTN_HEADS)).reshape(ng, hg)
    args, specs = [], []
    for arr, (win, dil) in zip(groups, DILATED_GROUPS):
        assert s % win == 0 and win // dil == DILATED_GROUPS[0][0]
        assert (s // (win // dil)) % ATTN_UNROLL == 0
        for a in range(3):
            args.append(arr)
            specs.append(pl.BlockSpec((1, dil, s // dil, e), lambda i, j, a=a: (i, 0, 0, a * hg + j)))
    return pl.pallas_call(
        _attn_kernel,
        out_shape=jax.ShapeDtypeStruct((b, s, ATTN_OUT_WIDTH), bf16),
        grid=(b, hg),
        in_specs=[pl.BlockSpec(memory_space=pltpu.SMEM)] + specs,
        out_specs=pl.BlockSpec((1, s, e), lambda i, j: (i, 0, j)),
        scratch_shapes=[pltpu.VMEM((2, s + DILATED_GROUPS[0][0], e), bf16)] + [pltpu.VMEM((ng, s, e), f32)] * 2,
        compiler_params=_cparams(("parallel", "parallel")),
        name="dilated_attn",
    )(slopes, *args)


def _join_tiles(ref):
    return jnp.concatenate([ref[0, t] for t in range(ref.shape[1])], axis=1).astype(f32)


def _pooled(u_ref, halo_ref, wmix_ref, scale_ref, ext, buf):
    ts = u_ref.shape[2]
    gw = POOL_GROUP_WIDTH
    n = POOL_HALO + ts
    i = pl.program_id(1)
    halo = _join_tiles(halo_ref)
    ext[0:POOL_HALO, :] = jnp.where(i > 0, halo, jnp.zeros_like(halo))
    ext[POOL_HALO:n, :] = _join_tiles(u_ref)
    src, w, lo = ext, 1, 0
    for gi, win in enumerate(POOL_WINDOWS):
        assert win == 2 * w
        lo = SUBLANES * pl.cdiv(lo + w, SUBLANES)
        dst = buf.at[gi % 2]
        dst[lo:n, gi * gw:] = src[lo:n, gi * gw:] + src[pl.ds(lo - w, n - lo), gi * gw:]
        src, w = dst, win
    assert lo <= POOL_HALO
    pos = i * ts + lax.broadcasted_iota(jnp.int32, (ts, 1), 0)
    outs = []
    for gi, win in enumerate(POOL_WINDOWS):
        cols = slice(gi * gw, (gi + 1) * gw)
        count = jnp.minimum(pos + 1, win).astype(f32)
        pooled = buf[gi % 2, POOL_HALO:n, cols] / count - ext[POOL_HALO:n, cols]
        outs.append(jnp.dot(pooled.astype(bf16), wmix_ref[gi], preferred_element_type=f32))
    return (jnp.concatenate(outs, axis=1) * scale_ref[...]).astype(bf16)


def _merge_kernel(x_ref, ys_ref, ya_ref, u_ref, halo_ref, gs_ref, ga_ref, gp_ref, g1_ref,
                  ws_ref, wa_ref, wmix_ref, pscale_ref, wp_ref, wo_ref, o_ref, ext, buf):
    dot = functools.partial(jnp.dot, preferred_element_type=f32)
    gate = lambda ref: _sigmoid(_join_tiles(ref))
    y_pool = _pooled(u_ref, halo_ref, wmix_ref.at[0], pscale_ref, ext, buf)
    m = (gate(gs_ref) * dot(ys_ref[0], ws_ref[0])
         + gate(ga_ref) * dot(ya_ref[0], wa_ref[0])
         + gate(gp_ref) * dot(y_pool, wp_ref[0]))
    o_ref[0] = x_ref[0] + _mod_row(g1_ref) * dot(m.astype(bf16), wo_ref[0])


def _merge(l, x, y_ssd, y_attn, ug, mod, w_ssd_out, w_attn_out, w_pool_mix, pool_scale, w_pool_out, w_out,
           ts=512):
    b, s, d = x.shape
    tpb = TILES_PER_BRANCH
    assert POOL_WIDTH == tpb * UG_TILE and ts % POOL_HALO == 0
    tile = lambda w: pl.BlockSpec((1, ts, w), lambda i, j: (i, j, 0))
    full = lambda w: _layer_spec(w, l)
    branch = lambda t: pl.BlockSpec((1, tpb, ts, UG_TILE), lambda i, j: (i, t, j, 0))
    per = ts // POOL_HALO
    halo = pl.BlockSpec((1, tpb, POOL_HALO, UG_TILE), lambda i, j: (i, 0, jnp.maximum(j * per - 1, 0), 0))
    return pl.pallas_call(
        _merge_kernel,
        out_shape=jax.ShapeDtypeStruct((b, s, d), f32),
        grid=(b, s // ts),
        in_specs=[tile(d), tile(y_ssd.shape[-1]), tile(y_attn.shape[-1]),
                  branch(0), halo, branch(1), branch(2), branch(3),
                  _mod_spec(mod, l, MOD_GATE1),
                  full(w_ssd_out), full(w_attn_out), full(w_pool_mix),
                  pl.BlockSpec((1, POOL_WIDTH), lambda i, j: (0, 0)), full(w_pool_out), full(w_out)],
        out_specs=tile(d),
        scratch_shapes=[pltpu.VMEM((ts + POOL_HALO, POOL_WIDTH), f32),
                        pltpu.VMEM((2, ts + POOL_HALO, POOL_WIDTH), f32)],
        compiler_params=_cparams(("parallel", "parallel")),
        name="merge",
    )(x, y_ssd, y_attn, ug, ug, ug, ug, ug, mod, w_ssd_out, w_attn_out, w_pool_mix,
      pool_scale.reshape(1, POOL_WIDTH), w_pool_out, w_out)


def _ffn_kernel(x_ref, nw_ref, sc_ref, sh_ref, g2_ref, w1_ref, w2_ref, o_ref, hs, acc):
    k = pl.program_id(2)

    @pl.when(k == 0)
    def _():
        hs[...] = _modulated_rms(x_ref[0], nw_ref[...], _mod_row(sc_ref), _mod_row(sh_ref)).astype(bf16)
        acc[...] = jnp.zeros_like(acc)

    a = jnp.maximum(jnp.dot(hs[...], w1_ref[0], preferred_element_type=f32), 0.0)
    acc[...] += jnp.dot((a * a).astype(bf16), w2_ref[0], preferred_element_type=f32)

    @pl.when(k == pl.num_programs(2) - 1)
    def _():
        o_ref[0] = x_ref[0] + _mod_row(g2_ref) * acc[...]


def _ffn(l, x, nw, mod, w1, w2, ts=1024, tf=2048):
    b, s, d = x.shape
    ff = w1.shape[2]
    return pl.pallas_call(
        _ffn_kernel,
        out_shape=jax.ShapeDtypeStruct((b, s, d), f32),
        grid=(b, s // ts, ff // tf),
        in_specs=[pl.BlockSpec((1, ts, d), lambda i, j, k: (i, j, 0)),
                  pl.BlockSpec((1, d), lambda i, j, k: (0, 0)),
                  _mod_spec(mod, l, MOD_SCALE2), _mod_spec(mod, l, MOD_SHIFT2), _mod_spec(mod, l, MOD_GATE2),
                  pl.BlockSpec((1, d, tf), lambda i, j, k: (l, 0, k)),
                  pl.BlockSpec((1, tf, d), lambda i, j, k: (l, k, 0))],
        out_specs=pl.BlockSpec((1, ts, d), lambda i, j, k: (i, j, 0)),
        scratch_shapes=[pltpu.VMEM((ts, d), bf16), pltpu.VMEM((ts, d), f32)],
        compiler_params=_cparams(("parallel", "parallel", "arbitrary")),
        name="ffn",
    )(x, nw.reshape(1, d), mod, mod, mod, w1, w2)


def _cut(w_ref, c0, width):
    shift = c0 % LANES
    if shift == 0:
        return w_ref[0, :, c0:c0 + width]
    end = min(c0 - shift + width + LANES, w_ref.shape[2])
    return w_ref[0, :, c0 - shift:end][:, shift:shift + width]


def _repack_kernel(w_ref, z_ref, dt_ref, qkv_ref, ug_ref, *, dt0, q0, u0):
    rows = w_ref.shape[1]
    ng = len(DILATED_GROUPS)
    z_ref[0] = _cut(w_ref, 0, dt0).astype(bf16)
    lane = lax.broadcasted_iota(jnp.int32, (rows, LANES), 1)
    dt_ref[0] = jnp.where(lane < q0 - dt0, _cut(w_ref, dt0, LANES), 0.0).astype(bf16)
    tw = qkv_ref.shape[3]
    for k in range(qkv_ref.shape[1]):
        qkv_ref[0, k] = _cut(w_ref, q0 + ((k % 3) * ng + k // 3) * tw, tw).astype(bf16)
    tw = ug_ref.shape[3]
    for k in range(ug_ref.shape[1]):
        ug_ref[0, k] = _cut(w_ref, u0 + k * tw, tw).astype(bf16)


def _repack(w_in, rows=128):
    depth, d, n = w_in.shape
    ng = len(DILATED_GROUPS)
    gcols = ATTN_HEADS_PER_GROUP * ATTN_HEAD_DIM
    dt0 = SSD_D_INNER + SSD_CONV_DIM
    q0 = dt0 + SSD_HEADS
    u0 = q0 + QKV_WIDTH
    assert u0 + UG_TILES * UG_TILE == n
    return pl.pallas_call(
        functools.partial(_repack_kernel, dt0=dt0, q0=q0, u0=u0),
        out_shape=(jax.ShapeDtypeStruct((depth, d, dt0), bf16),
                   jax.ShapeDtypeStruct((depth, d, LANES), bf16),
                   jax.ShapeDtypeStruct((depth, 3 * ng, d, gcols), bf16),
                   jax.ShapeDtypeStruct((depth, UG_TILES, d, UG_TILE), bf16)),
        grid=(depth, d // rows),
        in_specs=[pl.BlockSpec((1, rows, n), lambda l, r: (l, r, 0))],
        out_specs=(pl.BlockSpec((1, rows, dt0), lambda l, r: (l, r, 0)),
                   pl.BlockSpec((1, rows, LANES), lambda l, r: (l, r, 0)),
                   pl.BlockSpec((1, 3 * ng, rows, gcols), lambda l, r: (l, 0, r, 0)),
                   pl.BlockSpec((1, UG_TILES, rows, UG_TILE), lambda l, r: (l, 0, r, 0))),
        compiler_params=_cparams(("parallel", "parallel")),
        name="repack",
    )(w_in)


def kernel(x, c, w_ada, b_ada, norm1_w, norm2_w, w_in, conv_w, conv_b, dt_bias, a_log, d_skip, ssd_norm_w, w_ssd_out, q_norm_w, k_norm_w, w_attn_out, w_pool_mix, pool_scale, w_pool_out, w_out, w_ff1, w_ff2):
    b, s, d = x.shape
    depth = w_in.shape[0]
    ng = len(DILATED_GROUPS)
    mod = _adaln(c, w_ada, b_ada)
    mod = mod.reshape(depth, mod.shape[1], 6, d).transpose(0, 2, 1, 3)
    gcols = ATTN_HEADS_PER_GROUP * ATTN_HEAD_DIM
    w_z, w_dt, w_qkv, w_ug = _repack(w_in)
    per_group = jnp.concatenate([jnp.tile(q_norm_w * ATTN_HEAD_DIM ** -0.5, (1, ATTN_HEADS_PER_GROUP)),
                                 jnp.tile(k_norm_w, (1, ATTN_HEADS_PER_GROUP)),
                                 jnp.ones((depth, gcols), f32)], axis=-1)
    head_w = jnp.tile(per_group, (1, ng))[:, None, :]
    cast = lambda w: w.astype(bf16)
    w_ssd_out, w_attn_out, w_pool_mix, w_pool_out, w_out, w_ff1, w_ff2 = map(
        cast, (w_ssd_out, w_attn_out, w_pool_mix, w_pool_out, w_out, w_ff1, w_ff2))
    for l in range(depth):
        h, *qkv_groups = _qkv_proj(l, x, norm1_w[l], mod, w_qkv, head_w[l])
        y_ssd, ug = _ssd_proj(l, h, w_z, w_ug, w_dt, conv_w[l], conv_b[l], dt_bias[l], a_log[l],
                              d_skip[l], ssd_norm_w[l])
        y_attn = _attention(qkv_groups)
        x = _merge(l, x, y_ssd, y_attn, ug, mod, w_ssd_out, w_attn_out, w_pool_mix, pool_scale[l],
                   w_pool_out, w_out)
        x = _ffn(l, x, norm2_w[l], mod, w_ff1, w_ff2)
    return x
```

```python
import functools
import math

import numpy as np
import jax
import jax.numpy as jnp
from jax import lax
from jax.experimental import pallas as pl
from jax.experimental.pallas import tpu as pltpu

f32 = jnp.float32
bf16 = jnp.bfloat16

D_MODEL = 1024
SSD_D_INNER = 1024
SSD_HEAD_DIM = 64
SSD_HEADS = SSD_D_INNER // SSD_HEAD_DIM
SSD_GROUPS = 2
SSD_STATE = 128
SSD_CONV = 4
SSD_CHUNK = 128
SSD_BC = 2 * SSD_GROUPS * SSD_STATE
SSD_CONV_DIM = SSD_D_INNER + SSD_BC
ATTN_HEAD_DIM = 128
ATTN_HEADS_PER_GROUP = 4
DILATED_GROUPS = ((128, 1), (512, 4), (2048, 16))
ATTN_HEADS = ATTN_HEADS_PER_GROUP * len(DILATED_GROUPS)
ATTN_WIDTH = ATTN_HEADS * ATTN_HEAD_DIM
ATTN_OUT_WIDTH = ATTN_HEADS_PER_GROUP * ATTN_HEAD_DIM
POOL_WINDOWS = (2, 4, 8, 16)
POOL_WIDTH = 1024
POOL_GROUP_WIDTH = POOL_WIDTH // len(POOL_WINDOWS)
N_BRANCHES = 3
D_FF = 4 * D_MODEL
EPS = 1e-6
LOG2E = math.log2(math.e)

LANES = 128
SUBLANES = 8
VMEM_LIMIT_BYTES = 56 * 1024 * 1024

ATTN_GROUP_WIDTH = 3 * ATTN_HEADS_PER_GROUP * ATTN_HEAD_DIM
QKV_WIDTH = 3 * ATTN_WIDTH
UG_TILE = 512
UG_TILES = (POOL_WIDTH + N_BRANCHES * D_MODEL) // UG_TILE
TILES_PER_BRANCH = D_MODEL // UG_TILE
UG_PIECES = 2
POOL_HALO = 32
CONV_HALO = 8
ATTN_UNROLL = 16


def _alibi_slopes(n):
    def pow2(k):
        start = 2.0 ** (-8.0 / k)
        return [start ** (i + 1) for i in range(k)]
    if math.log2(n).is_integer():
        s = pow2(n)
    else:
        c = 2 ** math.floor(math.log2(n))
        s = pow2(c) + pow2(2 * c)[0::2][: n - c]
    return np.sort(np.asarray(s, np.float32))[::-1].copy()


def _cparams(sem):
    return pltpu.CompilerParams(dimension_semantics=sem, vmem_limit_bytes=VMEM_LIMIT_BYTES)


def _sigmoid(x):
    return 0.5 * (1.0 + jnp.tanh(0.5 * x))


def _silu(x):
    return x * _sigmoid(x)


def _dot_t(a, w_t):
    return lax.dot_general(a, w_t, (((1,), (1,)), ((), ())), preferred_element_type=f32)


def _modulated_rms(x, nw, sc, sh):
    y = x * lax.rsqrt(jnp.mean(x * x, axis=-1, keepdims=True) + EPS) * nw
    return y * (1.0 + sc) + sh


MOD_SHIFT1, MOD_SCALE1, MOD_GATE1, MOD_SHIFT2, MOD_SCALE2, MOD_GATE2 = range(6)


def _mod_spec(mod, l, which):
    return pl.BlockSpec((1, 1) + mod.shape[2:], lambda *_: (l, which, 0, 0))


def _mod_row(ref):
    return ref[0, 0, pl.ds(pl.program_id(0), 1), :]


def _layer_spec(w, l, **kw):
    return pl.BlockSpec((1,) + w.shape[1:], lambda *_: (l,) + (0,) * (w.ndim - 1), **kw)


def _adaln_kernel(c_ref, w_ref, b_ref, o_ref):
    cond = _silu(c_ref[...])
    o_ref[0] = jnp.dot(cond, w_ref[0], preferred_element_type=f32,
                       precision=lax.Precision.HIGHEST) + b_ref[0]


def _adaln(c, w_ada, b_ada):
    depth, d, n = w_ada.shape
    rows = SUBLANES * pl.cdiv(c.shape[0], SUBLANES)
    c_pad = jnp.pad(c, ((0, rows - c.shape[0]), (0, 0)))
    tn = n // 4
    return pl.pallas_call(
        _adaln_kernel,
        out_shape=jax.ShapeDtypeStruct((depth, rows, n), f32),
        grid=(depth, n // tn),
        in_specs=[pl.BlockSpec((rows, d), lambda l, j: (0, 0)),
                  pl.BlockSpec((1, d, tn), lambda l, j: (l, 0, j)),
                  pl.BlockSpec((1, 1, tn), lambda l, j: (l, 0, j))],
        out_specs=pl.BlockSpec((1, rows, tn), lambda l, j: (l, 0, j)),
        compiler_params=_cparams(("parallel", "parallel")),
        name="adaln",
    )(c_pad, w_ada, b_ada.reshape(depth, 1, n))


def _qkv_proj_kernel(x_ref, nw_ref, sc_ref, sh_ref, w_ref, hw_ref, h_ref, *rest):
    ng = len(DILATED_GROUPS)
    outs, (hf, hp) = rest[:ng], rest[ng:]
    e = ATTN_HEAD_DIM
    gw = ATTN_GROUP_WIDTH
    tm, d = x_ref.shape[1], x_ref.shape[2]
    hn = _modulated_rms(x_ref[0], nw_ref[...], _mod_row(sc_ref), _mod_row(sh_ref))
    h_ref[0] = hn.astype(bf16)
    for ct in range(d // LANES):
        hf[ct] = hn[:, ct * LANES:(ct + 1) * LANES]
    lhs = []
    for gi, (_, dil) in enumerate(DILATED_GROUPS):
        if dil == 1:
            lhs.append(h_ref.at[0])
            continue
        n = tm // dil
        slot = len([1 for _, dd in DILATED_GROUPS[:gi] if dd > 1])
        for ct in range(d // LANES):
            for r in range(dil):
                hp[slot, r * n:(r + 1) * n, ct * LANES:(ct + 1) * LANES] = (
                    hf[ct, pl.ds(r, n, stride=dil), :].astype(bf16))
        lhs.append(hp.at[slot])
    tw = w_ref.shape[2]
    for c in range(w_ref.shape[1] * tw // (2 * e)):
        cols = slice(c * 2 * e, (c + 1) * 2 * e)
        gi, lc = divmod(c * 2 * e, gw)
        dil = DILATED_GROUPS[gi][1]
        n = tm // dil
        t, tc = divmod(c * 2 * e, tw)
        y = _dot_t(lhs[gi][...], w_ref[0, t, tc:tc + 2 * e, :])
        if lc < 2 * gw // 3:
            heads = [y[:, half * e:(half + 1) * e] for half in range(2)]
            heads = [t * lax.rsqrt(jnp.mean(t * t, axis=-1, keepdims=True) + EPS) for t in heads]
            y = jnp.concatenate(heads, axis=1) * hw_ref[:, cols]
        y = y.astype(bf16)
        for r in range(dil):
            outs[gi][0, r, :, lc:lc + 2 * e] = y[r * n:(r + 1) * n, :]


def _qkv_proj(l, x, nw, mod, w, head_w, tm=1024):
    b, s, d = x.shape
    n = w.shape[1] * w.shape[2]
    gw = ATTN_GROUP_WIDTH
    dils = [dil for _, dil in DILATED_GROUPS]
    return pl.pallas_call(
        _qkv_proj_kernel,
        out_shape=[jax.ShapeDtypeStruct((b, s, d), bf16)]
                  + [jax.ShapeDtypeStruct((b, dil, s // dil, gw), bf16) for dil in dils],
        grid=(b, s // tm),
        in_specs=[pl.BlockSpec((1, tm, d), lambda i, j: (i, j, 0)),
                  pl.BlockSpec((1, d), lambda i, j: (0, 0)),
                  _mod_spec(mod, l, MOD_SCALE1), _mod_spec(mod, l, MOD_SHIFT1),
                  _layer_spec(w, l, pipeline_mode=pl.Buffered(1)),
                  pl.BlockSpec((1, n), lambda i, j: (0, 0))],
        out_specs=[pl.BlockSpec((1, tm, d), lambda i, j: (i, j, 0))]
                  + [pl.BlockSpec((1, dil, tm // dil, gw), lambda i, j: (i, 0, j, 0)) for dil in dils],
        scratch_shapes=[pltpu.VMEM((d // LANES, tm, LANES), f32),
                        pltpu.VMEM((len([1 for dil in dils if dil > 1]), tm, d), bf16)],
        compiler_params=_cparams(("parallel", "parallel")),
        name="qkv_proj",
    )(x, nw.reshape(1, d), mod, mod, w, head_w)


def _expand_heads(v):
    lane = lax.broadcasted_iota(jnp.int32, (v.shape[0], LANES), 1)
    tiles = [jnp.where(lane < SSD_HEAD_DIM, v[:, 2 * p:2 * p + 1], v[:, 2 * p + 1:2 * p + 2])
             for p in range(SSD_HEADS // 2)]
    return jnp.concatenate(tiles, axis=1)


def _ssd_chunk(h_c, z_c, xbc_c, wdt_ref, cw_ref, cb_ref, dtb_ref, alog_ref, dsk_ref, nw_ref, ext, conv, state,
               side=()):
    side = list(side)
    run_side = lambda: (side.pop(0) or (lambda: None))() if side else None
    L = SSD_CHUNK
    N = SSD_STATE
    DI = SSD_D_INNER
    GW = DI // SSD_GROUPS

    R = L // SUBLANES
    base = CONV_HALO - (SSD_CONV - 1)
    xbc_raw = xbc_c.astype(f32)
    run_side()
    for c in range(SSD_CONV_DIM // LANES):
        cols = slice(c * LANES, (c + 1) * LANES)
        ext[c, CONV_HALO:CONV_HALO + L, :] = xbc_raw[:, cols]
        taps = [jnp.broadcast_to(cw_ref[k:k + 1, cols], (SUBLANES, LANES)) for k in range(SSD_CONV)]
        bias = jnp.broadcast_to(cb_ref[:, cols], (SUBLANES, LANES))
        rows = [ext[c, pl.ds(base + j, SUBLANES, stride=R), :] for j in range(R + SSD_CONV - 1)]
        for v in range(R):
            acc = bias
            for k in range(SSD_CONV):
                acc = acc + rows[v + k] * taps[k]
            conv[c, pl.ds(v, SUBLANES, stride=R), :] = _silu(acc)
        ext[c, 0:CONV_HALO, :] = ext[c, L:L + CONV_HALO, :]
    xs = jnp.concatenate([conv[c] for c in range(DI // LANES)], axis=1)
    bc = jnp.concatenate([conv[c] for c in range(DI // LANES, SSD_CONV_DIM // LANES)], axis=1)

    dtr = _dot_t(h_c, wdt_ref[0]) + dtb_ref[...]
    dt = jnp.maximum(dtr, 0.0) + jnp.log1p(jnp.exp(-jnp.abs(dtr)))
    a2 = dt * (-LOG2E * jnp.exp(alog_ref[...]))
    ti = lax.broadcasted_iota(jnp.int32, (L, L), 0)
    si = lax.broadcasted_iota(jnp.int32, (L, L), 1)
    causal = ti >= si
    a_cum = jnp.dot(causal.astype(f32), a2, preferred_element_type=f32,
                    precision=lax.Precision.HIGHEST)
    a_cum_t = a_cum.T

    dt_e = _expand_heads(dt)
    acum_e = _expand_heads(a_cum)
    alast_e = acum_e[L - 1:L, :]
    x_dt = xs * dt_e
    x_dec = (x_dt * jnp.exp2(alast_e - acum_e)).astype(bf16)
    x_dt_b = x_dt.astype(bf16)
    e_acum = jnp.exp2(acum_e)
    e_alast = jnp.exp2(alast_e)
    lane = lax.broadcasted_iota(jnp.int32, (L, LANES), 1)
    lo = lane < SSD_HEAD_DIM
    zero_b = jnp.zeros((L, LANES), bf16)

    ys = []
    for g in range(SSD_GROUPS):
        run_side()
        b_g = bc[:, g * N:(g + 1) * N]
        c_g = bc[:, (SSD_GROUPS + g) * N:(SSD_GROUPS + g + 1) * N].astype(bf16)
        cb = lax.dot_general(c_g, b_g.astype(bf16), (((1,), (1,)), ((), ())),
                             preferred_element_type=f32)
        st = state[g]
        y_off = jnp.dot(c_g, st.astype(bf16), preferred_element_type=f32) * e_acum[:, g * GW:(g + 1) * GW]
        new = jnp.dot(b_g.T.astype(bf16), x_dec[:, g * GW:(g + 1) * GW], preferred_element_type=f32)
        state[g] = e_alast[:, g * GW:(g + 1) * GW] * st + new
        tiles = []
        for p in range(GW // LANES):
            col0 = g * GW + p * LANES
            xp = x_dt_b[:, col0:col0 + LANES]
            ms, xh = [], []
            for half in range(2):
                h = col0 // SSD_HEAD_DIM + half
                seg = a_cum[:, h:h + 1] - a_cum_t[h:h + 1, :]
                ms.append((cb * jnp.exp2(jnp.where(causal, seg, -jnp.inf))).astype(bf16))
                xh.append(jnp.where(lo if half == 0 else jnp.logical_not(lo), xp, zero_b))
            tiles.append(jnp.dot(jnp.concatenate(ms, axis=1), jnp.concatenate(xh, axis=0),
                                 preferred_element_type=f32))
        ys.append(jnp.concatenate(tiles, axis=1) + y_off)
    run_side()
    y = jnp.concatenate(ys, axis=1) + dsk_ref[...] * xs
    y = y * _silu(z_c.astype(f32))
    outs = []
    for g in range(SSD_GROUPS):
        yg = y[:, g * GW:(g + 1) * GW]
        outs.append(yg * lax.rsqrt(jnp.mean(yg * yg, axis=-1, keepdims=True) + EPS))
    return (jnp.concatenate(outs, axis=1) * nw_ref[...]).astype(bf16)


def _ssd_proj_kernel(h_ref, wz_ref, wug_ref, wdt_ref, cw_ref, cb_ref, dtb_ref, alog_ref, dsk_ref, nw_ref,
                     y_ref, ug_ref, zx, ext, conv, state):
    L = SSD_CHUNK
    DI = SSD_D_INNER

    @pl.when(pl.program_id(1) == 0)
    def _():
        ext[:, 0:CONV_HALO, :] = jnp.zeros((SSD_CONV_DIM // LANES, CONV_HALO, LANES), f32)
        state[...] = jnp.zeros_like(state)

    zx[...] = _dot_t(h_ref[0], wz_ref[0]).astype(zx.dtype)

    def chunk(c, carry):
        rows = pl.ds(pl.multiple_of(c * L, L), L)
        def piece(n0, n1):
            def run():
                ug_ref[0, c, :, n0:n1] = _dot_t(h_ref[0], wug_ref[0, c, n0:n1, :]).astype(ug_ref.dtype)
            return run

        tw = ug_ref.shape[3]
        pieces = [piece(p * tw // UG_PIECES, (p + 1) * tw // UG_PIECES) for p in range(UG_PIECES)]
        side = [pieces[0], pieces[1]]
        y_ref[0, rows, :] = _ssd_chunk(h_ref[0, rows, :], zx[rows, 0:DI], zx[rows, DI:], wdt_ref, cw_ref,
                                       cb_ref, dtb_ref, alog_ref, dsk_ref, nw_ref, ext, conv, state, side)
        return carry

    lax.fori_loop(0, ug_ref.shape[1], chunk, 0)


def _pad_lanes(v):
    return jnp.pad(v, [(0, 0)] * (v.ndim - 1) + [(0, LANES - v.shape[-1])])


def _ssd_proj(l, h, w_z, w_ug, w_dt, conv_w, conv_b, dt_bias, a_log, d_skip, ssd_norm_w):
    b, s, d = h.shape
    L = SSD_CHUNK
    di = SSD_D_INNER
    _, tiles, tw, _ = w_ug.shape
    tm = tiles * L
    row = lambda v: v.reshape(1, -1)
    once = dict(pipeline_mode=pl.Buffered(1))
    full = lambda shape: pl.BlockSpec(shape, lambda i, j: (0,) * len(shape))
    return pl.pallas_call(
        _ssd_proj_kernel,
        out_shape=(jax.ShapeDtypeStruct((b, s, di), bf16), jax.ShapeDtypeStruct((b, tiles, s, tw), bf16)),
        grid=(b, s // tm),
        in_specs=[pl.BlockSpec((1, tm, d), lambda i, j: (i, j, 0)),
                  _layer_spec(w_z, l, **once), _layer_spec(w_ug, l, **once), _layer_spec(w_dt, l),
                  full((SSD_CONV, SSD_CONV_DIM)), full((1, SSD_CONV_DIM)),
                  full((1, LANES)), full((1, LANES)), full((1, di)), full((1, di))],
        out_specs=(pl.BlockSpec((1, tm, di), lambda i, j: (i, j, 0)),
                   pl.BlockSpec((1, tiles, tm, tw), lambda i, j: (i, 0, j, 0))),
        scratch_shapes=[pltpu.VMEM((tm, di + SSD_CONV_DIM), bf16),
                        pltpu.VMEM((SSD_CONV_DIM // LANES, L + CONV_HALO, LANES), f32),
                        pltpu.VMEM((SSD_CONV_DIM // LANES, L, LANES), f32),
                        pltpu.VMEM((SSD_GROUPS, SSD_STATE, di // SSD_GROUPS), f32)],
        compiler_params=_cparams(("parallel", "arbitrary")),
        name="ssd_proj",
    )(h, w_z, w_ug, w_dt, conv_w, row(conv_b), row(_pad_lanes(dt_bias)), row(_pad_lanes(a_log)),
      row(jnp.repeat(d_skip, SSD_HEAD_DIM)), row(ssd_norm_w))


def _attn_group(gi, dil, slope, q, k, v, og, lse):
    seq = q.shape[0] * q.shape[1]
    steps = DILATED_GROUPS[gi][0] // dil
    e = ATTN_HEAD_DIM
    nb = seq // (dil * steps)
    qi = lax.broadcasted_iota(jnp.int32, (steps, 2 * steps), 0)
    kj = lax.broadcasted_iota(jnp.int32, (steps, 2 * steps), 1)
    neg = jnp.float32(-jnp.inf)
    rel = qi + steps - kj
    bias = jnp.where((rel >= 0) & (rel <= steps), (-slope * dil) * rel.astype(f32), neg)
    bias_first = jnp.where(kj >= steps, bias, neg)
    ones = jnp.ones((2 * steps, e), bf16)
    nt = (((1,), (1,)), ((), ()))

    def block(it):
        r = it // nb
        i = it % nb
        rows = pl.ds(pl.multiple_of(it * steps, steps), steps)
        band = pl.ds(pl.multiple_of(it * steps, steps), 2 * steps)
        vb = jnp.concatenate([v[band, :], ones], axis=1)
        qb = q[r, pl.ds(pl.multiple_of(i * steps, steps), steps), :]
        s = lax.dot_general(qb, k[band, :], nt, preferred_element_type=f32)
        s = s + jnp.where(i > 0, bias, bias_first)
        m = jnp.max(s, axis=-1, keepdims=True)
        p = jnp.exp(s - m).astype(bf16)
        oe = jnp.dot(p, vb, preferred_element_type=f32)
        l = oe[:, e:]
        out_rows = rows if dil == 1 else pl.ds(r + i * (steps * dil), steps, stride=dil)
        og[gi, out_rows, :] = oe[:, :e] / l
        lse[gi, out_rows, :] = m + jnp.log(l)

    def body(it, carry):
        for u in range(ATTN_UNROLL):
            block(it * ATTN_UNROLL + u)
        return carry

    lax.fori_loop(0, dil * nb // ATTN_UNROLL, body, 0)


def _attn_kernel(slopes_ref, *refs):
    ng = len(DILATED_GROUPS)
    ins, o_ref = refs[:3 * ng], refs[3 * ng]
    stage, og, lse = refs[3 * ng + 1:]
    j = pl.program_id(1)
    seq = o_ref.shape[1]
    rt = 512

    pad = stage.shape[1] - seq
    stage[:, 0:pad, :] = jnp.zeros((2, pad, stage.shape[2]), stage.dtype)
    for gi, (_, dil) in enumerate(DILATED_GROUPS):
        q, k, v = ins[3 * gi:3 * gi + 3]
        n = seq // dil
        for a, part in enumerate((k, v)):
            for r in range(dil):
                stage[a, pad + r * n:pad + (r + 1) * n, :] = part[0, r]
        _attn_group(gi, dil, slopes_ref[gi, j], q.at[0], stage.at[0], stage.at[1], og, lse)

    def merge(t, carry):
        rows = pl.ds(pl.multiple_of(t * rt, rt), rt)
        ls = [lse[gi, rows, :] for gi in range(ng)]
        top = functools.reduce(jnp.maximum, ls)
        ws = [jnp.exp(x - top) for x in ls]
        num = functools.reduce(jnp.add, [w * og[gi, rows, :] for gi, w in enumerate(ws)])
        o_ref[0, rows, :] = (num / functools.reduce(jnp.add, ws)).astype(o_ref.dtype)
        return carry

    lax.fori_loop(0, seq // rt, merge, 0)


def _attention(groups):
    e = ATTN_HEAD_DIM
    hg = ATTN_HEADS_PER_GROUP
    ng = len(DILATED_GROUPS)
    b = groups[0].shape[0]
    s = groups[0].shape[1] * groups[0].shape[2]
    slopes = jnp.asarray(_alibi_slopes(ATTN_HEADS)).reshape(ng, hg)
    args, specs = [], []
    for arr, (win, dil) in zip(groups, DILATED_GROUPS):
        assert s % win == 0 and win // dil == DILATED_GROUPS[0][0]
        assert (s // (win // dil)) % ATTN_UNROLL == 0
        for a in range(3):
            args.append(arr)
            specs.append(pl.BlockSpec((1, dil, s // dil, e), lambda i, j, a=a: (i, 0, 0, a * hg + j)))
    return pl.pallas_call(
        _attn_kernel,
        out_shape=jax.ShapeDtypeStruct((b, s, ATTN_OUT_WIDTH), bf16),
        grid=(b, hg),
        in_specs=[pl.BlockSpec(memory_space=pltpu.SMEM)] + specs,
        out_specs=pl.BlockSpec((1, s, e), lambda i, j: (i, 0, j)),
        scratch_shapes=[pltpu.VMEM((2, s + DILATED_GROUPS[0][0], e), bf16)] + [pltpu.VMEM((ng, s, e), f32)] * 2,
        compiler_params=_cparams(("parallel", "parallel")),
        name="dilated_attn",
    )(slopes, *args)


def _join_tiles(ref):
    return jnp.concatenate([ref[0, t] for t in range(ref.shape[1])], axis=1).astype(f32)


def _pooled(u_ref, halo_ref, wmix_ref, scale_ref, ext, buf):
    ts = u_ref.shape[2]
    gw = POOL_GROUP_WIDTH
    n = POOL_HALO + ts
    i = pl.program_id(1)
    halo = _join_tiles(halo_ref)
    ext[0:POOL_HALO, :] = jnp.where(i > 0, halo, jnp.zeros_like(halo))
    ext[POOL_HALO:n, :] = _join_tiles(u_ref)
    src, w, lo = ext, 1, 0
    for gi, win in enumerate(POOL_WINDOWS):
        assert win == 2 * w
        lo = SUBLANES * pl.cdiv(lo + w, SUBLANES)
        dst = buf.at[gi % 2]
        dst[lo:n, gi * gw:] = src[lo:n, gi * gw:] + src[pl.ds(lo - w, n - lo), gi * gw:]
        src, w = dst, win
    assert lo <= POOL_HALO
    pos = i * ts + lax.broadcasted_iota(jnp.int32, (ts, 1), 0)
    outs = []
    for gi, win in enumerate(POOL_WINDOWS):
        cols = slice(gi * gw, (gi + 1) * gw)
        count = jnp.minimum(pos + 1, win).astype(f32)
        pooled = buf[gi % 2, POOL_HALO:n, cols] / count - ext[POOL_HALO:n, cols]
        outs.append(jnp.dot(pooled.astype(bf16), wmix_ref[gi], preferred_element_type=f32))
    return (jnp.concatenate(outs, axis=1) * scale_ref[...]).astype(bf16)


def _merge_kernel(x_ref, ys_ref, ya_ref, u_ref, halo_ref, gs_ref, ga_ref, gp_ref, g1_ref,
                  ws_ref, wa_ref, wmix_ref, pscale_ref, wp_ref, wo_ref, o_ref, ext, buf):
    dot = functools.partial(jnp.dot, preferred_element_type=f32)
    gate = lambda ref: _sigmoid(_join_tiles(ref))
    y_pool = _pooled(u_ref, halo_ref, wmix_ref.at[0], pscale_ref, ext, buf)
    m = (gate(gs_ref) * dot(ys_ref[0], ws_ref[0])
         + gate(ga_ref) * dot(ya_ref[0], wa_ref[0])
         + gate(gp_ref) * dot(y_pool, wp_ref[0]))
    o_ref[0] = x_ref[0] + _mod_row(g1_ref) * dot(m.astype(bf16), wo_ref[0])


def _merge(l, x, y_ssd, y_attn, ug, mod, w_ssd_out, w_attn_out, w_pool_mix, pool_scale, w_pool_out, w_out,
           ts=512):
    b, s, d = x.shape
    tpb = TILES_PER_BRANCH
    assert POOL_WIDTH == tpb * UG_TILE and ts % POOL_HALO == 0
    tile = lambda w: pl.BlockSpec((1, ts, w), lambda i, j: (i, j, 0))
    full = lambda w: _layer_spec(w, l)
    branch = lambda t: pl.BlockSpec((1, tpb, ts, UG_TILE), lambda i, j: (i, t, j, 0))
    per = ts // POOL_HALO
    halo = pl.BlockSpec((1, tpb, POOL_HALO, UG_TILE), lambda i, j: (i, 0, jnp.maximum(j * per - 1, 0), 0))
    return pl.pallas_call(
        _merge_kernel,
        out_shape=jax.ShapeDtypeStruct((b, s, d), f32),
        grid=(b, s // ts),
        in_specs=[tile(d), tile(y_ssd.shape[-1]), tile(y_attn.shape[-1]),
                  branch(0), halo, branch(1), branch(2), branch(3),
                  _mod_spec(mod, l, MOD_GATE1),
                  full(w_ssd_out), full(w_attn_out), full(w_pool_mix),
                  pl.BlockSpec((1, POOL_WIDTH), lambda i, j: (0, 0)), full(w_pool_out), full(w_out)],
        out_specs=tile(d),
        scratch_shapes=[pltpu.VMEM((ts + POOL_HALO, POOL_WIDTH), f32),
                        pltpu.VMEM((2, ts + POOL_HALO, POOL_WIDTH), f32)],
        compiler_params=_cparams(("parallel", "parallel")),
        name="merge",
    )(x, y_ssd, y_attn, ug, ug, ug, ug, ug, mod, w_ssd_out, w_attn_out, w_pool_mix,
      pool_scale.reshape(1, POOL_WIDTH), w_pool_out, w_out)


def _ffn_kernel(x_ref, nw_ref, sc_ref, sh_ref, g2_ref, w1_ref, w2_ref, o_ref, hs, acc):
    k = pl.program_id(2)

    @pl.when(k == 0)
    def _():
        hs[...] = _modulated_rms(x_ref[0], nw_ref[...], _mod_row(sc_ref), _mod_row(sh_ref)).astype(bf16)
        acc[...] = jnp.zeros_like(acc)

    a = jnp.maximum(jnp.dot(hs[...], w1_ref[0], preferred_element_type=f32), 0.0)
    acc[...] += jnp.dot((a * a).astype(bf16), w2_ref[0], preferred_element_type=f32)

    @pl.when(k == pl.num_programs(2) - 1)
    def _():
        o_ref[0] = x_ref[0] + _mod_row(g2_ref) * acc[...]


def _ffn(l, x, nw, mod, w1, w2, ts=1024, tf=2048):
    b, s, d = x.shape
    ff = w1.shape[2]
    return pl.pallas_call(
        _ffn_kernel,
        out_shape=jax.ShapeDtypeStruct((b, s, d), f32),
        grid=(b, s // ts, ff // tf),
        in_specs=[pl.BlockSpec((1, ts, d), lambda i, j, k: (i, j, 0)),
                  pl.BlockSpec((1, d), lambda i, j, k: (0, 0)),
                  _mod_spec(mod, l, MOD_SCALE2), _mod_spec(mod, l, MOD_SHIFT2), _mod_spec(mod, l, MOD_GATE2),
                  pl.BlockSpec((1, d, tf), lambda i, j, k: (l, 0, k)),
                  pl.BlockSpec((1, tf, d), lambda i, j, k: (l, k, 0))],
        out_specs=pl.BlockSpec((1, ts, d), lambda i, j, k: (i, j, 0)),
        scratch_shapes=[pltpu.VMEM((ts, d), bf16), pltpu.VMEM((ts, d), f32)],
        compiler_params=_cparams(("parallel", "parallel", "arbitrary")),
        name="ffn",
    )(x, nw.reshape(1, d), mod, mod, mod, w1, w2)


def _repack_kernel(w_ref, z_ref, dt_ref, qkv_ref, ug_ref, *, dt0, q0, u0):
    ng = len(DILATED_GROUPS)
    cut = lambda r0, rows: w_ref[0, r0:r0 + rows, :].astype(bf16)
    z_ref[0] = cut(0, dt0)
    pad = jnp.zeros((dt_ref.shape[1] - (q0 - dt0), w_ref.shape[2]), bf16)
    dt_ref[0] = jnp.concatenate([cut(dt0, q0 - dt0), pad], axis=0)
    tw = qkv_ref.shape[2]
    for k in range(qkv_ref.shape[1]):
        qkv_ref[0, k] = cut(q0 + ((k % 3) * ng + k // 3) * tw, tw)
    tw = ug_ref.shape[2]
    for k in range(ug_ref.shape[1]):
        ug_ref[0, k] = cut(u0 + k * tw, tw)


def _repack(w_in, cols=256):
    depth, d, n = w_in.shape
    ng = len(DILATED_GROUPS)
    gcols = ATTN_HEADS_PER_GROUP * ATTN_HEAD_DIM
    dt0 = SSD_D_INNER + SSD_CONV_DIM
    q0 = dt0 + SSD_HEADS
    u0 = q0 + QKV_WIDTH
    assert u0 + UG_TILES * UG_TILE == n and SSD_HEADS % (2 * SUBLANES) == 0
    return pl.pallas_call(
        functools.partial(_repack_kernel, dt0=dt0, q0=q0, u0=u0),
        out_shape=(jax.ShapeDtypeStruct((depth, dt0, d), bf16),
                   jax.ShapeDtypeStruct((depth, LANES, d), bf16),
                   jax.ShapeDtypeStruct((depth, 3 * ng, gcols, d), bf16),
                   jax.ShapeDtypeStruct((depth, UG_TILES, UG_TILE, d), bf16)),
        grid=(depth, d // cols),
        in_specs=[pl.BlockSpec((1, n, cols), lambda l, c: (l, 0, c))],
        out_specs=(pl.BlockSpec((1, dt0, cols), lambda l, c: (l, 0, c)),
                   pl.BlockSpec((1, LANES, cols), lambda l, c: (l, 0, c)),
                   pl.BlockSpec((1, 3 * ng, gcols, cols), lambda l, c: (l, 0, 0, c)),
                   pl.BlockSpec((1, UG_TILES, UG_TILE, cols), lambda l, c: (l, 0, 0, c))),
        compiler_params=_cparams(("parallel", "parallel")),
        name="repack",
    )(jnp.swapaxes(w_in, 1, 2))


def kernel(x, c, w_ada, b_ada, norm1_w, norm2_w, w_in, conv_w, conv_b, dt_bias, a_log, d_skip, ssd_norm_w, w_ssd_out, q_norm_w, k_norm_w, w_attn_out, w_pool_mix, pool_scale, w_pool_out, w_out, w_ff1, w_ff2):
    b, s, d = x.shape
    depth = w_in.shape[0]
    ng = len(DILATED_GROUPS)
    mod = _adaln(c, w_ada, b_ada)
    mod = mod.reshape(depth, mod.shape[1], 6, d).transpose(0, 2, 1, 3)
    gcols = ATTN_HEADS_PER_GROUP * ATTN_HEAD_DIM
    w_z, w_dt, w_qkv, w_ug = _repack(w_in)
    per_group = jnp.concatenate([jnp.tile(q_norm_w * ATTN_HEAD_DIM ** -0.5, (1, ATTN_HEADS_PER_GROUP)),
                                 jnp.tile(k_norm_w, (1, ATTN_HEADS_PER_GROUP)),
                                 jnp.ones((depth, gcols), f32)], axis=-1)
    head_w = jnp.tile(per_group, (1, ng))[:, None, :]
    cast = lambda w: w.astype(bf16)
    w_ssd_out, w_attn_out, w_pool_mix, w_pool_out, w_out, w_ff1, w_ff2 = map(
        cast, (w_ssd_out, w_attn_out, w_pool_mix, w_pool_out, w_out, w_ff1, w_ff2))
    for l in range(depth):
        h, *qkv_groups = _qkv_proj(l, x, norm1_w[l], mod, w_qkv, head_w[l])
        y_ssd, ug = _ssd_proj(l, h, w_z, w_ug, w_dt, conv_w[l], conv_b[l], dt_bias[l], a_log[l],
                              d_skip[l], ssd_norm_w[l])
        y_attn = _attention(qkv_groups)
        x = _merge(l, x, y_ssd, y_attn, ug, mod, w_ssd_out, w_attn_out, w_pool_mix, pool_scale[l],
                   w_pool_out, w_out)
        x = _ffn(l, x, norm2_w[l], mod, w_ff1, w_ff2)
    return x
```

```python
import functools
import math

import numpy as np
import jax
import jax.numpy as jnp
from jax import lax
from jax.experimental import pallas as pl
from jax.experimental.pallas import tpu as pltpu

f32 = jnp.float32
bf16 = jnp.bfloat16

D_MODEL = 1024
SSD_D_INNER = 1024
SSD_HEAD_DIM = 64
SSD_HEADS = SSD_D_INNER // SSD_HEAD_DIM
SSD_GROUPS = 2
SSD_STATE = 128
SSD_CONV = 4
SSD_CHUNK = 128
SSD_BC = 2 * SSD_GROUPS * SSD_STATE
SSD_CONV_DIM = SSD_D_INNER + SSD_BC
ATTN_HEAD_DIM = 128
ATTN_HEADS_PER_GROUP = 4
DILATED_GROUPS = ((128, 1), (512, 4), (2048, 16))
ATTN_HEADS = ATTN_HEADS_PER_GROUP * len(DILATED_GROUPS)
ATTN_WIDTH = ATTN_HEADS * ATTN_HEAD_DIM
ATTN_OUT_WIDTH = ATTN_HEADS_PER_GROUP * ATTN_HEAD_DIM
POOL_WINDOWS = (2, 4, 8, 16)
POOL_WIDTH = 1024
POOL_GROUP_WIDTH = POOL_WIDTH // len(POOL_WINDOWS)
N_BRANCHES = 3
D_FF = 4 * D_MODEL
EPS = 1e-6
LOG2E = math.log2(math.e)

LANES = 128
SUBLANES = 8
VMEM_LIMIT_BYTES = 56 * 1024 * 1024

ATTN_GROUP_WIDTH = 3 * ATTN_HEADS_PER_GROUP * ATTN_HEAD_DIM
QKV_WIDTH = 3 * ATTN_WIDTH
UG_TILE = 512
UG_TILES = (POOL_WIDTH + N_BRANCHES * D_MODEL) // UG_TILE
TILES_PER_BRANCH = D_MODEL // UG_TILE
UG_PIECES = 2
POOL_HALO = 32
CONV_HALO = 8
ATTN_UNROLL = 32


def _alibi_slopes(n):
    def pow2(k):
        start = 2.0 ** (-8.0 / k)
        return [start ** (i + 1) for i in range(k)]
    if math.log2(n).is_integer():
        s = pow2(n)
    else:
        c = 2 ** math.floor(math.log2(n))
        s = pow2(c) + pow2(2 * c)[0::2][: n - c]
    return np.sort(np.asarray(s, np.float32))[::-1].copy()


def _cparams(sem):
    return pltpu.CompilerParams(dimension_semantics=sem, vmem_limit_bytes=VMEM_LIMIT_BYTES)


def _sigmoid(x):
    return 0.5 * (1.0 + jnp.tanh(0.5 * x))


def _silu(x):
    return x * _sigmoid(x)


def _dot_t(a, w_t):
    return lax.dot_general(a, w_t, (((1,), (1,)), ((), ())), preferred_element_type=f32)


def _modulated_rms(x, nw, sc, sh):
    y = x * lax.rsqrt(jnp.mean(x * x, axis=-1, keepdims=True) + EPS) * nw
    return y * (1.0 + sc) + sh


MOD_SHIFT1, MOD_SCALE1, MOD_GATE1, MOD_SHIFT2, MOD_SCALE2, MOD_GATE2 = range(6)


def _mod_spec(mod, l, which):
    return pl.BlockSpec((1, 1) + mod.shape[2:], lambda *_: (l, which, 0, 0))


def _mod_row(ref):
    return ref[0, 0, pl.ds(pl.program_id(0), 1), :]


def _layer_spec(w, l, **kw):
    return pl.BlockSpec((1,) + w.shape[1:], lambda *_: (l,) + (0,) * (w.ndim - 1), **kw)


def _adaln_kernel(c_ref, w_ref, b_ref, o_ref):
    cond = _silu(c_ref[...])
    o_ref[0] = jnp.dot(cond, w_ref[0], preferred_element_type=f32,
                       precision=lax.Precision.HIGHEST) + b_ref[0]


def _adaln(c, w_ada, b_ada):
    depth, d, n = w_ada.shape
    rows = SUBLANES * pl.cdiv(c.shape[0], SUBLANES)
    c_pad = jnp.pad(c, ((0, rows - c.shape[0]), (0, 0)))
    tn = n // 4
    return pl.pallas_call(
        _adaln_kernel,
        out_shape=jax.ShapeDtypeStruct((depth, rows, n), f32),
        grid=(depth, n // tn),
        in_specs=[pl.BlockSpec((rows, d), lambda l, j: (0, 0)),
                  pl.BlockSpec((1, d, tn), lambda l, j: (l, 0, j)),
                  pl.BlockSpec((1, 1, tn), lambda l, j: (l, 0, j))],
        out_specs=pl.BlockSpec((1, rows, tn), lambda l, j: (l, 0, j)),
        compiler_params=_cparams(("parallel", "parallel")),
        name="adaln",
    )(c_pad, w_ada, b_ada.reshape(depth, 1, n))


def _qkv_proj_kernel(x_ref, nw_ref, sc_ref, sh_ref, w_ref, hw_ref, h_ref, *rest):
    ng = len(DILATED_GROUPS)
    outs, (hf, hp) = rest[:ng], rest[ng:]
    e = ATTN_HEAD_DIM
    gw = ATTN_GROUP_WIDTH
    tm, d = x_ref.shape[1], x_ref.shape[2]
    hn = _modulated_rms(x_ref[0], nw_ref[...], _mod_row(sc_ref), _mod_row(sh_ref))
    h_ref[0] = hn.astype(bf16)
    for ct in range(d // LANES):
        hf[ct] = hn[:, ct * LANES:(ct + 1) * LANES]
    lhs = []
    for gi, (_, dil) in enumerate(DILATED_GROUPS):
        if dil == 1:
            lhs.append(h_ref.at[0])
            continue
        n = tm // dil
        slot = len([1 for _, dd in DILATED_GROUPS[:gi] if dd > 1])
        for ct in range(d // LANES):
            for r in range(dil):
                hp[slot, r * n:(r + 1) * n, ct * LANES:(ct + 1) * LANES] = (
                    hf[ct, pl.ds(r, n, stride=dil), :].astype(bf16))
        lhs.append(hp.at[slot])
    tw = w_ref.shape[2]
    for c in range(w_ref.shape[1] * tw // (2 * e)):
        cols = slice(c * 2 * e, (c + 1) * 2 * e)
        gi, lc = divmod(c * 2 * e, gw)
        dil = DILATED_GROUPS[gi][1]
        n = tm // dil
        t, tc = divmod(c * 2 * e, tw)
        y = _dot_t(lhs[gi][...], w_ref[0, t, tc:tc + 2 * e, :])
        if lc < 2 * gw // 3:
            heads = [y[:, half * e:(half + 1) * e] for half in range(2)]
            heads = [t * lax.rsqrt(jnp.mean(t * t, axis=-1, keepdims=True) + EPS) for t in heads]
            y = jnp.concatenate(heads, axis=1) * hw_ref[:, cols]
        y = y.astype(bf16)
        for r in range(dil):
            outs[gi][0, r, :, lc:lc + 2 * e] = y[r * n:(r + 1) * n, :]


def _qkv_proj(l, x, nw, mod, w, head_w, tm=1024):
    b, s, d = x.shape
    n = w.shape[1] * w.shape[2]
    gw = ATTN_GROUP_WIDTH
    dils = [dil for _, dil in DILATED_GROUPS]
    return pl.pallas_call(
        _qkv_proj_kernel,
        out_shape=[jax.ShapeDtypeStruct((b, s, d), bf16)]
                  + [jax.ShapeDtypeStruct((b, dil, s // dil, gw), bf16) for dil in dils],
        grid=(b, s // tm),
        in_specs=[pl.BlockSpec((1, tm, d), lambda i, j: (i, j, 0)),
                  pl.BlockSpec((1, d), lambda i, j: (0, 0)),
                  _mod_spec(mod, l, MOD_SCALE1), _mod_spec(mod, l, MOD_SHIFT1),
                  _layer_spec(w, l, pipeline_mode=pl.Buffered(1)),
                  pl.BlockSpec((1, n), lambda i, j: (0, 0))],
        out_specs=[pl.BlockSpec((1, tm, d), lambda i, j: (i, j, 0))]
                  + [pl.BlockSpec((1, dil, tm // dil, gw), lambda i, j: (i, 0, j, 0)) for dil in dils],
        scratch_shapes=[pltpu.VMEM((d // LANES, tm, LANES), f32),
                        pltpu.VMEM((len([1 for dil in dils if dil > 1]), tm, d), bf16)],
        compiler_params=_cparams(("parallel", "parallel")),
        name="qkv_proj",
    )(x, nw.reshape(1, d), mod, mod, w, head_w)


def _expand_heads(v):
    lane = lax.broadcasted_iota(jnp.int32, (v.shape[0], LANES), 1)
    tiles = [jnp.where(lane < SSD_HEAD_DIM, v[:, 2 * p:2 * p + 1], v[:, 2 * p + 1:2 * p + 2])
             for p in range(SSD_HEADS // 2)]
    return jnp.concatenate(tiles, axis=1)


def _ssd_chunk(h_c, z_c, xbc_c, wdt_ref, cw_ref, cb_ref, dtb_ref, alog_ref, dsk_ref, nw_ref, ext, conv, state,
               side=()):
    side = list(side)
    run_side = lambda: (side.pop(0) or (lambda: None))() if side else None
    L = SSD_CHUNK
    N = SSD_STATE
    DI = SSD_D_INNER
    GW = DI // SSD_GROUPS

    R = L // SUBLANES
    base = CONV_HALO - (SSD_CONV - 1)
    xbc_raw = xbc_c.astype(f32)
    run_side()
    for c in range(SSD_CONV_DIM // LANES):
        cols = slice(c * LANES, (c + 1) * LANES)
        ext[c, CONV_HALO:CONV_HALO + L, :] = xbc_raw[:, cols]
        taps = [jnp.broadcast_to(cw_ref[k:k + 1, cols], (SUBLANES, LANES)) for k in range(SSD_CONV)]
        bias = jnp.broadcast_to(cb_ref[:, cols], (SUBLANES, LANES))
        rows = [ext[c, pl.ds(base + j, SUBLANES, stride=R), :] for j in range(R + SSD_CONV - 1)]
        for v in range(R):
            acc = bias
            for k in range(SSD_CONV):
                acc = acc + rows[v + k] * taps[k]
            conv[c, pl.ds(v, SUBLANES, stride=R), :] = _silu(acc)
        ext[c, 0:CONV_HALO, :] = ext[c, L:L + CONV_HALO, :]
    xs = jnp.concatenate([conv[c] for c in range(DI // LANES)], axis=1)
    bc = jnp.concatenate([conv[c] for c in range(DI // LANES, SSD_CONV_DIM // LANES)], axis=1)

    dtr = _dot_t(h_c, wdt_ref[0]) + dtb_ref[...]
    dt = jnp.maximum(dtr, 0.0) + jnp.log1p(jnp.exp(-jnp.abs(dtr)))
    a2 = dt * (-LOG2E * jnp.exp(alog_ref[...]))
    ti = lax.broadcasted_iota(jnp.int32, (L, L), 0)
    si = lax.broadcasted_iota(jnp.int32, (L, L), 1)
    causal = ti >= si
    a_cum = jnp.dot(causal.astype(f32), a2, preferred_element_type=f32,
                    precision=lax.Precision.HIGHEST)
    a_cum_t = a_cum.T

    dt_e = _expand_heads(dt)
    acum_e = _expand_heads(a_cum)
    alast_e = acum_e[L - 1:L, :]
    x_dt = xs * dt_e
    x_dec = (x_dt * jnp.exp2(alast_e - acum_e)).astype(bf16)
    x_dt_b = x_dt.astype(bf16)
    e_acum = jnp.exp2(acum_e)
    e_alast = jnp.exp2(alast_e)
    lane = lax.broadcasted_iota(jnp.int32, (L, LANES), 1)
    lo = lane < SSD_HEAD_DIM
    zero_b = jnp.zeros((L, LANES), bf16)

    ys = []
    for g in range(SSD_GROUPS):
        run_side()
        b_g = bc[:, g * N:(g + 1) * N]
        c_g = bc[:, (SSD_GROUPS + g) * N:(SSD_GROUPS + g + 1) * N].astype(bf16)
        cb = lax.dot_general(c_g, b_g.astype(bf16), (((1,), (1,)), ((), ())),
                             preferred_element_type=f32)
        st = state[g]
        y_off = jnp.dot(c_g, st.astype(bf16), preferred_element_type=f32) * e_acum[:, g * GW:(g + 1) * GW]
        new = jnp.dot(b_g.T.astype(bf16), x_dec[:, g * GW:(g + 1) * GW], preferred_element_type=f32)
        state[g] = e_alast[:, g * GW:(g + 1) * GW] * st + new
        tiles = []
        for p in range(GW // LANES):
            col0 = g * GW + p * LANES
            xp = x_dt_b[:, col0:col0 + LANES]
            ms, xh = [], []
            for half in range(2):
                h = col0 // SSD_HEAD_DIM + half
                seg = a_cum[:, h:h + 1] - a_cum_t[h:h + 1, :]
                ms.append((cb * jnp.exp2(jnp.where(causal, seg, -jnp.inf))).astype(bf16))
                xh.append(jnp.where(lo if half == 0 else jnp.logical_not(lo), xp, zero_b))
            tiles.append(jnp.dot(jnp.concatenate(ms, axis=1), jnp.concatenate(xh, axis=0),
                                 preferred_element_type=f32))
        ys.append(jnp.concatenate(tiles, axis=1) + y_off)
    run_side()
    y = jnp.concatenate(ys, axis=1) + dsk_ref[...] * xs
    y = y * _silu(z_c.astype(f32))
    outs = []
    for g in range(SSD_GROUPS):
        yg = y[:, g * GW:(g + 1) * GW]
        outs.append(yg * lax.rsqrt(jnp.mean(yg * yg, axis=-1, keepdims=True) + EPS))
    return (jnp.concatenate(outs, axis=1) * nw_ref[...]).astype(bf16)


def _ssd_proj_kernel(h_ref, wz_ref, wug_ref, wdt_ref, cw_ref, cb_ref, dtb_ref, alog_ref, dsk_ref, nw_ref,
                     y_ref, ug_ref, zx, ext, conv, state):
    L = SSD_CHUNK
    DI = SSD_D_INNER

    @pl.when(pl.program_id(1) == 0)
    def _():
        ext[:, 0:CONV_HALO, :] = jnp.zeros((SSD_CONV_DIM // LANES, CONV_HALO, LANES), f32)
        state[...] = jnp.zeros_like(state)

    zx[...] = jnp.dot(h_ref[0], wz_ref[0], preferred_element_type=f32).astype(zx.dtype)

    def chunk(c, carry):
        rows = pl.ds(pl.multiple_of(c * L, L), L)
        def piece(n0, n1):
            def run():
                ug_ref[0, c, :, n0:n1] = jnp.dot(h_ref[0], wug_ref[0, c, :, n0:n1],
                                                 preferred_element_type=f32).astype(ug_ref.dtype)
            return run

        tw = ug_ref.shape[3]
        pieces = [piece(p * tw // UG_PIECES, (p + 1) * tw // UG_PIECES) for p in range(UG_PIECES)]
        side = [pieces[0], pieces[1]]
        y_ref[0, rows, :] = _ssd_chunk(h_ref[0, rows, :], zx[rows, 0:DI], zx[rows, DI:], wdt_ref, cw_ref,
                                       cb_ref, dtb_ref, alog_ref, dsk_ref, nw_ref, ext, conv, state, side)
        return carry

    lax.fori_loop(0, ug_ref.shape[1], chunk, 0)


def _pad_lanes(v):
    return jnp.pad(v, [(0, 0)] * (v.ndim - 1) + [(0, LANES - v.shape[-1])])


def _ssd_proj(l, h, w_z, w_ug, w_dt, conv_w, conv_b, dt_bias, a_log, d_skip, ssd_norm_w):
    b, s, d = h.shape
    L = SSD_CHUNK
    di = SSD_D_INNER
    _, tiles, _, tw = w_ug.shape
    tm = tiles * L
    row = lambda v: v.reshape(1, -1)
    once = dict(pipeline_mode=pl.Buffered(1))
    full = lambda shape: pl.BlockSpec(shape, lambda i, j: (0,) * len(shape))
    return pl.pallas_call(
        _ssd_proj_kernel,
        out_shape=(jax.ShapeDtypeStruct((b, s, di), bf16), jax.ShapeDtypeStruct((b, tiles, s, tw), bf16)),
        grid=(b, s // tm),
        in_specs=[pl.BlockSpec((1, tm, d), lambda i, j: (i, j, 0)),
                  _layer_spec(w_z, l, **once), _layer_spec(w_ug, l, **once), _layer_spec(w_dt, l),
                  full((SSD_CONV, SSD_CONV_DIM)), full((1, SSD_CONV_DIM)),
                  full((1, LANES)), full((1, LANES)), full((1, di)), full((1, di))],
        out_specs=(pl.BlockSpec((1, tm, di), lambda i, j: (i, j, 0)),
                   pl.BlockSpec((1, tiles, tm, tw), lambda i, j: (i, 0, j, 0))),
        scratch_shapes=[pltpu.VMEM((tm, di + SSD_CONV_DIM), bf16),
                        pltpu.VMEM((SSD_CONV_DIM // LANES, L + CONV_HALO, LANES), f32),
                        pltpu.VMEM((SSD_CONV_DIM // LANES, L, LANES), f32),
                        pltpu.VMEM((SSD_GROUPS, SSD_STATE, di // SSD_GROUPS), f32)],
        compiler_params=_cparams(("parallel", "arbitrary")),
        name="ssd_proj",
    )(h, w_z, w_ug, w_dt, conv_w, row(conv_b), row(_pad_lanes(dt_bias)), row(_pad_lanes(a_log)),
      row(jnp.repeat(d_skip, SSD_HEAD_DIM)), row(ssd_norm_w))


def _attn_group(gi, dil, slope, q, k, v, og, lse):
    seq = q.shape[0] * q.shape[1]
    steps = DILATED_GROUPS[gi][0] // dil
    e = ATTN_HEAD_DIM
    nb = seq // (dil * steps)
    qi = lax.broadcasted_iota(jnp.int32, (steps, 2 * steps), 0)
    kj = lax.broadcasted_iota(jnp.int32, (steps, 2 * steps), 1)
    neg = jnp.float32(-jnp.inf)
    rel = qi + steps - kj
    bias = jnp.where((rel >= 0) & (rel <= steps), (-slope * dil) * rel.astype(f32), neg)
    bias_first = jnp.where(kj >= steps, bias, neg)
    ones = jnp.ones((2 * steps, e), bf16)
    nt = (((1,), (1,)), ((), ()))

    def block(it):
        r = it // nb
        i = it % nb
        rows = pl.ds(pl.multiple_of(it * steps, steps), steps)
        band = pl.ds(pl.multiple_of(it * steps, steps), 2 * steps)
        vb = jnp.concatenate([v[band, :], ones], axis=1)
        qb = q[r, pl.ds(pl.multiple_of(i * steps, steps), steps), :]
        s = lax.dot_general(qb, k[band, :], nt, preferred_element_type=f32)
        s = s + jnp.where(i > 0, bias, bias_first)
        m = jnp.max(s, axis=-1, keepdims=True)
        p = jnp.exp(s - m).astype(bf16)
        oe = jnp.dot(p, vb, preferred_element_type=f32)
        l = oe[:, e:]
        out_rows = rows if dil == 1 else pl.ds(r + i * (steps * dil), steps, stride=dil)
        og[gi, out_rows, :] = oe[:, :e] / l
        lse[gi, out_rows, :] = m + jnp.log(l)

    def body(it, carry):
        for u in range(ATTN_UNROLL):
            block(it * ATTN_UNROLL + u)
        return carry

    lax.fori_loop(0, dil * nb // ATTN_UNROLL, body, 0)


def _attn_kernel(slopes_ref, *refs):
    ng = len(DILATED_GROUPS)
    ins, o_ref = refs[:3 * ng], refs[3 * ng]
    stage, og, lse = refs[3 * ng + 1:]
    j = pl.program_id(1)
    seq = o_ref.shape[1]
    rt = 512

    pad = stage.shape[1] - seq
    stage[:, 0:pad, :] = jnp.zeros((2, pad, stage.shape[2]), stage.dtype)
    for gi, (_, dil) in enumerate(DILATED_GROUPS):
        q, k, v = ins[3 * gi:3 * gi + 3]
        n = seq // dil
        for a, part in enumerate((k, v)):
            for r in range(dil):
                stage[a, pad + r * n:pad + (r + 1) * n, :] = part[0, r]
        _attn_group(gi, dil, slopes_ref[gi, j], q.at[0], stage.at[0], stage.at[1], og, lse)

    def merge(t, carry):
        rows = pl.ds(pl.multiple_of(t * rt, rt), rt)
        ls = [lse[gi, rows, :] for gi in range(ng)]
        top = functools.reduce(jnp.maximum, ls)
        ws = [jnp.exp(x - top) for x in ls]
        num = functools.reduce(jnp.add, [w * og[gi, rows, :] for gi, w in enumerate(ws)])
        o_ref[0, rows, :] = (num / functools.reduce(jnp.add, ws)).astype(o_ref.dtype)
        return carry

    lax.fori_loop(0, seq // rt, merge, 0)


def _attention(groups):
    e = ATTN_HEAD_DIM
    hg = ATTN_HEADS_PER_GROUP
    ng = len(DILATED_GROUPS)
    b = groups[0].shape[0]
    s = groups[0].shape[1] * groups[0].shape[2]
    slopes = jnp.asarray(_alibi_slopes(ATTN_HEADS)).reshape(ng, hg)
    args, specs = [], []
    for arr, (win, dil) in zip(groups, DILATED_GROUPS):
        assert s % win == 0 and win // dil == DILATED_GROUPS[0][0]
        assert (s // (win // dil)) % ATTN_UNROLL == 0
        for a in range(3):
            args.append(arr)
            specs.append(pl.BlockSpec((1, dil, s // dil, e), lambda i, j, a=a: (i, 0, 0, a * hg + j)))
    return pl.pallas_call(
        _attn_kernel,
        out_shape=jax.ShapeDtypeStruct((b, s, ATTN_OUT_WIDTH), bf16),
        grid=(b, hg),
        in_specs=[pl.BlockSpec(memory_space=pltpu.SMEM)] + specs,
        out_specs=pl.BlockSpec((1, s, e), lambda i, j: (i, 0, j)),
        scratch_shapes=[pltpu.VMEM((2, s + DILATED_GROUPS[0][0], e), bf16)] + [pltpu.VMEM((ng, s, e), f32)] * 2,
        compiler_params=_cparams(("parallel", "parallel")),
        name="dilated_attn",
    )(slopes, *args)


def _join_tiles(ref):
    return jnp.concatenate([ref[0, t] for t in range(ref.shape[1])], axis=1).astype(f32)


def _pooled(u_ref, halo_ref, wmix_ref, scale_ref, ext, buf):
    ts = u_ref.shape[2]
    gw = POOL_GROUP_WIDTH
    n = POOL_HALO + ts
    i = pl.program_id(1)
    halo = _join_tiles(halo_ref)
    ext[0:POOL_HALO, :] = jnp.where(i > 0, halo, jnp.zeros_like(halo))
    ext[POOL_HALO:n, :] = _join_tiles(u_ref)
    src, w, lo = ext, 1, 0
    for gi, win in enumerate(POOL_WINDOWS):
        assert win == 2 * w
        lo = SUBLANES * pl.cdiv(lo + w, SUBLANES)
        dst = buf.at[gi % 2]
        dst[lo:n, gi * gw:] = src[lo:n, gi * gw:] + src[pl.ds(lo - w, n - lo), gi * gw:]
        src, w = dst, win
    assert lo <= POOL_HALO
    pos = i * ts + lax.broadcasted_iota(jnp.int32, (ts, 1), 0)
    outs = []
    for gi, win in enumerate(POOL_WINDOWS):
        cols = slice(gi * gw, (gi + 1) * gw)
        count = jnp.minimum(pos + 1, win).astype(f32)
        pooled = buf[gi % 2, POOL_HALO:n, cols] / count - ext[POOL_HALO:n, cols]
        outs.append(jnp.dot(pooled.astype(bf16), wmix_ref[gi], preferred_element_type=f32))
    return (jnp.concatenate(outs, axis=1) * scale_ref[...]).astype(bf16)


def _merge_kernel(x_ref, ys_ref, ya_ref, u_ref, halo_ref, gs_ref, ga_ref, gp_ref, g1_ref,
                  ws_ref, wa_ref, wmix_ref, pscale_ref, wp_ref, wo_ref, o_ref, ext, buf):
    dot = functools.partial(jnp.dot, preferred_element_type=f32)
    gate = lambda ref: _sigmoid(_join_tiles(ref))
    y_pool = _pooled(u_ref, halo_ref, wmix_ref.at[0], pscale_ref, ext, buf)
    m = (gate(gs_ref) * dot(ys_ref[0], ws_ref[0])
         + gate(ga_ref) * dot(ya_ref[0], wa_ref[0])
         + gate(gp_ref) * dot(y_pool, wp_ref[0]))
    o_ref[0] = x_ref[0] + _mod_row(g1_ref) * dot(m.astype(bf16), wo_ref[0])


def _merge(l, x, y_ssd, y_attn, ug, mod, w_ssd_out, w_attn_out, w_pool_mix, pool_scale, w_pool_out, w_out,
           ts=512):
    b, s, d = x.shape
    tpb = TILES_PER_BRANCH
    assert POOL_WIDTH == tpb * UG_TILE and ts % POOL_HALO == 0
    tile = lambda w: pl.BlockSpec((1, ts, w), lambda i, j: (i, j, 0))
    full = lambda w: _layer_spec(w, l)
    branch = lambda t: pl.BlockSpec((1, tpb, ts, UG_TILE), lambda i, j: (i, t, j, 0))
    per = ts // POOL_HALO
    halo = pl.BlockSpec((1, tpb, POOL_HALO, UG_TILE), lambda i, j: (i, 0, jnp.maximum(j * per - 1, 0), 0))
    return pl.pallas_call(
        _merge_kernel,
        out_shape=jax.ShapeDtypeStruct((b, s, d), f32),
        grid=(b, s // ts),
        in_specs=[tile(d), tile(y_ssd.shape[-1]), tile(y_attn.shape[-1]),
                  branch(0), halo, branch(1), branch(2), branch(3),
                  _mod_spec(mod, l, MOD_GATE1),
                  full(w_ssd_out), full(w_attn_out), full(w_pool_mix),
                  pl.BlockSpec((1, POOL_WIDTH), lambda i, j: (0, 0)), full(w_pool_out), full(w_out)],
        out_specs=tile(d),
        scratch_shapes=[pltpu.VMEM((ts + POOL_HALO, POOL_WIDTH), f32),
                        pltpu.VMEM((2, ts + POOL_HALO, POOL_WIDTH), f32)],
        compiler_params=_cparams(("parallel", "parallel")),
        name="merge",
    )(x, y_ssd, y_attn, ug, ug, ug, ug, ug, mod, w_ssd_out, w_attn_out, w_pool_mix,
      pool_scale.reshape(1, POOL_WIDTH), w_pool_out, w_out)


def _ffn_kernel(x_ref, nw_ref, sc_ref, sh_ref, g2_ref, w1_ref, w2_ref, o_ref, hs, acc):
    k = pl.program_id(2)

    @pl.when(k == 0)
    def _():
        hs[...] = _modulated_rms(x_ref[0], nw_ref[...], _mod_row(sc_ref), _mod_row(sh_ref)).astype(bf16)
        acc[...] = jnp.zeros_like(acc)

    a = jnp.maximum(jnp.dot(hs[...], w1_ref[0], preferred_element_type=f32), 0.0)
    acc[...] += jnp.dot((a * a).astype(bf16), w2_ref[0], preferred_element_type=f32)

    @pl.when(k == pl.num_programs(2) - 1)
    def _():
        o_ref[0] = x_ref[0] + _mod_row(g2_ref) * acc[...]


def _ffn(l, x, nw, mod, w1, w2, ts=1024, tf=2048):
    b, s, d = x.shape
    ff = w1.shape[2]
    return pl.pallas_call(
        _ffn_kernel,
        out_shape=jax.ShapeDtypeStruct((b, s, d), f32),
        grid=(b, s // ts, ff // tf),
        in_specs=[pl.BlockSpec((1, ts, d), lambda i, j, k: (i, j, 0)),
                  pl.BlockSpec((1, d), lambda i, j, k: (0, 0)),
                  _mod_spec(mod, l, MOD_SCALE2), _mod_spec(mod, l, MOD_SHIFT2), _mod_spec(mod, l, MOD_GATE2),
                  pl.BlockSpec((1, d, tf), lambda i, j, k: (l, 0, k)),
                  pl.BlockSpec((1, tf, d), lambda i, j, k: (l, k, 0))],
        out_specs=pl.BlockSpec((1, ts, d), lambda i, j, k: (i, j, 0)),
        scratch_shapes=[pltpu.VMEM((ts, d), bf16), pltpu.VMEM((ts, d), f32)],
        compiler_params=_cparams(("parallel", "parallel", "arbitrary")),
        name="ffn",
    )(x, nw.reshape(1, d), mod, mod, mod, w1, w2)


def _repack_kernel(w_ref, z_ref, dt_ref, qkv_ref, ug_ref, *, dt0, q0, u0):
    ng = len(DILATED_GROUPS)
    cut = lambda r0, rows: w_ref[0, r0:r0 + rows, :].astype(bf16)
    cut_t = lambda r0, rows: w_ref[0, r0:r0 + rows, :].T.astype(bf16)
    tw = ug_ref.shape[3]
    for k in range(dt0 // tw):
        z_ref[0, :, k * tw:(k + 1) * tw] = cut_t(k * tw, tw)
    pad = jnp.zeros((dt_ref.shape[1] - (q0 - dt0), w_ref.shape[2]), bf16)
    dt_ref[0] = jnp.concatenate([cut(dt0, q0 - dt0), pad], axis=0)
    for k in range(ug_ref.shape[1]):
        ug_ref[0, k] = cut_t(u0 + k * tw, tw)
    tw = qkv_ref.shape[2]
    for k in range(qkv_ref.shape[1]):
        qkv_ref[0, k] = cut(q0 + ((k % 3) * ng + k // 3) * tw, tw)


def _repack(w_in, cols=256):
    depth, d, n = w_in.shape
    ng = len(DILATED_GROUPS)
    gcols = ATTN_HEADS_PER_GROUP * ATTN_HEAD_DIM
    dt0 = SSD_D_INNER + SSD_CONV_DIM
    q0 = dt0 + SSD_HEADS
    u0 = q0 + QKV_WIDTH
    assert u0 + UG_TILES * UG_TILE == n and SSD_HEADS % (2 * SUBLANES) == 0
    return pl.pallas_call(
        functools.partial(_repack_kernel, dt0=dt0, q0=q0, u0=u0),
        out_shape=(jax.ShapeDtypeStruct((depth, d, dt0), bf16),
                   jax.ShapeDtypeStruct((depth, LANES, d), bf16),
                   jax.ShapeDtypeStruct((depth, 3 * ng, gcols, d), bf16),
                   jax.ShapeDtypeStruct((depth, UG_TILES, d, UG_TILE), bf16)),
        grid=(depth, d // cols),
        in_specs=[pl.BlockSpec((1, n, cols), lambda l, c: (l, 0, c))],
        out_specs=(pl.BlockSpec((1, cols, dt0), lambda l, c: (l, c, 0)),
                   pl.BlockSpec((1, LANES, cols), lambda l, c: (l, 0, c)),
                   pl.BlockSpec((1, 3 * ng, gcols, cols), lambda l, c: (l, 0, 0, c)),
                   pl.BlockSpec((1, UG_TILES, cols, UG_TILE), lambda l, c: (l, 0, c, 0))),
        compiler_params=_cparams(("parallel", "parallel")),
        name="repack",
    )(jnp.swapaxes(w_in, 1, 2))


def kernel(x, c, w_ada, b_ada, norm1_w, norm2_w, w_in, conv_w, conv_b, dt_bias, a_log, d_skip, ssd_norm_w, w_ssd_out, q_norm_w, k_norm_w, w_attn_out, w_pool_mix, pool_scale, w_pool_out, w_out, w_ff1, w_ff2):
    b, s, d = x.shape
    depth = w_in.shape[0]
    ng = len(DILATED_GROUPS)
    mod = _adaln(c, w_ada, b_ada)
    mod = mod.reshape(depth, mod.shape[1], 6, d).transpose(0, 2, 1, 3)
    gcols = ATTN_HEADS_PER_GROUP * ATTN_HEAD_DIM
    w_z, w_dt, w_qkv, w_ug = _repack(w_in)
    per_group = jnp.concatenate([jnp.tile(q_norm_w * ATTN_HEAD_DIM ** -0.5, (1, ATTN_HEADS_PER_GROUP)),
                                 jnp.tile(k_norm_w, (1, ATTN_HEADS_PER_GROUP)),
                                 jnp.ones((depth, gcols), f32)], axis=-1)
    head_w = jnp.tile(per_group, (1, ng))[:, None, :]
    cast = lambda w: w.astype(bf16)
    w_ssd_out, w_attn_out, w_pool_mix, w_pool_out, w_out, w_ff1, w_ff2 = map(
        cast, (w_ssd_out, w_attn_out, w_pool_mix, w_pool_out, w_out, w_ff1, w_ff2))
    for l in range(depth):
        h, *qkv_groups = _qkv_proj(l, x, norm1_w[l], mod, w_qkv, head_w[l])
        y_ssd, ug = _ssd_proj(l, h, w_z, w_ug, w_dt, conv_w[l], conv_b[l], dt_bias[l], a_log[l],
                              d_skip[l], ssd_norm_w[l])
        y_attn = _attention(qkv_groups)
        x = _merge(l, x, y_ssd, y_attn, ug, mod, w_ssd_out, w_attn_out, w_pool_mix, pool_scale[l],
                   w_pool_out, w_out)
        x = _ffn(l, x, norm2_w[l], mod, w_ff1, w_ff2)
    return x
```

```python
import functools
import math

import numpy as np
import jax
import jax.numpy as jnp
from jax import lax
from jax.experimental import pallas as pl
from jax.experimental.pallas import tpu as pltpu

f32 = jnp.float32
bf16 = jnp.bfloat16

D_MODEL = 1024
SSD_D_INNER = 1024
SSD_HEAD_DIM = 64
SSD_HEADS = SSD_D_INNER // SSD_HEAD_DIM
SSD_GROUPS = 2
SSD_STATE = 128
SSD_CONV = 4
SSD_CHUNK = 128
SSD_BC = 2 * SSD_GROUPS * SSD_STATE
SSD_CONV_DIM = SSD_D_INNER + SSD_BC
ATTN_HEAD_DIM = 128
ATTN_HEADS_PER_GROUP = 4
DILATED_GROUPS = ((128, 1), (512, 4), (2048, 16))
ATTN_HEADS = ATTN_HEADS_PER_GROUP * len(DILATED_GROUPS)
ATTN_WIDTH = ATTN_HEADS * ATTN_HEAD_DIM
ATTN_OUT_WIDTH = ATTN_HEADS_PER_GROUP * ATTN_HEAD_DIM
POOL_WINDOWS = (2, 4, 8, 16)
POOL_WIDTH = 1024
POOL_GROUP_WIDTH = POOL_WIDTH // len(POOL_WINDOWS)
N_BRANCHES = 3
D_FF = 4 * D_MODEL
EPS = 1e-6
LOG2E = math.log2(math.e)

LANES = 128
SUBLANES = 8
VMEM_LIMIT_BYTES = 56 * 1024 * 1024

ATTN_GROUP_WIDTH = 3 * ATTN_HEADS_PER_GROUP * ATTN_HEAD_DIM
QKV_WIDTH = 3 * ATTN_WIDTH
UG_TILE = 512
UG_TILES = (POOL_WIDTH + N_BRANCHES * D_MODEL) // UG_TILE
TILES_PER_BRANCH = D_MODEL // UG_TILE
UG_PIECES = 2
POOL_HALO = 32
CONV_HALO = 8
ATTN_UNROLL = 32


def _alibi_slopes(n):
    def pow2(k):
        start = 2.0 ** (-8.0 / k)
        return [start ** (i + 1) for i in range(k)]
    if math.log2(n).is_integer():
        s = pow2(n)
    else:
        c = 2 ** math.floor(math.log2(n))
        s = pow2(c) + pow2(2 * c)[0::2][: n - c]
    return np.sort(np.asarray(s, np.float32))[::-1].copy()


def _cparams(sem):
    return pltpu.CompilerParams(dimension_semantics=sem, vmem_limit_bytes=VMEM_LIMIT_BYTES)


def _sigmoid(x):
    return 0.5 * (1.0 + jnp.tanh(0.5 * x))


def _silu(x):
    return x * _sigmoid(x)


def _dot_t(a, w_t):
    return lax.dot_general(a, w_t, (((1,), (1,)), ((), ())), preferred_element_type=f32)


def _modulated_rms(x, nw, sc, sh):
    y = x * lax.rsqrt(jnp.mean(x * x, axis=-1, keepdims=True) + EPS) * nw
    return y * (1.0 + sc) + sh


MOD_SHIFT1, MOD_SCALE1, MOD_GATE1, MOD_SHIFT2, MOD_SCALE2, MOD_GATE2 = range(6)


def _mod_spec(mod, l, which):
    return pl.BlockSpec((1, 1) + mod.shape[2:], lambda *_: (l, which, 0, 0))


def _mod_row(ref):
    return ref[0, 0, pl.ds(pl.program_id(0), 1), :]


def _layer_spec(w, l, **kw):
    return pl.BlockSpec((1,) + w.shape[1:], lambda *_: (l,) + (0,) * (w.ndim - 1), **kw)


def _adaln_kernel(c_ref, w_ref, b_ref, o_ref):
    cond = _silu(c_ref[...])
    o_ref[0] = jnp.dot(cond, w_ref[0], preferred_element_type=f32,
                       precision=lax.Precision.HIGHEST) + b_ref[0]


def _adaln(c, w_ada, b_ada):
    depth, d, n = w_ada.shape
    rows = SUBLANES * pl.cdiv(c.shape[0], SUBLANES)
    c_pad = jnp.pad(c, ((0, rows - c.shape[0]), (0, 0)))
    tn = n // 2
    return pl.pallas_call(
        _adaln_kernel,
        out_shape=jax.ShapeDtypeStruct((depth, rows, n), f32),
        grid=(depth, n // tn),
        in_specs=[pl.BlockSpec((rows, d), lambda l, j: (0, 0)),
                  pl.BlockSpec((1, d, tn), lambda l, j: (l, 0, j)),
                  pl.BlockSpec((1, 1, tn), lambda l, j: (l, 0, j))],
        out_specs=pl.BlockSpec((1, rows, tn), lambda l, j: (l, 0, j)),
        compiler_params=_cparams(("parallel", "parallel")),
        name="adaln",
    )(c_pad, w_ada, b_ada.reshape(depth, 1, n))


def _qkv_proj_kernel(x_ref, nw_ref, sc_ref, sh_ref, w_ref, hw_ref, h_ref, *rest):
    ng = len(DILATED_GROUPS)
    outs, (hf, hp) = rest[:ng], rest[ng:]
    e = ATTN_HEAD_DIM
    gw = ATTN_GROUP_WIDTH
    tm, d = x_ref.shape[1], x_ref.shape[2]
    hn = _modulated_rms(x_ref[0], nw_ref[...], _mod_row(sc_ref), _mod_row(sh_ref))
    h_ref[0] = hn.astype(bf16)
    for ct in range(d // LANES):
        hf[ct] = hn[:, ct * LANES:(ct + 1) * LANES]
    lhs = []
    for gi, (_, dil) in enumerate(DILATED_GROUPS):
        if dil == 1:
            lhs.append(h_ref.at[0])
            continue
        n = tm // dil
        slot = len([1 for _, dd in DILATED_GROUPS[:gi] if dd > 1])
        for ct in range(d // LANES):
            for r in range(dil):
                hp[slot, r * n:(r + 1) * n, ct * LANES:(ct + 1) * LANES] = (
                    hf[ct, pl.ds(r, n, stride=dil), :].astype(bf16))
        lhs.append(hp.at[slot])
    tw = w_ref.shape[2]
    for c in range(w_ref.shape[1] * tw // (2 * e)):
        cols = slice(c * 2 * e, (c + 1) * 2 * e)
        gi, lc = divmod(c * 2 * e, gw)
        dil = DILATED_GROUPS[gi][1]
        n = tm // dil
        t, tc = divmod(c * 2 * e, tw)
        y = _dot_t(lhs[gi][...], w_ref[0, t, tc:tc + 2 * e, :])
        if lc < 2 * gw // 3:
            heads = [y[:, half * e:(half + 1) * e] for half in range(2)]
            heads = [t * lax.rsqrt(jnp.mean(t * t, axis=-1, keepdims=True) + EPS) for t in heads]
            y = jnp.concatenate(heads, axis=1) * hw_ref[:, cols]
        y = y.astype(bf16)
        for r in range(dil):
            outs[gi][0, r, :, lc:lc + 2 * e] = y[r * n:(r + 1) * n, :]


def _qkv_proj(l, x, nw, mod, w, head_w, tm=1024):
    b, s, d = x.shape
    n = w.shape[1] * w.shape[2]
    gw = ATTN_GROUP_WIDTH
    dils = [dil for _, dil in DILATED_GROUPS]
    return pl.pallas_call(
        _qkv_proj_kernel,
        out_shape=[jax.ShapeDtypeStruct((b, s, d), bf16)]
                  + [jax.ShapeDtypeStruct((b, dil, s // dil, gw), bf16) for dil in dils],
        grid=(b, s // tm),
        in_specs=[pl.BlockSpec((1, tm, d), lambda i, j: (i, j, 0)),
                  pl.BlockSpec((1, d), lambda i, j: (0, 0)),
                  _mod_spec(mod, l, MOD_SCALE1), _mod_spec(mod, l, MOD_SHIFT1),
                  _layer_spec(w, l, pipeline_mode=pl.Buffered(1)),
                  pl.BlockSpec((1, n), lambda i, j: (0, 0))],
        out_specs=[pl.BlockSpec((1, tm, d), lambda i, j: (i, j, 0))]
                  + [pl.BlockSpec((1, dil, tm // dil, gw), lambda i, j: (i, 0, j, 0)) for dil in dils],
        scratch_shapes=[pltpu.VMEM((d // LANES, tm, LANES), f32),
                        pltpu.VMEM((len([1 for dil in dils if dil > 1]), tm, d), bf16)],
        compiler_params=_cparams(("parallel", "parallel")),
        name="qkv_proj",
    )(x, nw.reshape(1, d), mod, mod, w, head_w)


def _expand_heads(v):
    lane = lax.broadcasted_iota(jnp.int32, (v.shape[0], LANES), 1)
    tiles = [jnp.where(lane < SSD_HEAD_DIM, v[:, 2 * p:2 * p + 1], v[:, 2 * p + 1:2 * p + 2])
             for p in range(SSD_HEADS // 2)]
    return jnp.concatenate(tiles, axis=1)


def _ssd_chunk(h_c, z_c, xbc_c, wdt_ref, cw_ref, cb_ref, dtb_ref, alog_ref, dsk_ref, nw_ref, ext, conv, state,
               side=()):
    side = list(side)
    run_side = lambda: (side.pop(0) or (lambda: None))() if side else None
    L = SSD_CHUNK
    N = SSD_STATE
    DI = SSD_D_INNER
    GW = DI // SSD_GROUPS

    R = L // SUBLANES
    base = CONV_HALO - (SSD_CONV - 1)
    xbc_raw = xbc_c.astype(f32)
    run_side()
    for c in range(SSD_CONV_DIM // LANES):
        cols = slice(c * LANES, (c + 1) * LANES)
        ext[c, CONV_HALO:CONV_HALO + L, :] = xbc_raw[:, cols]
        taps = [jnp.broadcast_to(cw_ref[k:k + 1, cols], (SUBLANES, LANES)) for k in range(SSD_CONV)]
        bias = jnp.broadcast_to(cb_ref[:, cols], (SUBLANES, LANES))
        rows = [ext[c, pl.ds(base + j, SUBLANES, stride=R), :] for j in range(R + SSD_CONV - 1)]
        for v in range(R):
            acc = bias
            for k in range(SSD_CONV):
                acc = acc + rows[v + k] * taps[k]
            conv[c, pl.ds(v, SUBLANES, stride=R), :] = _silu(acc)
        ext[c, 0:CONV_HALO, :] = ext[c, L:L + CONV_HALO, :]
    xs = jnp.concatenate([conv[c] for c in range(DI // LANES)], axis=1)
    bc = jnp.concatenate([conv[c] for c in range(DI // LANES, SSD_CONV_DIM // LANES)], axis=1)

    dtr = _dot_t(h_c, wdt_ref[0]) + dtb_ref[...]
    dt = jnp.maximum(dtr, 0.0) + jnp.log1p(jnp.exp(-jnp.abs(dtr)))
    a2 = dt * (-LOG2E * jnp.exp(alog_ref[...]))
    ti = lax.broadcasted_iota(jnp.int32, (L, L), 0)
    si = lax.broadcasted_iota(jnp.int32, (L, L), 1)
    causal = ti >= si
    a_cum = jnp.dot(causal.astype(f32), a2, preferred_element_type=f32,
                    precision=lax.Precision.HIGHEST)
    a_cum_t = a_cum.T

    dt_e = _expand_heads(dt)
    acum_e = _expand_heads(a_cum)
    alast_e = acum_e[L - 1:L, :]
    x_dt = xs * dt_e
    x_dec = (x_dt * jnp.exp2(alast_e - acum_e)).astype(bf16)
    x_dt_b = x_dt.astype(bf16)
    e_acum = jnp.exp2(acum_e)
    e_alast = jnp.exp2(alast_e)
    lane = lax.broadcasted_iota(jnp.int32, (L, LANES), 1)
    lo = lane < SSD_HEAD_DIM
    zero_b = jnp.zeros((L, LANES), bf16)

    ys = []
    for g in range(SSD_GROUPS):
        run_side()
        b_g = bc[:, g * N:(g + 1) * N]
        c_g = bc[:, (SSD_GROUPS + g) * N:(SSD_GROUPS + g + 1) * N].astype(bf16)
        cb = lax.dot_general(c_g, b_g.astype(bf16), (((1,), (1,)), ((), ())),
                             preferred_element_type=f32)
        st = state[g]
        y_off = jnp.dot(c_g, st.astype(bf16), preferred_element_type=f32) * e_acum[:, g * GW:(g + 1) * GW]
        new = jnp.dot(b_g.T.astype(bf16), x_dec[:, g * GW:(g + 1) * GW], preferred_element_type=f32)
        state[g] = e_alast[:, g * GW:(g + 1) * GW] * st + new
        tiles = []
        for p in range(GW // LANES):
            col0 = g * GW + p * LANES
            xp = x_dt_b[:, col0:col0 + LANES]
            ms, xh = [], []
            for half in range(2):
                h = col0 // SSD_HEAD_DIM + half
                seg = a_cum[:, h:h + 1] - a_cum_t[h:h + 1, :]
                ms.append((cb * jnp.exp2(jnp.where(causal, seg, -jnp.inf))).astype(bf16))
                xh.append(jnp.where(lo if half == 0 else jnp.logical_not(lo), xp, zero_b))
            tiles.append(jnp.dot(jnp.concatenate(ms, axis=1), jnp.concatenate(xh, axis=0),
                                 preferred_element_type=f32))
        ys.append(jnp.concatenate(tiles, axis=1) + y_off)
    run_side()
    y = jnp.concatenate(ys, axis=1) + dsk_ref[...] * xs
    y = y * _silu(z_c.astype(f32))
    outs = []
    for g in range(SSD_GROUPS):
        yg = y[:, g * GW:(g + 1) * GW]
        outs.append(yg * lax.rsqrt(jnp.mean(yg * yg, axis=-1, keepdims=True) + EPS))
    return (jnp.concatenate(outs, axis=1) * nw_ref[...]).astype(bf16)


def _ssd_proj_kernel(h_ref, wz_ref, wug_ref, wdt_ref, cw_ref, cb_ref, dtb_ref, alog_ref, dsk_ref, nw_ref,
                     y_ref, ug_ref, zx, ext, conv, state):
    L = SSD_CHUNK
    DI = SSD_D_INNER

    @pl.when(pl.program_id(1) == 0)
    def _():
        ext[:, 0:CONV_HALO, :] = jnp.zeros((SSD_CONV_DIM // LANES, CONV_HALO, LANES), f32)
        state[...] = jnp.zeros_like(state)

    zx[...] = jnp.dot(h_ref[0], wz_ref[0], preferred_element_type=f32).astype(zx.dtype)

    def chunk(c, carry):
        rows = pl.ds(pl.multiple_of(c * L, L), L)
        def piece(n0, n1):
            def run():
                ug_ref[0, c, :, n0:n1] = jnp.dot(h_ref[0], wug_ref[0, c, :, n0:n1],
                                                 preferred_element_type=f32).astype(ug_ref.dtype)
            return run

        tw = ug_ref.shape[3]
        pieces = [piece(p * tw // UG_PIECES, (p + 1) * tw // UG_PIECES) for p in range(UG_PIECES)]
        side = [pieces[0], pieces[1]]
        y_ref[0, rows, :] = _ssd_chunk(h_ref[0, rows, :], zx[rows, 0:DI], zx[rows, DI:], wdt_ref, cw_ref,
                                       cb_ref, dtb_ref, alog_ref, dsk_ref, nw_ref, ext, conv, state, side)
        return carry

    lax.fori_loop(0, ug_ref.shape[1], chunk, 0)


def _pad_lanes(v):
    return jnp.pad(v, [(0, 0)] * (v.ndim - 1) + [(0, LANES - v.shape[-1])])


def _ssd_proj(l, h, w_z, w_ug, w_dt, conv_w, conv_b, dt_bias, a_log, d_skip, ssd_norm_w):
    b, s, d = h.shape
    L = SSD_CHUNK
    di = SSD_D_INNER
    _, tiles, _, tw = w_ug.shape
    tm = tiles * L
    row = lambda v: v.reshape(1, -1)
    once = dict(pipeline_mode=pl.Buffered(1))
    full = lambda shape: pl.BlockSpec(shape, lambda i, j: (0,) * len(shape))
    return pl.pallas_call(
        _ssd_proj_kernel,
        out_shape=(jax.ShapeDtypeStruct((b, s, di), bf16), jax.ShapeDtypeStruct((b, tiles, s, tw), bf16)),
        grid=(b, s // tm),
        in_specs=[pl.BlockSpec((1, tm, d), lambda i, j: (i, j, 0)),
                  _layer_spec(w_z, l, **once), _layer_spec(w_ug, l, **once), _layer_spec(w_dt, l),
                  full((SSD_CONV, SSD_CONV_DIM)), full((1, SSD_CONV_DIM)),
                  full((1, LANES)), full((1, LANES)), full((1, di)), full((1, di))],
        out_specs=(pl.BlockSpec((1, tm, di), lambda i, j: (i, j, 0)),
                   pl.BlockSpec((1, tiles, tm, tw), lambda i, j: (i, 0, j, 0))),
        scratch_shapes=[pltpu.VMEM((tm, di + SSD_CONV_DIM), bf16),
                        pltpu.VMEM((SSD_CONV_DIM // LANES, L + CONV_HALO, LANES), f32),
                        pltpu.VMEM((SSD_CONV_DIM // LANES, L, LANES), f32),
                        pltpu.VMEM((SSD_GROUPS, SSD_STATE, di // SSD_GROUPS), f32)],
        compiler_params=_cparams(("parallel", "arbitrary")),
        name="ssd_proj",
    )(h, w_z, w_ug, w_dt, conv_w, row(conv_b), row(_pad_lanes(dt_bias)), row(_pad_lanes(a_log)),
      row(jnp.repeat(d_skip, SSD_HEAD_DIM)), row(ssd_norm_w))


def _attn_group(gi, dil, slope, q, k, v, og, lse):
    seq = q.shape[0] * q.shape[1]
    steps = DILATED_GROUPS[gi][0] // dil
    e = ATTN_HEAD_DIM
    nb = seq // (dil * steps)
    qi = lax.broadcasted_iota(jnp.int32, (steps, 2 * steps), 0)
    kj = lax.broadcasted_iota(jnp.int32, (steps, 2 * steps), 1)
    neg = jnp.float32(-jnp.inf)
    rel = qi + steps - kj
    bias = jnp.where((rel >= 0) & (rel <= steps), (-slope * dil) * rel.astype(f32), neg)
    bias_first = jnp.where(kj >= steps, bias, neg)
    ones = jnp.ones((2 * steps, e), bf16)
    nt = (((1,), (1,)), ((), ()))

    def block(it):
        r = it // nb
        i = it % nb
        rows = pl.ds(pl.multiple_of(it * steps, steps), steps)
        band = pl.ds(pl.multiple_of(it * steps, steps), 2 * steps)
        vb = jnp.concatenate([v[band, :], ones], axis=1)
        qb = q[r, pl.ds(pl.multiple_of(i * steps, steps), steps), :]
        s = lax.dot_general(qb, k[band, :], nt, preferred_element_type=f32)
        s = s + jnp.where(i > 0, bias, bias_first)
        m = jnp.max(s, axis=-1, keepdims=True)
        p = jnp.exp(s - m).astype(bf16)
        oe = jnp.dot(p, vb, preferred_element_type=f32)
        l = oe[:, e:]
        out_rows = rows if dil == 1 else pl.ds(r + i * (steps * dil), steps, stride=dil)
        og[gi, out_rows, :] = oe[:, :e] / l
        lse[gi, out_rows, :] = m + jnp.log(l)

    def body(it, carry):
        for u in range(ATTN_UNROLL):
            block(it * ATTN_UNROLL + u)
        return carry

    lax.fori_loop(0, dil * nb // ATTN_UNROLL, body, 0)


def _attn_kernel(slopes_ref, *refs):
    ng = len(DILATED_GROUPS)
    ins, o_ref = refs[:3 * ng], refs[3 * ng]
    stage, og, lse = refs[3 * ng + 1:]
    j = pl.program_id(1)
    seq = o_ref.shape[1]
    rt = 512

    pad = stage.shape[1] - seq
    stage[:, 0:pad, :] = jnp.zeros((2, pad, stage.shape[2]), stage.dtype)
    for gi, (_, dil) in enumerate(DILATED_GROUPS):
        q, k, v = ins[3 * gi:3 * gi + 3]
        n = seq // dil
        for a, part in enumerate((k, v)):
            for r in range(dil):
                stage[a, pad + r * n:pad + (r + 1) * n, :] = part[0, r]
        _attn_group(gi, dil, slopes_ref[gi, j], q.at[0], stage.at[0], stage.at[1], og, lse)

    def merge(t, carry):
        rows = pl.ds(pl.multiple_of(t * rt, rt), rt)
        ls = [lse[gi, rows, :] for gi in range(ng)]
        top = functools.reduce(jnp.maximum, ls)
        ws = [jnp.exp(x - top) for x in ls]
        num = functools.reduce(jnp.add, [w * og[gi, rows, :] for gi, w in enumerate(ws)])
        o_ref[0, rows, :] = (num / functools.reduce(jnp.add, ws)).astype(o_ref.dtype)
        return carry

    lax.fori_loop(0, seq // rt, merge, 0)


def _attention(groups):
    e = ATTN_HEAD_DIM
    hg = ATTN_HEADS_PER_GROUP
    ng = len(DILATED_GROUPS)
    b = groups[0].shape[0]
    s = groups[0].shape[1] * groups[0].shape[2]
    slopes = jnp.asarray(_alibi_slopes(ATTN_HEADS)).reshape(ng, hg)
    args, specs = [], []
    for arr, (win, dil) in zip(groups, DILATED_GROUPS):
        assert s % win == 0 and win // dil == DILATED_GROUPS[0][0]
        assert (s // (win // dil)) % ATTN_UNROLL == 0
        for a in range(3):
            args.append(arr)
            specs.append(pl.BlockSpec((1, dil, s // dil, e), lambda i, j, a=a: (i, 0, 0, a * hg + j)))
    return pl.pallas_call(
        _attn_kernel,
        out_shape=jax.ShapeDtypeStruct((b, s, ATTN_OUT_WIDTH), bf16),
        grid=(b, hg),
        in_specs=[pl.BlockSpec(memory_space=pltpu.SMEM)] + specs,
        out_specs=pl.BlockSpec((1, s, e), lambda i, j: (i, 0, j)),
        scratch_shapes=[pltpu.VMEM((2, s + DILATED_GROUPS[0][0], e), bf16)] + [pltpu.VMEM((ng, s, e), f32)] * 2,
        compiler_params=_cparams(("parallel", "parallel")),
        name="dilated_attn",
    )(slopes, *args)


def _join_tiles(ref):
    return jnp.concatenate([ref[0, t] for t in range(ref.shape[1])], axis=1).astype(f32)


def _pooled(u_ref, halo_ref, wmix_ref, scale_ref, ext, buf):
    ts = u_ref.shape[2]
    gw = POOL_GROUP_WIDTH
    n = POOL_HALO + ts
    i = pl.program_id(1)
    halo = _join_tiles(halo_ref)
    ext[0:POOL_HALO, :] = jnp.where(i > 0, halo, jnp.zeros_like(halo))
    ext[POOL_HALO:n, :] = _join_tiles(u_ref)
    src, w, lo = ext, 1, 0
    for gi, win in enumerate(POOL_WINDOWS):
        assert win == 2 * w
        lo = SUBLANES * pl.cdiv(lo + w, SUBLANES)
        dst = buf.at[gi % 2]
        dst[lo:n, gi * gw:] = src[lo:n, gi * gw:] + src[pl.ds(lo - w, n - lo), gi * gw:]
        src, w = dst, win
    assert lo <= POOL_HALO
    pos = i * ts + lax.broadcasted_iota(jnp.int32, (ts, 1), 0)
    outs = []
    for gi, win in enumerate(POOL_WINDOWS):
        cols = slice(gi * gw, (gi + 1) * gw)
        count = jnp.minimum(pos + 1, win).astype(f32)
        pooled = buf[gi % 2, POOL_HALO:n, cols] / count - ext[POOL_HALO:n, cols]
        outs.append(jnp.dot(pooled.astype(bf16), wmix_ref[gi], preferred_element_type=f32))
    return (jnp.concatenate(outs, axis=1) * scale_ref[...]).astype(bf16)


def _merge_kernel(x_ref, ys_ref, ya_ref, u_ref, halo_ref, gs_ref, ga_ref, gp_ref, g1_ref,
                  ws_ref, wa_ref, wmix_ref, pscale_ref, wp_ref, wo_ref, o_ref, ext, buf):
    dot = functools.partial(jnp.dot, preferred_element_type=f32)
    gate = lambda ref: _sigmoid(_join_tiles(ref))
    y_pool = _pooled(u_ref, halo_ref, wmix_ref.at[0], pscale_ref, ext, buf)
    m = (gate(gs_ref) * dot(ys_ref[0], ws_ref[0])
         + gate(ga_ref) * dot(ya_ref[0], wa_ref[0])
         + gate(gp_ref) * dot(y_pool, wp_ref[0]))
    o_ref[0] = x_ref[0] + _mod_row(g1_ref) * dot(m.astype(bf16), wo_ref[0])


def _merge(l, x, y_ssd, y_attn, ug, mod, w_ssd_out, w_attn_out, w_pool_mix, pool_scale, w_pool_out, w_out,
           ts=512):
    b, s, d = x.shape
    tpb = TILES_PER_BRANCH
    assert POOL_WIDTH == tpb * UG_TILE and ts % POOL_HALO == 0
    tile = lambda w: pl.BlockSpec((1, ts, w), lambda i, j: (i, j, 0))
    full = lambda w: _layer_spec(w, l)
    branch = lambda t: pl.BlockSpec((1, tpb, ts, UG_TILE), lambda i, j: (i, t, j, 0))
    per = ts // POOL_HALO
    halo = pl.BlockSpec((1, tpb, POOL_HALO, UG_TILE), lambda i, j: (i, 0, jnp.maximum(j * per - 1, 0), 0))
    return pl.pallas_call(
        _merge_kernel,
        out_shape=jax.ShapeDtypeStruct((b, s, d), f32),
        grid=(b, s // ts),
        in_specs=[tile(d), tile(y_ssd.shape[-1]), tile(y_attn.shape[-1]),
                  branch(0), halo, branch(1), branch(2), branch(3),
                  _mod_spec(mod, l, MOD_GATE1),
                  full(w_ssd_out), full(w_attn_out), full(w_pool_mix),
                  pl.BlockSpec((1, POOL_WIDTH), lambda i, j: (0, 0)), full(w_pool_out), full(w_out)],
        out_specs=tile(d),
        scratch_shapes=[pltpu.VMEM((ts + POOL_HALO, POOL_WIDTH), f32),
                        pltpu.VMEM((2, ts + POOL_HALO, POOL_WIDTH), f32)],
        compiler_params=_cparams(("parallel", "parallel")),
        name="merge",
    )(x, y_ssd, y_attn, ug, ug, ug, ug, ug, mod, w_ssd_out, w_attn_out, w_pool_mix,
      pool_scale.reshape(1, POOL_WIDTH), w_pool_out, w_out)


def _ffn_kernel(x_ref, nw_ref, sc_ref, sh_ref, g2_ref, w1_ref, w2_ref, o_ref, hs, acc):
    k = pl.program_id(2)

    @pl.when(k == 0)
    def _():
        hs[...] = _modulated_rms(x_ref[0], nw_ref[...], _mod_row(sc_ref), _mod_row(sh_ref)).astype(bf16)
        acc[...] = jnp.zeros_like(acc)

    a = jnp.maximum(jnp.dot(hs[...], w1_ref[0], preferred_element_type=f32), 0.0)
    acc[...] += jnp.dot((a * a).astype(bf16), w2_ref[0], preferred_element_type=f32)

    @pl.when(k == pl.num_programs(2) - 1)
    def _():
        o_ref[0] = x_ref[0] + _mod_row(g2_ref) * acc[...]


def _ffn(l, x, nw, mod, w1, w2, ts=1024, tf=2048):
    b, s, d = x.shape
    ff = w1.shape[2]
    return pl.pallas_call(
        _ffn_kernel,
        out_shape=jax.ShapeDtypeStruct((b, s, d), f32),
        grid=(b, s // ts, ff // tf),
        in_specs=[pl.BlockSpec((1, ts, d), lambda i, j, k: (i, j, 0)),
                  pl.BlockSpec((1, d), lambda i, j, k: (0, 0)),
                  _mod_spec(mod, l, MOD_SCALE2), _mod_spec(mod, l, MOD_SHIFT2), _mod_spec(mod, l, MOD_GATE2),
                  pl.BlockSpec((1, d, tf), lambda i, j, k: (l, 0, k)),
                  pl.BlockSpec((1, tf, d), lambda i, j, k: (l, k, 0))],
        out_specs=pl.BlockSpec((1, ts, d), lambda i, j, k: (i, j, 0)),
        scratch_shapes=[pltpu.VMEM((ts, d), bf16), pltpu.VMEM((ts, d), f32)],
        compiler_params=_cparams(("parallel", "parallel", "arbitrary")),
        name="ffn",
    )(x, nw.reshape(1, d), mod, mod, mod, w1, w2)


def _repack_kernel(w_ref, z_ref, dt_ref, qkv_ref, ug_ref, *, dt0, q0, u0):
    ng = len(DILATED_GROUPS)
    cut = lambda r0, rows: w_ref[0, r0:r0 + rows, :].astype(bf16)
    cut_t = lambda r0, rows: w_ref[0, r0:r0 + rows, :].T.astype(bf16)
    tw = ug_ref.shape[3]
    for k in range(dt0 // tw):
        z_ref[0, :, k * tw:(k + 1) * tw] = cut_t(k * tw, tw)
    pad = jnp.zeros((dt_ref.shape[1] - (q0 - dt0), w_ref.shape[2]), bf16)
    dt_ref[0] = jnp.concatenate([cut(dt0, q0 - dt0), pad], axis=0)
    for k in range(ug_ref.shape[1]):
        ug_ref[0, k] = cut_t(u0 + k * tw, tw)
    tw = qkv_ref.shape[2]
    for k in range(qkv_ref.shape[1]):
        qkv_ref[0, k] = cut(q0 + ((k % 3) * ng + k // 3) * tw, tw)


def _repack(w_in, cols=256):
    depth, d, n = w_in.shape
    ng = len(DILATED_GROUPS)
    gcols = ATTN_HEADS_PER_GROUP * ATTN_HEAD_DIM
    dt0 = SSD_D_INNER + SSD_CONV_DIM
    q0 = dt0 + SSD_HEADS
    u0 = q0 + QKV_WIDTH
    assert u0 + UG_TILES * UG_TILE == n and SSD_HEADS % (2 * SUBLANES) == 0
    return pl.pallas_call(
        functools.partial(_repack_kernel, dt0=dt0, q0=q0, u0=u0),
        out_shape=(jax.ShapeDtypeStruct((depth, d, dt0), bf16),
                   jax.ShapeDtypeStruct((depth, LANES, d), bf16),
                   jax.ShapeDtypeStruct((depth, 3 * ng, gcols, d), bf16),
                   jax.ShapeDtypeStruct((depth, UG_TILES, d, UG_TILE), bf16)),
        grid=(depth, d // cols),
        in_specs=[pl.BlockSpec((1, n, cols), lambda l, c: (l, 0, c))],
        out_specs=(pl.BlockSpec((1, cols, dt0), lambda l, c: (l, c, 0)),
                   pl.BlockSpec((1, LANES, cols), lambda l, c: (l, 0, c)),
                   pl.BlockSpec((1, 3 * ng, gcols, cols), lambda l, c: (l, 0, 0, c)),
                   pl.BlockSpec((1, UG_TILES, cols, UG_TILE), lambda l, c: (l, 0, c, 0))),
        compiler_params=_cparams(("parallel", "parallel")),
        name="repack",
    )(jnp.swapaxes(w_in, 1, 2))


def kernel(x, c, w_ada, b_ada, norm1_w, norm2_w, w_in, conv_w, conv_b, dt_bias, a_log, d_skip, ssd_norm_w, w_ssd_out, q_norm_w, k_norm_w, w_attn_out, w_pool_mix, pool_scale, w_pool_out, w_out, w_ff1, w_ff2):
    b, s, d = x.shape
    depth = w_in.shape[0]
    ng = len(DILATED_GROUPS)
    mod = _adaln(c, w_ada, b_ada)
    mod = mod.reshape(depth, mod.shape[1], 6, d).transpose(0, 2, 1, 3)
    gcols = ATTN_HEADS_PER_GROUP * ATTN_HEAD_DIM
    w_z, w_dt, w_qkv, w_ug = _repack(w_in)
    per_group = jnp.concatenate([jnp.tile(q_norm_w * ATTN_HEAD_DIM ** -0.5, (1, ATTN_HEADS_PER_GROUP)),
                                 jnp.tile(k_norm_w, (1, ATTN_HEADS_PER_GROUP)),
                                 jnp.ones((depth, gcols), f32)], axis=-1)
    head_w = jnp.tile(per_group, (1, ng))[:, None, :]
    cast = lambda w: w.astype(bf16)
    w_ssd_out, w_attn_out, w_pool_mix, w_pool_out, w_out, w_ff1, w_ff2 = map(
        cast, (w_ssd_out, w_attn_out, w_pool_mix, w_pool_out, w_out, w_ff1, w_ff2))
    for l in range(depth):
        h, *qkv_groups = _qkv_proj(l, x, norm1_w[l], mod, w_qkv, head_w[l])
        y_ssd, ug = _ssd_proj(l, h, w_z, w_ug, w_dt, conv_w[l], conv_b[l], dt_bias[l], a_log[l],
                              d_skip[l], ssd_norm_w[l])
        y_attn = _attention(qkv_groups)
        x = _merge(l, x, y_ssd, y_attn, ug, mod, w_ssd_out, w_attn_out, w_pool_mix, pool_scale[l],
                   w_pool_out, w_out)
        x = _ffn(l, x, norm2_w[l], mod, w_ff1, w_ff2)
    return x
```

```python
import functools
import math

import numpy as np
import jax
import jax.numpy as jnp
from jax import lax
from jax.experimental import pallas as pl
from jax.experimental.pallas import tpu as pltpu

f32 = jnp.float32
bf16 = jnp.bfloat16

D_MODEL = 1024
SSD_D_INNER = 1024
SSD_HEAD_DIM = 64
SSD_HEADS = SSD_D_INNER // SSD_HEAD_DIM
SSD_GROUPS = 2
SSD_STATE = 128
SSD_CONV = 4
SSD_CHUNK = 128
SSD_BC = 2 * SSD_GROUPS * SSD_STATE
SSD_CONV_DIM = SSD_D_INNER + SSD_BC
ATTN_HEAD_DIM = 128
ATTN_HEADS_PER_GROUP = 4
DILATED_GROUPS = ((128, 1), (512, 4), (2048, 16))
ATTN_HEADS = ATTN_HEADS_PER_GROUP * len(DILATED_GROUPS)
ATTN_WIDTH = ATTN_HEADS * ATTN_HEAD_DIM
ATTN_OUT_WIDTH = ATTN_HEADS_PER_GROUP * ATTN_HEAD_DIM
POOL_WINDOWS = (2, 4, 8, 16)
POOL_WIDTH = 1024
POOL_GROUP_WIDTH = POOL_WIDTH // len(POOL_WINDOWS)
N_BRANCHES = 3
D_FF = 4 * D_MODEL
EPS = 1e-6
LOG2E = math.log2(math.e)

LANES = 128
SUBLANES = 8
VMEM_LIMIT_BYTES = 56 * 1024 * 1024

ATTN_GROUP_WIDTH = 3 * ATTN_HEADS_PER_GROUP * ATTN_HEAD_DIM
QKV_WIDTH = 3 * ATTN_WIDTH
UG_TILE = 512
UG_TILES = (POOL_WIDTH + N_BRANCHES * D_MODEL) // UG_TILE
TILES_PER_BRANCH = D_MODEL // UG_TILE
UG_PIECES = 2
POOL_HALO = 32
CONV_HALO = 8
ATTN_UNROLL = 32


def _alibi_slopes(n):
    def pow2(k):
        start = 2.0 ** (-8.0 / k)
        return [start ** (i + 1) for i in range(k)]
    if math.log2(n).is_integer():
        s = pow2(n)
    else:
        c = 2 ** math.floor(math.log2(n))
        s = pow2(c) + pow2(2 * c)[0::2][: n - c]
    return np.sort(np.asarray(s, np.float32))[::-1].copy()


def _cparams(sem):
    return pltpu.CompilerParams(dimension_semantics=sem, vmem_limit_bytes=VMEM_LIMIT_BYTES)


def _sigmoid(x):
    return 0.5 * (1.0 + jnp.tanh(0.5 * x))


def _silu(x):
    return x * _sigmoid(x)


def _dot_t(a, w_t):
    return lax.dot_general(a, w_t, (((1,), (1,)), ((), ())), preferred_element_type=f32)


def _modulated_rms(x, nw, sc, sh):
    y = x * lax.rsqrt(jnp.mean(x * x, axis=-1, keepdims=True) + EPS) * nw
    return y * (1.0 + sc) + sh


MOD_SHIFT1, MOD_SCALE1, MOD_GATE1, MOD_SHIFT2, MOD_SCALE2, MOD_GATE2 = range(6)


def _mod_spec(mod, l, which):
    return pl.BlockSpec((1, 1) + mod.shape[2:], lambda *_: (l, which, 0, 0))


def _mod_row(ref):
    return ref[0, 0, pl.ds(pl.program_id(0), 1), :]


def _layer_spec(w, l, **kw):
    return pl.BlockSpec((1,) + w.shape[1:], lambda *_: (l,) + (0,) * (w.ndim - 1), **kw)


def _adaln_kernel(c_ref, w_ref, b_ref, o_ref):
    cond = _silu(c_ref[...])
    o_ref[0] = jnp.dot(cond, w_ref[0], preferred_element_type=f32,
                       precision=lax.Precision.HIGHEST) + b_ref[0]


def _adaln(c, w_ada, b_ada):
    depth, d, n = w_ada.shape
    rows = SUBLANES * pl.cdiv(c.shape[0], SUBLANES)
    c_pad = jnp.pad(c, ((0, rows - c.shape[0]), (0, 0)))
    tn = n // 2
    return pl.pallas_call(
        _adaln_kernel,
        out_shape=jax.ShapeDtypeStruct((depth, rows, n), f32),
        grid=(depth, n // tn),
        in_specs=[pl.BlockSpec((rows, d), lambda l, j: (0, 0)),
                  pl.BlockSpec((1, d, tn), lambda l, j: (l, 0, j)),
                  pl.BlockSpec((1, 1, tn), lambda l, j: (l, 0, j))],
        out_specs=pl.BlockSpec((1, rows, tn), lambda l, j: (l, 0, j)),
        compiler_params=_cparams(("parallel", "parallel")),
        name="adaln",
    )(c_pad, w_ada, b_ada.reshape(depth, 1, n))


def _qkv_proj_kernel(x_ref, nw_ref, sc_ref, sh_ref, w_ref, hw_ref, h_ref, *rest):
    ng = len(DILATED_GROUPS)
    outs, (hf, hp) = rest[:ng], rest[ng:]
    e = ATTN_HEAD_DIM
    gw = ATTN_GROUP_WIDTH
    tm, d = x_ref.shape[1], x_ref.shape[2]
    hn = _modulated_rms(x_ref[0], nw_ref[...], _mod_row(sc_ref), _mod_row(sh_ref))
    h_ref[0] = hn.astype(bf16)
    for ct in range(d // LANES):
        hf[ct] = hn[:, ct * LANES:(ct + 1) * LANES]
    lhs = []
    for gi, (_, dil) in enumerate(DILATED_GROUPS):
        if dil == 1:
            lhs.append(h_ref.at[0])
            continue
        n = tm // dil
        slot = len([1 for _, dd in DILATED_GROUPS[:gi] if dd > 1])
        for ct in range(d // LANES):
            for r in range(dil):
                hp[slot, r * n:(r + 1) * n, ct * LANES:(ct + 1) * LANES] = (
                    hf[ct, pl.ds(r, n, stride=dil), :].astype(bf16))
        lhs.append(hp.at[slot])
    tw = w_ref.shape[2]
    for c in range(w_ref.shape[1] * tw // (2 * e)):
        cols = slice(c * 2 * e, (c + 1) * 2 * e)
        gi, lc = divmod(c * 2 * e, gw)
        dil = DILATED_GROUPS[gi][1]
        n = tm // dil
        t, tc = divmod(c * 2 * e, tw)
        y = _dot_t(lhs[gi][...], w_ref[0, t, tc:tc + 2 * e, :])
        if lc < 2 * gw // 3:
            heads = [y[:, half * e:(half + 1) * e] for half in range(2)]
            heads = [t * lax.rsqrt(jnp.mean(t * t, axis=-1, keepdims=True) + EPS) for t in heads]
            y = jnp.concatenate(heads, axis=1) * hw_ref[:, cols]
        y = y.astype(bf16)
        for r in range(dil):
            outs[gi][0, r, :, lc:lc + 2 * e] = y[r * n:(r + 1) * n, :]


def _qkv_proj(l, x, nw, mod, w, head_w, tm=1024):
    b, s, d = x.shape
    n = w.shape[1] * w.shape[2]
    gw = ATTN_GROUP_WIDTH
    dils = [dil for _, dil in DILATED_GROUPS]
    return pl.pallas_call(
        _qkv_proj_kernel,
        out_shape=[jax.ShapeDtypeStruct((b, s, d), bf16)]
                  + [jax.ShapeDtypeStruct((b, dil, s // dil, gw), bf16) for dil in dils],
        grid=(b, s // tm),
        in_specs=[pl.BlockSpec((1, tm, d), lambda i, j: (i, j, 0)),
                  pl.BlockSpec((1, d), lambda i, j: (0, 0)),
                  _mod_spec(mod, l, MOD_SCALE1), _mod_spec(mod, l, MOD_SHIFT1),
                  _layer_spec(w, l, pipeline_mode=pl.Buffered(1)),
                  pl.BlockSpec((1, n), lambda i, j: (0, 0))],
        out_specs=[pl.BlockSpec((1, tm, d), lambda i, j: (i, j, 0))]
                  + [pl.BlockSpec((1, dil, tm // dil, gw), lambda i, j: (i, 0, j, 0)) for dil in dils],
        scratch_shapes=[pltpu.VMEM((d // LANES, tm, LANES), f32),
                        pltpu.VMEM((len([1 for dil in dils if dil > 1]), tm, d), bf16)],
        compiler_params=_cparams(("parallel", "parallel")),
        name="qkv_proj",
    )(x, nw.reshape(1, d), mod, mod, w, head_w)


def _expand_heads(v):
    lane = lax.broadcasted_iota(jnp.int32, (v.shape[0], LANES), 1)
    tiles = [jnp.where(lane < SSD_HEAD_DIM, v[:, 2 * p:2 * p + 1], v[:, 2 * p + 1:2 * p + 2])
             for p in range(SSD_HEADS // 2)]
    return jnp.concatenate(tiles, axis=1)


def _ssd_chunk(h_c, z_c, xbc_c, wdt_ref, cw_ref, cb_ref, dtb_ref, alog_ref, dsk_ref, nw_ref, ext, conv, state,
               side=()):
    side = list(side)
    run_side = lambda: (side.pop(0) or (lambda: None))() if side else None
    L = SSD_CHUNK
    N = SSD_STATE
    DI = SSD_D_INNER
    GW = DI // SSD_GROUPS

    R = L // SUBLANES
    base = CONV_HALO - (SSD_CONV - 1)
    xbc_raw = xbc_c.astype(f32)
    run_side()
    for c in range(SSD_CONV_DIM // LANES):
        cols = slice(c * LANES, (c + 1) * LANES)
        ext[c, CONV_HALO:CONV_HALO + L, :] = xbc_raw[:, cols]
        taps = [jnp.broadcast_to(cw_ref[k:k + 1, cols], (SUBLANES, LANES)) for k in range(SSD_CONV)]
        bias = jnp.broadcast_to(cb_ref[:, cols], (SUBLANES, LANES))
        rows = [ext[c, pl.ds(base + j, SUBLANES, stride=R), :] for j in range(R + SSD_CONV - 1)]
        for v in range(R):
            acc = bias
            for k in range(SSD_CONV):
                acc = acc + rows[v + k] * taps[k]
            conv[c, pl.ds(v, SUBLANES, stride=R), :] = _silu(acc)
        ext[c, 0:CONV_HALO, :] = ext[c, L:L + CONV_HALO, :]
    xs = jnp.concatenate([conv[c] for c in range(DI // LANES)], axis=1)
    bc = jnp.concatenate([conv[c] for c in range(DI // LANES, SSD_CONV_DIM // LANES)], axis=1)

    dtr = _dot_t(h_c, wdt_ref[0]) + dtb_ref[...]
    dt = jnp.maximum(dtr, 0.0) + jnp.log1p(jnp.exp(-jnp.abs(dtr)))
    a2 = dt * (-LOG2E * jnp.exp(alog_ref[...]))
    ti = lax.broadcasted_iota(jnp.int32, (L, L), 0)
    si = lax.broadcasted_iota(jnp.int32, (L, L), 1)
    causal = ti >= si
    a_cum = jnp.dot(causal.astype(f32), a2, preferred_element_type=f32,
                    precision=lax.Precision.HIGHEST)
    a_cum_t = a_cum.T

    dt_e = _expand_heads(dt)
    acum_e = _expand_heads(a_cum)
    alast_e = acum_e[L - 1:L, :]
    x_dt = xs * dt_e
    x_dec = (x_dt * jnp.exp2(alast_e - acum_e)).astype(bf16)
    x_dt_b = x_dt.astype(bf16)
    e_acum = jnp.exp2(acum_e)
    e_alast = jnp.exp2(alast_e)
    lane = lax.broadcasted_iota(jnp.int32, (L, LANES), 1)
    lo = lane < SSD_HEAD_DIM
    zero_b = jnp.zeros((L, LANES), bf16)

    ys = []
    for g in range(SSD_GROUPS):
        run_side()
        b_g = bc[:, g * N:(g + 1) * N]
        c_g = bc[:, (SSD_GROUPS + g) * N:(SSD_GROUPS + g + 1) * N].astype(bf16)
        cb = lax.dot_general(c_g, b_g.astype(bf16), (((1,), (1,)), ((), ())),
                             preferred_element_type=f32)
        st = state[g]
        y_off = jnp.dot(c_g, st.astype(bf16), preferred_element_type=f32) * e_acum[:, g * GW:(g + 1) * GW]
        new = jnp.dot(b_g.T.astype(bf16), x_dec[:, g * GW:(g + 1) * GW], preferred_element_type=f32)
        state[g] = e_alast[:, g * GW:(g + 1) * GW] * st + new
        tiles = []
        for p in range(GW // LANES):
            col0 = g * GW + p * LANES
            xp = x_dt_b[:, col0:col0 + LANES]
            ms, xh = [], []
            for half in range(2):
                h = col0 // SSD_HEAD_DIM + half
                seg = a_cum[:, h:h + 1] - a_cum_t[h:h + 1, :]
                ms.append((cb * jnp.exp2(jnp.where(causal, seg, -jnp.inf))).astype(bf16))
                xh.append(jnp.where(lo if half == 0 else jnp.logical_not(lo), xp, zero_b))
            tiles.append(jnp.dot(jnp.concatenate(ms, axis=1), jnp.concatenate(xh, axis=0),
                                 preferred_element_type=f32))
        ys.append(jnp.concatenate(tiles, axis=1) + y_off)
    run_side()
    y = jnp.concatenate(ys, axis=1) + dsk_ref[...] * xs
    y = y * _silu(z_c.astype(f32))
    outs = []
    for g in range(SSD_GROUPS):
        yg = y[:, g * GW:(g + 1) * GW]
        outs.append(yg * lax.rsqrt(jnp.mean(yg * yg, axis=-1, keepdims=True) + EPS))
    return (jnp.concatenate(outs, axis=1) * nw_ref[...]).astype(bf16)


def _ssd_proj_kernel(h_ref, wz_ref, wug_ref, wdt_ref, cw_ref, cb_ref, dtb_ref, alog_ref, dsk_ref, nw_ref,
                     y_ref, ug_ref, zx, ext, conv, state):
    L = SSD_CHUNK
    DI = SSD_D_INNER

    @pl.when(pl.program_id(1) == 0)
    def _():
        ext[:, 0:CONV_HALO, :] = jnp.zeros((SSD_CONV_DIM // LANES, CONV_HALO, LANES), f32)
        state[...] = jnp.zeros_like(state)

    zx[...] = jnp.dot(h_ref[0], wz_ref[0], preferred_element_type=f32).astype(zx.dtype)

    def chunk(c, carry):
        rows = pl.ds(pl.multiple_of(c * L, L), L)
        def piece(n0, n1):
            def run():
                ug_ref[0, c, :, n0:n1] = jnp.dot(h_ref[0], wug_ref[0, c, :, n0:n1],
                                                 preferred_element_type=f32).astype(ug_ref.dtype)
            return run

        tw = ug_ref.shape[3]
        pieces = [piece(p * tw // UG_PIECES, (p + 1) * tw // UG_PIECES) for p in range(UG_PIECES)]
        side = [pieces[0], pieces[1]]
        y_ref[0, rows, :] = _ssd_chunk(h_ref[0, rows, :], zx[rows, 0:DI], zx[rows, DI:], wdt_ref, cw_ref,
                                       cb_ref, dtb_ref, alog_ref, dsk_ref, nw_ref, ext, conv, state, side)
        return carry

    lax.fori_loop(0, ug_ref.shape[1], chunk, 0, unroll=2)


def _pad_lanes(v):
    return jnp.pad(v, [(0, 0)] * (v.ndim - 1) + [(0, LANES - v.shape[-1])])


def _ssd_proj(l, h, w_z, w_ug, w_dt, conv_w, conv_b, dt_bias, a_log, d_skip, ssd_norm_w):
    b, s, d = h.shape
    L = SSD_CHUNK
    di = SSD_D_INNER
    _, tiles, _, tw = w_ug.shape
    tm = tiles * L
    row = lambda v: v.reshape(1, -1)
    once = dict(pipeline_mode=pl.Buffered(1))
    full = lambda shape: pl.BlockSpec(shape, lambda i, j: (0,) * len(shape))
    return pl.pallas_call(
        _ssd_proj_kernel,
        out_shape=(jax.ShapeDtypeStruct((b, s, di), bf16), jax.ShapeDtypeStruct((b, tiles, s, tw), bf16)),
        grid=(b, s // tm),
        in_specs=[pl.BlockSpec((1, tm, d), lambda i, j: (i, j, 0)),
                  _layer_spec(w_z, l, **once), _layer_spec(w_ug, l, **once), _layer_spec(w_dt, l),
                  full((SSD_CONV, SSD_CONV_DIM)), full((1, SSD_CONV_DIM)),
                  full((1, LANES)), full((1, LANES)), full((1, di)), full((1, di))],
        out_specs=(pl.BlockSpec((1, tm, di), lambda i, j: (i, j, 0)),
                   pl.BlockSpec((1, tiles, tm, tw), lambda i, j: (i, 0, j, 0))),
        scratch_shapes=[pltpu.VMEM((tm, di + SSD_CONV_DIM), bf16),
                        pltpu.VMEM((SSD_CONV_DIM // LANES, L + CONV_HALO, LANES), f32),
                        pltpu.VMEM((SSD_CONV_DIM // LANES, L, LANES), f32),
                        pltpu.VMEM((SSD_GROUPS, SSD_STATE, di // SSD_GROUPS), f32)],
        compiler_params=_cparams(("parallel", "arbitrary")),
        name="ssd_proj",
    )(h, w_z, w_ug, w_dt, conv_w, row(conv_b), row(_pad_lanes(dt_bias)), row(_pad_lanes(a_log)),
      row(jnp.repeat(d_skip, SSD_HEAD_DIM)), row(ssd_norm_w))


def _attn_group(gi, dil, slope, q, k, v, og, lse):
    seq = q.shape[0] * q.shape[1]
    steps = DILATED_GROUPS[gi][0] // dil
    e = ATTN_HEAD_DIM
    nb = seq // (dil * steps)
    qi = lax.broadcasted_iota(jnp.int32, (steps, 2 * steps), 0)
    kj = lax.broadcasted_iota(jnp.int32, (steps, 2 * steps), 1)
    neg = jnp.float32(-jnp.inf)
    rel = qi + steps - kj
    bias = jnp.where((rel >= 0) & (rel <= steps), (-slope * dil) * rel.astype(f32), neg)
    bias_first = jnp.where(kj >= steps, bias, neg)
    ones = jnp.ones((2 * steps, e), bf16)
    nt = (((1,), (1,)), ((), ()))

    def block(it):
        r = it // nb
        i = it % nb
        rows = pl.ds(pl.multiple_of(it * steps, steps), steps)
        band = pl.ds(pl.multiple_of(it * steps, steps), 2 * steps)
        vb = jnp.concatenate([v[band, :], ones], axis=1)
        qb = q[r, pl.ds(pl.multiple_of(i * steps, steps), steps), :]
        s = lax.dot_general(qb, k[band, :], nt, preferred_element_type=f32)
        s = s + jnp.where(i > 0, bias, bias_first)
        m = jnp.max(s, axis=-1, keepdims=True)
        p = jnp.exp(s - m).astype(bf16)
        oe = jnp.dot(p, vb, preferred_element_type=f32)
        l = oe[:, e:]
        out_rows = rows if dil == 1 else pl.ds(r + i * (steps * dil), steps, stride=dil)
        og[gi, out_rows, :] = oe[:, :e] / l
        lse[gi, out_rows, :] = m + jnp.log(l)

    def body(it, carry):
        for u in range(ATTN_UNROLL):
            block(it * ATTN_UNROLL + u)
        return carry

    lax.fori_loop(0, dil * nb // ATTN_UNROLL, body, 0)


def _attn_kernel(slopes_ref, *refs):
    ng = len(DILATED_GROUPS)
    ins, o_ref = refs[:3 * ng], refs[3 * ng]
    stage, og, lse = refs[3 * ng + 1:]
    j = pl.program_id(1)
    seq = o_ref.shape[1]
    rt = 512

    pad = stage.shape[1] - seq
    stage[:, 0:pad, :] = jnp.zeros((2, pad, stage.shape[2]), stage.dtype)
    for gi, (_, dil) in enumerate(DILATED_GROUPS):
        q, k, v = ins[3 * gi:3 * gi + 3]
        n = seq // dil
        for a, part in enumerate((k, v)):
            for r in range(dil):
                stage[a, pad + r * n:pad + (r + 1) * n, :] = part[0, r]
        _attn_group(gi, dil, slopes_ref[gi, j], q.at[0], stage.at[0], stage.at[1], og, lse)

    def merge(t, carry):
        rows = pl.ds(pl.multiple_of(t * rt, rt), rt)
        ls = [lse[gi, rows, :] for gi in range(ng)]
        top = functools.reduce(jnp.maximum, ls)
        ws = [jnp.exp(x - top) for x in ls]
        num = functools.reduce(jnp.add, [w * og[gi, rows, :] for gi, w in enumerate(ws)])
        o_ref[0, rows, :] = (num / functools.reduce(jnp.add, ws)).astype(o_ref.dtype)
        return carry

    lax.fori_loop(0, seq // rt, merge, 0)


def _attention(groups):
    e = ATTN_HEAD_DIM
    hg = ATTN_HEADS_PER_GROUP
    ng = len(DILATED_GROUPS)
    b = groups[0].shape[0]
    s = groups[0].shape[1] * groups[0].shape[2]
    slopes = jnp.asarray(_alibi_slopes(ATTN_HEADS)).reshape(ng, hg)
    args, specs = [], []
    for arr, (win, dil) in zip(groups, DILATED_GROUPS):
        assert s % win == 0 and win // dil == DILATED_GROUPS[0][0]
        assert (s // (win // dil)) % ATTN_UNROLL == 0
        for a in range(3):
            args.append(arr)
            specs.append(pl.BlockSpec((1, dil, s // dil, e), lambda i, j, a=a: (i, 0, 0, a * hg + j)))
    return pl.pallas_call(
        _attn_kernel,
        out_shape=jax.ShapeDtypeStruct((b, s, ATTN_OUT_WIDTH), bf16),
        grid=(b, hg),
        in_specs=[pl.BlockSpec(memory_space=pltpu.SMEM)] + specs,
        out_specs=pl.BlockSpec((1, s, e), lambda i, j: (i, 0, j)),
        scratch_shapes=[pltpu.VMEM((2, s + DILATED_GROUPS[0][0], e), bf16)] + [pltpu.VMEM((ng, s, e), f32)] * 2,
        compiler_params=_cparams(("parallel", "parallel")),
        name="dilated_attn",
    )(slopes, *args)


def _join_tiles(ref):
    return jnp.concatenate([ref[0, t] for t in range(ref.shape[1])], axis=1).astype(f32)


def _pooled(u_ref, halo_ref, wmix_ref, scale_ref, ext, buf):
    ts = u_ref.shape[2]
    gw = POOL_GROUP_WIDTH
    n = POOL_HALO + ts
    i = pl.program_id(1)
    halo = _join_tiles(halo_ref)
    ext[0:POOL_HALO, :] = jnp.where(i > 0, halo, jnp.zeros_like(halo))
    ext[POOL_HALO:n, :] = _join_tiles(u_ref)
    src, w, lo = ext, 1, 0
    for gi, win in enumerate(POOL_WINDOWS):
        assert win == 2 * w
        lo = SUBLANES * pl.cdiv(lo + w, SUBLANES)
        dst = buf.at[gi % 2]
        dst[lo:n, gi * gw:] = src[lo:n, gi * gw:] + src[pl.ds(lo - w, n - lo), gi * gw:]
        src, w = dst, win
    assert lo <= POOL_HALO
    pos = i * ts + lax.broadcasted_iota(jnp.int32, (ts, 1), 0)
    outs = []
    for gi, win in enumerate(POOL_WINDOWS):
        cols = slice(gi * gw, (gi + 1) * gw)
        count = jnp.minimum(pos + 1, win).astype(f32)
        pooled = buf[gi % 2, POOL_HALO:n, cols] / count - ext[POOL_HALO:n, cols]
        outs.append(jnp.dot(pooled.astype(bf16), wmix_ref[gi], preferred_element_type=f32))
    return (jnp.concatenate(outs, axis=1) * scale_ref[...]).astype(bf16)


def _merge_kernel(x_ref, ys_ref, ya_ref, u_ref, halo_ref, gs_ref, ga_ref, gp_ref, g1_ref,
                  ws_ref, wa_ref, wmix_ref, pscale_ref, wp_ref, wo_ref, o_ref, ext, buf):
    dot = functools.partial(jnp.dot, preferred_element_type=f32)
    gate = lambda ref: _sigmoid(_join_tiles(ref))
    y_pool = _pooled(u_ref, halo_ref, wmix_ref.at[0], pscale_ref, ext, buf)
    m = (gate(gs_ref) * dot(ys_ref[0], ws_ref[0])
         + gate(ga_ref) * dot(ya_ref[0], wa_ref[0])
         + gate(gp_ref) * dot(y_pool, wp_ref[0]))
    o_ref[0] = x_ref[0] + _mod_row(g1_ref) * dot(m.astype(bf16), wo_ref[0])


def _merge(l, x, y_ssd, y_attn, ug, mod, w_ssd_out, w_attn_out, w_pool_mix, pool_scale, w_pool_out, w_out,
           ts=512):
    b, s, d = x.shape
    tpb = TILES_PER_BRANCH
    assert POOL_WIDTH == tpb * UG_TILE and ts % POOL_HALO == 0
    tile = lambda w: pl.BlockSpec((1, ts, w), lambda i, j: (i, j, 0))
    full = lambda w: _layer_spec(w, l)
    branch = lambda t: pl.BlockSpec((1, tpb, ts, UG_TILE), lambda i, j: (i, t, j, 0))
    per = ts // POOL_HALO
    halo = pl.BlockSpec((1, tpb, POOL_HALO, UG_TILE), lambda i, j: (i, 0, jnp.maximum(j * per - 1, 0), 0))
    return pl.pallas_call(
        _merge_kernel,
        out_shape=jax.ShapeDtypeStruct((b, s, d), f32),
        grid=(b, s // ts),
        in_specs=[tile(d), tile(y_ssd.shape[-1]), tile(y_attn.shape[-1]),
                  branch(0), halo, branch(1), branch(2), branch(3),
                  _mod_spec(mod, l, MOD_GATE1),
                  full(w_ssd_out), full(w_attn_out), full(w_pool_mix),
                  pl.BlockSpec((1, POOL_WIDTH), lambda i, j: (0, 0)), full(w_pool_out), full(w_out)],
        out_specs=tile(d),
        scratch_shapes=[pltpu.VMEM((ts + POOL_HALO, POOL_WIDTH), f32),
                        pltpu.VMEM((2, ts + POOL_HALO, POOL_WIDTH), f32)],
        compiler_params=_cparams(("parallel", "parallel")),
        name="merge",
    )(x, y_ssd, y_attn, ug, ug, ug, ug, ug, mod, w_ssd_out, w_attn_out, w_pool_mix,
      pool_scale.reshape(1, POOL_WIDTH), w_pool_out, w_out)


def _ffn_kernel(x_ref, nw_ref, sc_ref, sh_ref, g2_ref, w1_ref, w2_ref, o_ref, hs, acc):
    k = pl.program_id(2)

    @pl.when(k == 0)
    def _():
        hs[...] = _modulated_rms(x_ref[0], nw_ref[...], _mod_row(sc_ref), _mod_row(sh_ref)).astype(bf16)
        acc[...] = jnp.zeros_like(acc)

    a = jnp.maximum(jnp.dot(hs[...], w1_ref[0], preferred_element_type=f32), 0.0)
    acc[...] += jnp.dot((a * a).astype(bf16), w2_ref[0], preferred_element_type=f32)

    @pl.when(k == pl.num_programs(2) - 1)
    def _():
        o_ref[0] = x_ref[0] + _mod_row(g2_ref) * acc[...]


def _ffn(l, x, nw, mod, w1, w2, ts=1024, tf=2048):
    b, s, d = x.shape
    ff = w1.shape[2]
    return pl.pallas_call(
        _ffn_kernel,
        out_shape=jax.ShapeDtypeStruct((b, s, d), f32),
        grid=(b, s // ts, ff // tf),
        in_specs=[pl.BlockSpec((1, ts, d), lambda i, j, k: (i, j, 0)),
                  pl.BlockSpec((1, d), lambda i, j, k: (0, 0)),
                  _mod_spec(mod, l, MOD_SCALE2), _mod_spec(mod, l, MOD_SHIFT2), _mod_spec(mod, l, MOD_GATE2),
                  pl.BlockSpec((1, d, tf), lambda i, j, k: (l, 0, k)),
                  pl.BlockSpec((1, tf, d), lambda i, j, k: (l, k, 0))],
        out_specs=pl.BlockSpec((1, ts, d), lambda i, j, k: (i, j, 0)),
        scratch_shapes=[pltpu.VMEM((ts, d), bf16), pltpu.VMEM((ts, d), f32)],
        compiler_params=_cparams(("parallel", "parallel", "arbitrary")),
        name="ffn",
    )(x, nw.reshape(1, d), mod, mod, mod, w1, w2)


def _repack_kernel(w_ref, z_ref, dt_ref, qkv_ref, ug_ref, *, dt0, q0, u0):
    ng = len(DILATED_GROUPS)
    cut = lambda r0, rows: w_ref[0, r0:r0 + rows, :].astype(bf16)
    cut_t = lambda r0, rows: w_ref[0, r0:r0 + rows, :].T.astype(bf16)
    tw = ug_ref.shape[3]
    for k in range(dt0 // tw):
        z_ref[0, :, k * tw:(k + 1) * tw] = cut_t(k * tw, tw)
    pad = jnp.zeros((dt_ref.shape[1] - (q0 - dt0), w_ref.shape[2]), bf16)
    dt_ref[0] = jnp.concatenate([cut(dt0, q0 - dt0), pad], axis=0)
    for k in range(ug_ref.shape[1]):
        ug_ref[0, k] = cut_t(u0 + k * tw, tw)
    tw = qkv_ref.shape[2]
    for k in range(qkv_ref.shape[1]):
        qkv_ref[0, k] = cut(q0 + ((k % 3) * ng + k // 3) * tw, tw)


def _repack(w_in, cols=256):
    depth, d, n = w_in.shape
    ng = len(DILATED_GROUPS)
    gcols = ATTN_HEADS_PER_GROUP * ATTN_HEAD_DIM
    dt0 = SSD_D_INNER + SSD_CONV_DIM
    q0 = dt0 + SSD_HEADS
    u0 = q0 + QKV_WIDTH
    assert u0 + UG_TILES * UG_TILE == n and SSD_HEADS % (2 * SUBLANES) == 0
    return pl.pallas_call(
        functools.partial(_repack_kernel, dt0=dt0, q0=q0, u0=u0),
        out_shape=(jax.ShapeDtypeStruct((depth, d, dt0), bf16),
                   jax.ShapeDtypeStruct((depth, LANES, d), bf16),
                   jax.ShapeDtypeStruct((depth, 3 * ng, gcols, d), bf16),
                   jax.ShapeDtypeStruct((depth, UG_TILES, d, UG_TILE), bf16)),
        grid=(depth, d // cols),
        in_specs=[pl.BlockSpec((1, n, cols), lambda l, c: (l, 0, c))],
        out_specs=(pl.BlockSpec((1, cols, dt0), lambda l, c: (l, c, 0)),
                   pl.BlockSpec((1, LANES, cols), lambda l, c: (l, 0, c)),
                   pl.BlockSpec((1, 3 * ng, gcols, cols), lambda l, c: (l, 0, 0, c)),
                   pl.BlockSpec((1, UG_TILES, cols, UG_TILE), lambda l, c: (l, 0, c, 0))),
        compiler_params=_cparams(("parallel", "parallel")),
        name="repack",
    )(jnp.swapaxes(w_in, 1, 2))


def kernel(x, c, w_ada, b_ada, norm1_w, norm2_w, w_in, conv_w, conv_b, dt_bias, a_log, d_skip, ssd_norm_w, w_ssd_out, q_norm_w, k_norm_w, w_attn_out, w_pool_mix, pool_scale, w_pool_out, w_out, w_ff1, w_ff2):
    b, s, d = x.shape
    depth = w_in.shape[0]
    ng = len(DILATED_GROUPS)
    mod = _adaln(c, w_ada, b_ada)
    mod = mod.reshape(depth, mod.shape[1], 6, d).transpose(0, 2, 1, 3)
    gcols = ATTN_HEADS_PER_GROUP * ATTN_HEAD_DIM
    w_z, w_dt, w_qkv, w_ug = _repack(w_in)
    per_group = jnp.concatenate([jnp.tile(q_norm_w * ATTN_HEAD_DIM ** -0.5, (1, ATTN_HEADS_PER_GROUP)),
                                 jnp.tile(k_norm_w, (1, ATTN_HEADS_PER_GROUP)),
                                 jnp.ones((depth, gcols), f32)], axis=-1)
    head_w = jnp.tile(per_group, (1, ng))[:, None, :]
    cast = lambda w: w.astype(bf16)
    w_ssd_out, w_attn_out, w_pool_mix, w_pool_out, w_out, w_ff1, w_ff2 = map(
        cast, (w_ssd_out, w_attn_out, w_pool_mix, w_pool_out, w_out, w_ff1, w_ff2))
    for l in range(depth):
        h, *qkv_groups = _qkv_proj(l, x, norm1_w[l], mod, w_qkv, head_w[l])
        y_ssd, ug = _ssd_proj(l, h, w_z, w_ug, w_dt, conv_w[l], conv_b[l], dt_bias[l], a_log[l],
                              d_skip[l], ssd_norm_w[l])
        y_attn = _attention(qkv_groups)
        x = _merge(l, x, y_ssd, y_attn, ug, mod, w_ssd_out, w_attn_out, w_pool_mix, pool_scale[l],
                   w_pool_out, w_out)
        x = _ffn(l, x, norm2_w[l], mod, w_ff1, w_ff2)
    return x
```
